```python
import jax
import jax.numpy as jnp
from jax import lax
import numpy as np

D_MODEL = 2048
BATCH = 4
SEQ = 2048
DEPTH = 1
DEC_BATCH = 128
DEC_SEQ = 1
PAST_LEN = 2048
PAGE_SIZE = 128

HEAD_DIM = 64
NSA_HEADS = D_MODEL // (2 * HEAD_DIM)
NSA_KV_HEADS = max(1, NSA_HEADS // 4)
NSA_GROUP = NSA_HEADS // NSA_KV_HEADS
MOBA_HEADS = D_MODEL // (2 * HEAD_DIM)
NSA_WIDTH = NSA_HEADS * HEAD_DIM
NSA_KV_WIDTH = NSA_KV_HEADS * HEAD_DIM
MOBA_WIDTH = MOBA_HEADS * HEAD_DIM
N_GATE = 3 * NSA_HEADS
N_IN_COLS = NSA_WIDTH + 6 * NSA_KV_WIDTH + N_GATE + 3 * MOBA_WIDTH
CMP_LEN = 32
CMP_STRIDE = 16
CMP_HIDDEN = 64
SEL_BLOCK = 64
SEL_TOPK = 8
WINDOW = 512
MOBA_BLOCK = 256
MOBA_TOPK = 3
PEER_KEYS = 128
PEER_EXPERTS = PEER_KEYS * PEER_KEYS
PEER_HEADS = 8
PEER_TOPK = 16
PEER_QDIM = 256
PLE_DIM = 256
QBLOCK = 128
MOBA_QBLOCK = 64
PEER_TBLOCK = 128
RMS_EPS = 1e-6
NEG = -1e30
OWN_SCORE = 1e9
SCALE = HEAD_DIM ** -0.5

kernel_name = 'hymba_nsa_moba_peer_decode_step'


def rms_norm(x, g):
    xf = x.astype(jnp.float32)
    y = xf * lax.rsqrt(jnp.mean(xf * xf, axis=-1, keepdims=True) + RMS_EPS)
    return (y * g.astype(jnp.float32)).astype(x.dtype)


def alibi_slopes():
    n = NSA_HEADS + MOBA_HEADS
    s = jnp.exp2(-8.0 * jnp.arange(1, n + 1, dtype=jnp.float32) / n)
    return s[0::2], s[1::2]


def pad_rows(a, length):
    extra = length - a.shape[1]
    if extra == 0:
        return a
    return jnp.pad(a, [(0, 0), (0, extra)] + [(0, 0)] * (a.ndim - 2))


def masked_softmax(s, mask):
    s = jnp.where(mask, s, NEG)
    m = jnp.max(s, axis=-1, keepdims=True)
    e = jnp.where(mask, jnp.exp(s - m), 0.0)
    l = jnp.sum(e, axis=-1, keepdims=True)
    safe = jnp.where(l > 0, l, 1.0)
    lse = jnp.where(l > 0, m + jnp.log(safe), NEG)
    return e / safe, lse[..., 0]


def merge_partials(o1, lse1, o2, lse2):
    lse = jnp.logaddexp(lse1, lse2)
    return jnp.exp(lse1 - lse)[..., None] * o1 + jnp.exp(lse2 - lse)[..., None] * o2


def sweep_queries(fn, block, q_pos, *xs):
    t = q_pos.shape[0]
    if t <= block or t % block:
        return fn(q_pos, *xs)
    nb = t // block

    def split(a):
        return jnp.moveaxis(a.reshape((a.shape[0], nb, block) + a.shape[2:]), 1, 0)

    def join(a):
        a = jnp.moveaxis(a, 0, 1)
        return a.reshape((a.shape[0], t) + a.shape[3:])

    outs = lax.map(lambda args: fn(*args), (q_pos.reshape(nb, block),) + tuple(split(a) for a in xs))
    return jax.tree_util.tree_map(join, outs)


def gathered_block_attention(q, k_blocks, v_blocks, idx, valid, q_pos, slopes, block):
    b, t, g, r, dk = q.shape
    n = idx.shape[-1]
    bi = jnp.arange(b)[:, None, None, None]
    gi = jnp.arange(g)[None, None, :, None]
    kg = k_blocks[bi, idx, :, gi].reshape(b, t, g, n * block, dk)
    vg = v_blocks[bi, idx, :, gi].reshape(b, t, g, n * block, dk)
    k_pos = (idx[..., None] * block + jnp.arange(block)).reshape(b, t, g, n * block)
    dist = q_pos[None, :, None, None] - k_pos
    mask = (dist >= 0) & jnp.broadcast_to(valid[..., None], (b, t, g, n, block)).reshape(b, t, g, n * block)
    s = jnp.einsum('btgrd,btgmd->btgrm', q, kg, preferred_element_type=jnp.float32) * SCALE
    s = s - slopes[None, None, :, :, None] * dist.astype(jnp.float32)[:, :, :, None, :]
    p, lse = masked_softmax(s, mask[:, :, :, None, :])
    o = jnp.einsum('btgrm,btgmd->btgrd', p, vg, preferred_element_type=jnp.float32)
    return o, lse


def own_block_attention(q_blk, k_blk, v_blk, slopes):
    b, nb, mb, h, dk = q_blk.shape
    s = jnp.einsum('bnqhd,bnkhd->bnhqk', q_blk, k_blk, preferred_element_type=jnp.float32) * SCALE
    dist = jnp.arange(mb)[:, None] - jnp.arange(mb)[None, :]
    s = s - slopes[:, None, None] * dist.astype(jnp.float32)
    p, lse = masked_softmax(s, dist >= 0)
    o = jnp.einsum('bnhqk,bnkhd->bnqhd', p, v_blk, preferred_element_type=jnp.float32)
    return o.reshape(b, nb * mb, h, dk), jnp.moveaxis(lse, 2, 3).reshape(b, nb * mb, h)


def window_attention_banded(q, k, v, slopes):
    b, s_len, g, r, dk = q.shape
    nb = s_len // QBLOCK
    w = WINDOW // QBLOCK
    kp = jnp.concatenate([jnp.zeros((b, WINDOW, g, dk), k.dtype), k], axis=1).reshape(b, nb + w, QBLOCK, g, dk)
    vp = jnp.concatenate([jnp.zeros((b, WINDOW, g, dk), v.dtype), v], axis=1).reshape(b, nb + w, QBLOCK, g, dk)
    kband = jnp.concatenate([kp[:, i:i + nb] for i in range(w + 1)], axis=2)
    vband = jnp.concatenate([vp[:, i:i + nb] for i in range(w + 1)], axis=2)
    qb = q.reshape(b, nb, QBLOCK, g, r, dk)
    s = jnp.einsum('bnqgrd,bnkgd->bngrqk', qb, kband, preferred_element_type=jnp.float32) * SCALE
    q_pos = jnp.arange(nb)[:, None] * QBLOCK + jnp.arange(QBLOCK)[None, :]
    k_pos = jnp.arange(nb)[:, None] * QBLOCK - WINDOW + jnp.arange((w + 1) * QBLOCK)[None, :]
    dist = q_pos[:, :, None] - k_pos[:, None, :]
    mask = (dist >= 0) & (dist <= WINDOW) & (k_pos >= 0)[:, None, :]
    s = s - slopes[None, None, :, :, None, None] * dist.astype(jnp.float32)[None, :, None, None]
    p, _ = masked_softmax(s, mask[None, :, None, None])
    o = jnp.einsum('bngrqk,bnkgd->bnqgrd', p, vband, preferred_element_type=jnp.float32)
    return o.reshape(b, s_len, g, r, dk)


def window_attention_dense(q, k, v, q_pos, k_pos, slopes):
    s = jnp.einsum('btgrd,bkgd->btgrk', q, k, preferred_element_type=jnp.float32) * SCALE
    dist = q_pos[:, None] - k_pos[None, :]
    mask = (dist >= 0) & (dist <= WINDOW)
    s = s - slopes[None, None, :, :, None] * dist.astype(jnp.float32)[None, :, None, None, :]
    p, _ = masked_softmax(s, mask[None, :, None, None, :])
    return jnp.einsum('btgrk,bkgd->btgrd', p, v, preferred_element_type=jnp.float32)


def compress_blocks(kv, pos_emb, w1, w2):
    b, length, g, dk = kv.shape
    n_chunks = length // CMP_STRIDE
    n_sub = CMP_LEN // CMP_STRIDE
    n_cmp = n_chunks - n_sub + 1
    chunks = kv[:, :n_chunks * CMP_STRIDE].reshape(b, n_chunks, CMP_STRIDE, g, dk)
    blocks = jnp.concatenate([chunks[:, i:i + n_cmp] for i in range(n_sub)], axis=2)
    blocks = blocks + pos_emb[None, None, :, None, :]
    flat = jnp.moveaxis(blocks, 3, 2).reshape(b, n_cmp, g, CMP_LEN * dk)
    return jax.nn.gelu(flat @ w1) @ w2


def nsa_mixer(q, gates, ctx, o_win, q_pos, slopes, lw):
    b, t = q.shape[:2]
    sl = slopes.reshape(NSA_KV_HEADS, NSA_GROUP)
    kc = rms_norm(compress_blocks(ctx[:, :, 0], lw['cmp_pe_k'], lw['cmp_w1_k'], lw['cmp_w2_k']), lw['g_k_cmp'])
    vc = compress_blocks(ctx[:, :, 1], lw['cmp_pe_v'], lw['cmp_w1_v'], lw['cmp_w2_v'])
    n_cmp = kc.shape[1]
    c_start = jnp.arange(n_cmp) * CMP_STRIDE
    dist = q_pos[:, None] - (c_start + CMP_LEN - 1)[None, :]
    s = jnp.einsum('btgrd,bcgd->btgrc', q, kc, preferred_element_type=jnp.float32) * SCALE
    s = s - sl[None, None, :, :, None] * dist.astype(jnp.float32)[None, :, None, None, :]
    p_cmp, _ = masked_softmax(s, (dist >= 0)[None, :, None, None, :])
    o_cmp = jnp.einsum('btgrc,bcgd->btgrd', p_cmp, vc, preferred_element_type=jnp.float32)
    length = ctx.shape[1]
    n_sel = -(-length // SEL_BLOCK)
    j_start = jnp.arange(n_sel) * SEL_BLOCK
    overlap = ((c_start[:, None] < j_start[None, :] + SEL_BLOCK)
               & (c_start[:, None] + CMP_LEN > j_start[None, :])).astype(jnp.float32)
    imp = jnp.einsum('btgrc,cj->btgj', p_cmp, overlap)
    jj = jnp.arange(n_sel)[None, :]
    own = (q_pos // SEL_BLOCK)[:, None]
    imp = jnp.where((jj == own)[None, :, None, :], OWN_SCORE,
                    jnp.where((jj < own)[None, :, None, :], imp, NEG))
    val, idx = lax.top_k(imp, min(SEL_TOPK, n_sel))
    valid = val > 0.5 * NEG
    k_blk = pad_rows(ctx[:, :, 2], n_sel * SEL_BLOCK).reshape(b, n_sel, SEL_BLOCK, NSA_KV_HEADS, HEAD_DIM)
    v_blk = pad_rows(ctx[:, :, 3], n_sel * SEL_BLOCK).reshape(b, n_sel, SEL_BLOCK, NSA_KV_HEADS, HEAD_DIM)

    def sel_fn(qp, qb, ib, vb):
        return gathered_block_attention(qb, k_blk, v_blk, ib, vb, qp, sl, SEL_BLOCK)[0]

    o_slc = sweep_queries(sel_fn, QBLOCK, q_pos, q, idx, valid)
    o = gates[..., 0:1] * o_cmp + gates[..., 1:2] * o_slc + gates[..., 2:3] * o_win
    return o.reshape(b, t, NSA_WIDTH)


def moba_mixer(q, k_ctx, v_ctx, q_pos, slopes, prompt):
    b, t, h, dk = q.shape
    length = k_ctx.shape[1]
    nb = -(-length // MOBA_BLOCK)
    lp = nb * MOBA_BLOCK
    k_blk = pad_rows(k_ctx, lp).reshape(b, nb, MOBA_BLOCK, h, dk)
    v_blk = pad_rows(v_ctx, lp).reshape(b, nb, MOBA_BLOCK, h, dk)
    means = jnp.mean(k_blk.astype(jnp.float32), axis=2)
    gate = jnp.einsum('bthd,bnhd->bthn', q, means, preferred_element_type=jnp.float32)
    past = jnp.arange(nb)[None, :] < (q_pos // MOBA_BLOCK)[:, None]
    gate = jnp.where(past[None, :, None, :], gate, NEG)
    val, idx = lax.top_k(gate, min(MOBA_TOPK, nb))
    valid = val > 0.5 * NEG
    q5 = q[:, :, :, None, :]
    sl = slopes[:, None]

    def sel_fn(qp, qb, ib, vb):
        return gathered_block_attention(qb, k_blk, v_blk, ib, vb, qp, sl, MOBA_BLOCK)

    o_sel, lse_sel = sweep_queries(sel_fn, MOBA_QBLOCK, q_pos, q5, idx, valid)
    if prompt:
        q_blk = pad_rows(q, lp).reshape(b, nb, MOBA_BLOCK, h, dk)
        o_own, lse_own = own_block_attention(q_blk, k_blk, v_blk, slopes)
        o_own = o_own[:, :t, :, None]
        lse_own = lse_own[:, :t, :, None]
    else:
        own = jnp.broadcast_to((q_pos // MOBA_BLOCK)[None, :, None, None], (b, t, h, 1))

        def own_fn(qp, qb, ib):
            return gathered_block_attention(qb, k_blk, v_blk, ib, jnp.ones(ib.shape, bool), qp, sl, MOBA_BLOCK)

        o_own, lse_own = sweep_queries(own_fn, MOBA_QBLOCK, q_pos, q5, own)
    o = merge_partials(o_sel, lse_sel, o_own, lse_own)
    return o.reshape(b, t, MOBA_WIDTH)


def peer_ffn(h, w_q, sub_keys, expert_u, expert_v):
    b, t, d = h.shape
    n = b * t
    n_blk = -(-n // PEER_TBLOCK)
    xt = h.reshape(n, d)
    if n_blk * PEER_TBLOCK != n:
        xt = jnp.pad(xt, ((0, n_blk * PEER_TBLOCK - n), (0, 0)))
    q = (xt @ w_q).reshape(-1, PEER_HEADS, 2, PEER_QDIM // 2)
    s = jnp.einsum('nphd,phkd->nphk', q, sub_keys, preferred_element_type=jnp.float32)
    top_s, top_i = lax.top_k(s, PEER_TOPK)
    cand_s = (top_s[:, :, 0, :, None] + top_s[:, :, 1, None, :]).reshape(-1, PEER_HEADS, PEER_TOPK * PEER_TOPK)
    cand_i = (top_i[:, :, 0, :, None] * PEER_KEYS + top_i[:, :, 1, None, :]).reshape(-1, PEER_HEADS, PEER_TOPK * PEER_TOPK)
    best_s, pos = lax.top_k(cand_s, PEER_TOPK)
    e_idx = jnp.take_along_axis(cand_i, pos, axis=-1)
    g = jax.nn.softmax(best_s, axis=-1)

    def block_fn(args):
        xb, eb, gb = args
        a = jax.nn.gelu(jnp.einsum('td,tpkd->tpk', xb, expert_u[eb], preferred_element_type=jnp.float32))
        return jnp.einsum('tpk,tpkd->td', (gb * a).astype(xb.dtype), expert_v[eb], preferred_element_type=jnp.float32)

    out = lax.map(block_fn, (xt.reshape(n_blk, PEER_TBLOCK, d),
                             e_idx.reshape(n_blk, PEER_TBLOCK, PEER_HEADS, PEER_TOPK),
                             g.reshape(n_blk, PEER_TBLOCK, PEER_HEADS, PEER_TOPK)))
    return out.reshape(-1, d)[:n].reshape(b, t, d).astype(h.dtype)


def layer_forward(x, ple, lw, slopes_nsa, slopes_moba, past_nsa=None, past_moba=None, win_buf=None):
    b, t, _ = x.shape
    h = rms_norm(x, lw['g_mix'])
    sizes = [NSA_WIDTH] + [NSA_KV_WIDTH] * 6 + [N_GATE] + [MOBA_WIDTH] * 3
    offsets = np.cumsum(sizes)[:-1].tolist()
    q_n, kc, vc, ks, vs, kw, vw, gt, q_m, k_m, v_m = jnp.split(h @ lw['w_in'], offsets, axis=-1)

    def heads(a, nh):
        return a.reshape(b, t, nh, HEAD_DIM)

    q_n = rms_norm(heads(q_n, NSA_HEADS), lw['g_q_nsa']).reshape(b, t, NSA_KV_HEADS, NSA_GROUP, HEAD_DIM)
    kc = heads(kc, NSA_KV_HEADS)
    vc = heads(vc, NSA_KV_HEADS)
    ks = rms_norm(heads(ks, NSA_KV_HEADS), lw['g_k_slc'])
    vs = heads(vs, NSA_KV_HEADS)
    kw = rms_norm(heads(kw, NSA_KV_HEADS), lw['g_k_win'])
    vw = heads(vw, NSA_KV_HEADS)
    gates = jax.nn.sigmoid(gt.astype(jnp.float32)).reshape(b, t, NSA_KV_HEADS, NSA_GROUP, 3)
    q_m = rms_norm(heads(q_m, MOBA_HEADS), lw['g_q_moba'])
    k_m = rms_norm(heads(k_m, MOBA_HEADS), lw['g_k_moba'])
    v_m = heads(v_m, MOBA_HEADS)
    nsa_rows = jnp.stack([kc, vc, ks, vs], axis=2)
    moba_rows = jnp.stack([k_m, v_m], axis=2)
    win_rows = jnp.stack([kw, vw], axis=2)
    if past_nsa is None:
        q_pos = jnp.arange(t)
        nsa_ctx, moba_ctx = nsa_rows, moba_rows
        o_win = window_attention_banded(q_n, kw, vw, slopes_nsa.reshape(NSA_KV_HEADS, NSA_GROUP))
        new_win = win_rows[:, t - min(WINDOW, t):]
    else:
        p_len = past_nsa.shape[1]
        q_pos = p_len + jnp.arange(t)
        nsa_ctx = jnp.concatenate([past_nsa, nsa_rows], axis=1)
        moba_ctx = jnp.concatenate([past_moba, moba_rows], axis=1)
        buf = jnp.concatenate([win_buf, win_rows], axis=1)
        k_pos = p_len - win_buf.shape[1] + jnp.arange(buf.shape[1])
        o_win = window_attention_dense(q_n, buf[:, :, 0], buf[:, :, 1], q_pos, k_pos,
                                       slopes_nsa.reshape(NSA_KV_HEADS, NSA_GROUP))
        new_win = buf[:, t:]
    o_nsa = nsa_mixer(q_n, gates, nsa_ctx, o_win, q_pos, slopes_nsa, lw)
    o_moba = moba_mixer(q_m, moba_ctx[:, :, 0], moba_ctx[:, :, 1], q_pos, slopes_moba, past_nsa is None)
    mix = jnp.concatenate([rms_norm(o_nsa, lw['g_out_nsa']), rms_norm(o_moba, lw['g_out_moba'])], axis=-1)
    x = x + mix.astype(x.dtype) @ lw['w_out']
    x = x + peer_ffn(rms_norm(x, lw['g_ffn']), lw['w_peer_q'], lw['peer_sub_keys'], lw['peer_u'], lw['peer_v'])
    gate = jax.nn.sigmoid(rms_norm(x, lw['g_ple']) @ lw['w_ple_gate'])
    x = x + gate * rms_norm(ple @ lw['w_ple_proj'], lw['g_ple_post'])
    return x, nsa_rows, moba_rows, new_win


def setup_inputs(seed: int = 0) -> dict:
    key = jax.random.key(seed)
    kit = iter(jax.random.split(key, 40))

    def nrm(shape, scale):
        return jax.random.normal(next(kit), shape, jnp.float32) * scale

    def gain(shape):
        return 1.0 + 0.05 * jax.random.normal(next(kit), shape, jnp.float32)

    n_pages = PAST_LEN // PAGE_SIZE
    n_used = DEC_BATCH * n_pages
    n_phys = n_used + max(1, n_used // 4)
    win_rows = min(WINDOW, PAST_LEN)
    page_table = jax.random.permutation(next(kit), n_phys)[:n_used].reshape(DEC_BATCH, n_pages).astype(jnp.int32)
    return {
        'x_prompt': nrm((BATCH, SEQ, D_MODEL), 1.0),
        'x_sample': nrm((DEC_BATCH, DEC_SEQ, D_MODEL), 1.0),
        'cache_nsa': nrm((DEPTH, n_phys, PAGE_SIZE, 4, NSA_KV_HEADS, HEAD_DIM), 1.0),
        'cache_moba': nrm((DEPTH, n_phys, PAGE_SIZE, 2, MOBA_HEADS, HEAD_DIM), 1.0),
        'state_win': nrm((DEPTH, DEC_BATCH, win_rows, 2, NSA_KV_HEADS, HEAD_DIM), 1.0),
        'page_table': page_table,
        'p_prompt': nrm((DEPTH, BATCH, SEQ, PLE_DIM), 1.0),
        'p_sample': nrm((DEPTH, DEC_BATCH, DEC_SEQ, PLE_DIM), 1.0),
        'g_mix': gain((DEPTH, D_MODEL)),
        'w_in': nrm((DEPTH, D_MODEL, N_IN_COLS), D_MODEL ** -0.5),
        'g_q_nsa': gain((DEPTH, HEAD_DIM)),
        'g_k_cmp': gain((DEPTH, HEAD_DIM)),
        'g_k_slc': gain((DEPTH, HEAD_DIM)),
        'g_k_win': gain((DEPTH, HEAD_DIM)),
        'g_q_moba': gain((DEPTH, HEAD_DIM)),
        'g_k_moba': gain((DEPTH, HEAD_DIM)),
        'cmp_pe_k': nrm((DEPTH, CMP_LEN, HEAD_DIM), 0.1),
        'cmp_w1_k': nrm((DEPTH, CMP_LEN * HEAD_DIM, CMP_HIDDEN), (CMP_LEN * HEAD_DIM) ** -0.5),
        'cmp_w2_k': nrm((DEPTH, CMP_HIDDEN, HEAD_DIM), CMP_HIDDEN ** -0.5),
        'cmp_pe_v': nrm((DEPTH, CMP_LEN, HEAD_DIM), 0.1),
        'cmp_w1_v': nrm((DEPTH, CMP_LEN * HEAD_DIM, CMP_HIDDEN), (CMP_LEN * HEAD_DIM) ** -0.5),
        'cmp_w2_v': nrm((DEPTH, CMP_HIDDEN, HEAD_DIM), CMP_HIDDEN ** -0.5),
        'g_out_nsa': gain((DEPTH, NSA_WIDTH)),
        'g_out_moba': gain((DEPTH, MOBA_WIDTH)),
        'w_out': nrm((DEPTH, NSA_WIDTH + MOBA_WIDTH, D_MODEL), (NSA_WIDTH + MOBA_WIDTH) ** -0.5),
        'g_ffn': gain((DEPTH, D_MODEL)),
        'w_peer_q': nrm((DEPTH, D_MODEL, PEER_HEADS * PEER_QDIM), D_MODEL ** -0.5),
        'peer_sub_keys': nrm((DEPTH, PEER_HEADS, 2, PEER_KEYS, PEER_QDIM // 2), (PEER_QDIM // 2) ** -0.5),
        'peer_u': nrm((DEPTH, PEER_EXPERTS, D_MODEL), D_MODEL ** -0.5),
        'peer_v': nrm((DEPTH, PEER_EXPERTS, D_MODEL), PEER_HEADS ** -0.5),
        'g_ple': gain((DEPTH, D_MODEL)),
        'w_ple_gate': nrm((DEPTH, D_MODEL, D_MODEL), D_MODEL ** -0.5),
        'w_ple_proj': nrm((DEPTH, PLE_DIM, D_MODEL), PLE_DIM ** -0.5),
        'g_ple_post': gain((DEPTH, D_MODEL)),
    }


def reference(x_prompt, x_sample, cache_nsa, cache_moba, state_win, page_table, p_prompt, p_sample,
              g_mix, w_in, g_q_nsa, g_k_cmp, g_k_slc, g_k_win, g_q_moba, g_k_moba,
              cmp_pe_k, cmp_w1_k, cmp_w2_k, cmp_pe_v, cmp_w1_v, cmp_w2_v,
              g_out_nsa, g_out_moba, w_out, g_ffn, w_peer_q, peer_sub_keys, peer_u, peer_v,
              g_ple, w_ple_gate, w_ple_proj, g_ple_post):
    slopes_nsa, slopes_moba = alibi_slopes()
    n_seq, n_pages = page_table.shape
    y_prompt, y_sample = x_prompt, x_sample
    nsa_p, moba_p, win_p, nsa_s, moba_s, win_s = [], [], [], [], [], []
    for i in range(DEPTH):
        lw = {
            'g_mix': g_mix[i], 'w_in': w_in[i], 'g_q_nsa': g_q_nsa[i], 'g_k_cmp': g_k_cmp[i],
            'g_k_slc': g_k_slc[i], 'g_k_win': g_k_win[i], 'g_q_moba': g_q_moba[i], 'g_k_moba': g_k_moba[i],
            'cmp_pe_k': cmp_pe_k[i], 'cmp_w1_k': cmp_w1_k[i], 'cmp_w2_k': cmp_w2_k[i],
            'cmp_pe_v': cmp_pe_v[i], 'cmp_w1_v': cmp_w1_v[i], 'cmp_w2_v': cmp_w2_v[i],
            'g_out_nsa': g_out_nsa[i], 'g_out_moba': g_out_moba[i], 'w_out': w_out[i], 'g_ffn': g_ffn[i],
            'w_peer_q': w_peer_q[i], 'peer_sub_keys': peer_sub_keys[i], 'peer_u': peer_u[i], 'peer_v': peer_v[i],
            'g_ple': g_ple[i], 'w_ple_gate': w_ple_gate[i], 'w_ple_proj': w_ple_proj[i], 'g_ple_post': g_ple_post[i],
        }
        y_prompt, r_nsa, r_moba, r_win = layer_forward(y_prompt, p_prompt[i], lw, slopes_nsa, slopes_moba)
        nsa_p.append(r_nsa)
        moba_p.append(r_moba)
        win_p.append(r_win)
        past_nsa = cache_nsa[i][page_table].reshape((n_seq, n_pages * PAGE_SIZE) + cache_nsa.shape[3:])
        past_moba = cache_moba[i][page_table].reshape((n_seq, n_pages * PAGE_SIZE) + cache_moba.shape[3:])
        y_sample, r_nsa, r_moba, r_win = layer_forward(y_sample, p_sample[i], lw, slopes_nsa, slopes_moba,
                                                       past_nsa, past_moba, state_win[i])
        nsa_s.append(r_nsa)
        moba_s.append(r_moba)
        win_s.append(r_win)
    nsa_rows_prompt = jnp.stack(nsa_p)
    moba_rows_prompt = jnp.stack(moba_p)
    win_prompt = jnp.stack(win_p)
    nsa_rows_sample = jnp.stack(nsa_s)
    moba_rows_sample = jnp.stack(moba_s)
    win_sample = jnp.stack(win_s)
    return (y_prompt, y_sample, nsa_rows_prompt, moba_rows_prompt, win_prompt, nsa_rows_sample, moba_rows_sample, win_sample)
```

```python
import functools

import jax
import jax.numpy as jnp
from jax import lax
import numpy as np
from jax.experimental import pallas as pl
from jax.experimental.pallas import tpu as pltpu

D_MODEL = 2048
BATCH = 4
SEQ = 2048
DEPTH = 1
DEC_BATCH = 128
DEC_SEQ = 1
PAST_LEN = 2048
PAGE_SIZE = 128

HEAD_DIM = 64
NSA_HEADS = D_MODEL // (2 * HEAD_DIM)
NSA_KV_HEADS = max(1, NSA_HEADS // 4)
NSA_GROUP = NSA_HEADS // NSA_KV_HEADS
MOBA_HEADS = D_MODEL // (2 * HEAD_DIM)
NSA_WIDTH = NSA_HEADS * HEAD_DIM
NSA_KV_WIDTH = NSA_KV_HEADS * HEAD_DIM
MOBA_WIDTH = MOBA_HEADS * HEAD_DIM
N_GATE = 3 * NSA_HEADS
N_IN_COLS = NSA_WIDTH + 6 * NSA_KV_WIDTH + N_GATE + 3 * MOBA_WIDTH
CMP_LEN = 32
CMP_STRIDE = 16
CMP_HIDDEN = 64
SEL_BLOCK = 64
SEL_TOPK = 8
WINDOW = 512
MOBA_BLOCK = 256
MOBA_TOPK = 3
PEER_KEYS = 128
PEER_EXPERTS = PEER_KEYS * PEER_KEYS
PEER_HEADS = 8
PEER_TOPK = 16
PEER_QDIM = 256
PLE_DIM = 256
QBLOCK = 128
MOBA_QBLOCK = 64
PEER_TBLOCK = 128
RMS_EPS = 1e-6
NEG = -1e30
OWN_SCORE = 1e9
SCALE = HEAD_DIM ** -0.5


def _mm_kernel(x_ref, w_ref, o_ref):
    o_ref[...] = jnp.dot(x_ref[...].astype(jnp.bfloat16), w_ref[...].astype(jnp.bfloat16),
                         preferred_element_type=jnp.float32)


def _mm(x, w):
    lead = x.shape[:-1]
    k = x.shape[-1]
    n = w.shape[-1]
    x2 = x.reshape(-1, k)
    m = x2.shape[0]
    tm = 512 if m % 512 == 0 else m
    n_pad = -(-n // 128) * 128
    if n_pad != n:
        w = jnp.pad(w, ((0, 0), (0, n_pad - n)))
    tn = 512 if n_pad % 512 == 0 else (256 if n_pad % 256 == 0 else 128)
    out = pl.pallas_call(
        _mm_kernel,
        grid=(m // tm, n_pad // tn),
        in_specs=[pl.BlockSpec((tm, k), lambda i, j: (i, 0)),
                  pl.BlockSpec((k, tn), lambda i, j: (0, j))],
        out_specs=pl.BlockSpec((tm, tn), lambda i, j: (i, j)),
        out_shape=jax.ShapeDtypeStruct((m, n_pad), jnp.float32),
        compiler_params=pltpu.CompilerParams(dimension_semantics=("arbitrary", "arbitrary")),
        name="mm",
    )(x2, w)
    if n_pad != n:
        out = out[:, :n]
    return out.reshape(lead + (n,))


def rms_norm(x, g):
    xf = x.astype(jnp.float32)
    y = xf * lax.rsqrt(jnp.mean(xf * xf, axis=-1, keepdims=True) + RMS_EPS)
    return (y * g.astype(jnp.float32)).astype(x.dtype)


def alibi_slopes():
    n = NSA_HEADS + MOBA_HEADS
    s = jnp.exp2(-8.0 * jnp.arange(1, n + 1, dtype=jnp.float32) / n)
    return s[0::2], s[1::2]


def pad_rows(a, length):
    extra = length - a.shape[1]
    if extra == 0:
        return a
    return jnp.pad(a, [(0, 0), (0, extra)] + [(0, 0)] * (a.ndim - 2))


def masked_softmax(s, mask):
    s = jnp.where(mask, s, NEG)
    m = jnp.max(s, axis=-1, keepdims=True)
    e = jnp.where(mask, jnp.exp(s - m), 0.0)
    l = jnp.sum(e, axis=-1, keepdims=True)
    safe = jnp.where(l > 0, l, 1.0)
    lse = jnp.where(l > 0, m + jnp.log(safe), NEG)
    return e / safe, lse[..., 0]


def merge_partials(o1, lse1, o2, lse2):
    lse = jnp.logaddexp(lse1, lse2)
    return jnp.exp(lse1 - lse)[..., None] * o1 + jnp.exp(lse2 - lse)[..., None] * o2


def sweep_queries(fn, block, q_pos, *xs):
    t = q_pos.shape[0]
    if t <= block or t % block:
        return fn(q_pos, *xs)
    nb = t // block

    def split(a):
        return jnp.moveaxis(a.reshape((a.shape[0], nb, block) + a.shape[2:]), 1, 0)

    def join(a):
        a = jnp.moveaxis(a, 0, 1)
        return a.reshape((a.shape[0], t) + a.shape[3:])

    outs = lax.map(lambda args: fn(*args), (q_pos.reshape(nb, block),) + tuple(split(a) for a in xs))
    return jax.tree_util.tree_map(join, outs)


def gathered_block_attention(q, k_blocks, v_blocks, idx, valid, q_pos, slopes, block):
    b, t, g, r, dk = q.shape
    n = idx.shape[-1]
    bi = jnp.arange(b)[:, None, None, None]
    gi = jnp.arange(g)[None, None, :, None]
    kg = k_blocks[bi, idx, :, gi].reshape(b, t, g, n * block, dk)
    vg = v_blocks[bi, idx, :, gi].reshape(b, t, g, n * block, dk)
    k_pos = (idx[..., None] * block + jnp.arange(block)).reshape(b, t, g, n * block)
    dist = q_pos[None, :, None, None] - k_pos
    mask = (dist >= 0) & jnp.broadcast_to(valid[..., None], (b, t, g, n, block)).reshape(b, t, g, n * block)
    s = jnp.einsum('btgrd,btgmd->btgrm', q, kg, preferred_element_type=jnp.float32) * SCALE
    s = s - slopes[None, None, :, :, None] * dist.astype(jnp.float32)[:, :, :, None, :]
    p, lse = masked_softmax(s, mask[:, :, :, None, :])
    o = jnp.einsum('btgrm,btgmd->btgrd', p, vg, preferred_element_type=jnp.float32)
    return o, lse


def own_block_attention(q_blk, k_blk, v_blk, slopes):
    b, nb, mb, h, dk = q_blk.shape
    s = jnp.einsum('bnqhd,bnkhd->bnhqk', q_blk, k_blk, preferred_element_type=jnp.float32) * SCALE
    dist = jnp.arange(mb)[:, None] - jnp.arange(mb)[None, :]
    s = s - slopes[:, None, None] * dist.astype(jnp.float32)
    p, lse = masked_softmax(s, dist >= 0)
    o = jnp.einsum('bnhqk,bnkhd->bnqhd', p, v_blk, preferred_element_type=jnp.float32)
    return o.reshape(b, nb * mb, h, dk), jnp.moveaxis(lse, 2, 3).reshape(b, nb * mb, h)


def window_attention_banded(q, k, v, slopes):
    b, s_len, g, r, dk = q.shape
    nb = s_len // QBLOCK
    w = WINDOW // QBLOCK
    kp = jnp.concatenate([jnp.zeros((b, WINDOW, g, dk), k.dtype), k], axis=1).reshape(b, nb + w, QBLOCK, g, dk)
    vp = jnp.concatenate([jnp.zeros((b, WINDOW, g, dk), v.dtype), v], axis=1).reshape(b, nb + w, QBLOCK, g, dk)
    kband = jnp.concatenate([kp[:, i:i + nb] for i in range(w + 1)], axis=2)
    vband = jnp.concatenate([vp[:, i:i + nb] for i in range(w + 1)], axis=2)
    qb = q.reshape(b, nb, QBLOCK, g, r, dk)
    s = jnp.einsum('bnqgrd,bnkgd->bngrqk', qb, kband, preferred_element_type=jnp.float32) * SCALE
    q_pos = jnp.arange(nb)[:, None] * QBLOCK + jnp.arange(QBLOCK)[None, :]
    k_pos = jnp.arange(nb)[:, None] * QBLOCK - WINDOW + jnp.arange((w + 1) * QBLOCK)[None, :]
    dist = q_pos[:, :, None] - k_pos[:, None, :]
    mask = (dist >= 0) & (dist <= WINDOW) & (k_pos >= 0)[:, None, :]
    s = s - slopes[None, None, :, :, None, None] * dist.astype(jnp.float32)[None, :, None, None]
    p, _ = masked_softmax(s, mask[None, :, None, None])
    o = jnp.einsum('bngrqk,bnkgd->bnqgrd', p, vband, preferred_element_type=jnp.float32)
    return o.reshape(b, s_len, g, r, dk)


def window_attention_dense(q, k, v, q_pos, k_pos, slopes):
    s = jnp.einsum('btgrd,bkgd->btgrk', q, k, preferred_element_type=jnp.float32) * SCALE
    dist = q_pos[:, None] - k_pos[None, :]
    mask = (dist >= 0) & (dist <= WINDOW)
    s = s - slopes[None, None, :, :, None] * dist.astype(jnp.float32)[None, :, None, None, :]
    p, _ = masked_softmax(s, mask[None, :, None, None, :])
    return jnp.einsum('btgrk,bkgd->btgrd', p, v, preferred_element_type=jnp.float32)


def compress_blocks(kv, pos_emb, w1, w2):
    b, length, g, dk = kv.shape
    n_chunks = length // CMP_STRIDE
    n_sub = CMP_LEN // CMP_STRIDE
    n_cmp = n_chunks - n_sub + 1
    chunks = kv[:, :n_chunks * CMP_STRIDE].reshape(b, n_chunks, CMP_STRIDE, g, dk)
    blocks = jnp.concatenate([chunks[:, i:i + n_cmp] for i in range(n_sub)], axis=2)
    blocks = blocks + pos_emb[None, None, :, None, :]
    flat = jnp.moveaxis(blocks, 3, 2).reshape(b, n_cmp, g, CMP_LEN * dk)
    return jax.nn.gelu(flat @ w1) @ w2


def nsa_mixer(q, gates, ctx, o_win, q_pos, slopes, lw):
    b, t = q.shape[:2]
    sl = slopes.reshape(NSA_KV_HEADS, NSA_GROUP)
    kc = rms_norm(compress_blocks(ctx[:, :, 0], lw['cmp_pe_k'], lw['cmp_w1_k'], lw['cmp_w2_k']), lw['g_k_cmp'])
    vc = compress_blocks(ctx[:, :, 1], lw['cmp_pe_v'], lw['cmp_w1_v'], lw['cmp_w2_v'])
    n_cmp = kc.shape[1]
    c_start = jnp.arange(n_cmp) * CMP_STRIDE
    dist = q_pos[:, None] - (c_start + CMP_LEN - 1)[None, :]
    s = jnp.einsum('btgrd,bcgd->btgrc', q, kc, preferred_element_type=jnp.float32) * SCALE
    s = s - sl[None, None, :, :, None] * dist.astype(jnp.float32)[None, :, None, None, :]
    p_cmp, _ = masked_softmax(s, (dist >= 0)[None, :, None, None, :])
    o_cmp = jnp.einsum('btgrc,bcgd->btgrd', p_cmp, vc, preferred_element_type=jnp.float32)
    length = ctx.shape[1]
    n_sel = -(-length // SEL_BLOCK)
    j_start = jnp.arange(n_sel) * SEL_BLOCK
    overlap = ((c_start[:, None] < j_start[None, :] + SEL_BLOCK)
               & (c_start[:, None] + CMP_LEN > j_start[None, :])).astype(jnp.float32)
    imp = jnp.einsum('btgrc,cj->btgj', p_cmp, overlap)
    jj = jnp.arange(n_sel)[None, :]
    own = (q_pos // SEL_BLOCK)[:, None]
    imp = jnp.where((jj == own)[None, :, None, :], OWN_SCORE,
                    jnp.where((jj < own)[None, :, None, :], imp, NEG))
    val, idx = lax.top_k(imp, min(SEL_TOPK, n_sel))
    valid = val > 0.5 * NEG
    k_blk = pad_rows(ctx[:, :, 2], n_sel * SEL_BLOCK).reshape(b, n_sel, SEL_BLOCK, NSA_KV_HEADS, HEAD_DIM)
    v_blk = pad_rows(ctx[:, :, 3], n_sel * SEL_BLOCK).reshape(b, n_sel, SEL_BLOCK, NSA_KV_HEADS, HEAD_DIM)

    def sel_fn(qp, qb, ib, vb):
        return gathered_block_attention(qb, k_blk, v_blk, ib, vb, qp, sl, SEL_BLOCK)[0]

    o_slc = sweep_queries(sel_fn, QBLOCK, q_pos, q, idx, valid)
    o = gates[..., 0:1] * o_cmp + gates[..., 1:2] * o_slc + gates[..., 2:3] * o_win
    return o.reshape(b, t, NSA_WIDTH)


def moba_mixer(q, k_ctx, v_ctx, q_pos, slopes, prompt):
    b, t, h, dk = q.shape
    length = k_ctx.shape[1]
    nb = -(-length // MOBA_BLOCK)
    lp = nb * MOBA_BLOCK
    k_blk = pad_rows(k_ctx, lp).reshape(b, nb, MOBA_BLOCK, h, dk)
    v_blk = pad_rows(v_ctx, lp).reshape(b, nb, MOBA_BLOCK, h, dk)
    means = jnp.mean(k_blk.astype(jnp.float32), axis=2)
    gate = jnp.einsum('bthd,bnhd->bthn', q, means, preferred_element_type=jnp.float32)
    past = jnp.arange(nb)[None, :] < (q_pos // MOBA_BLOCK)[:, None]
    gate = jnp.where(past[None, :, None, :], gate, NEG)
    val, idx = lax.top_k(gate, min(MOBA_TOPK, nb))
    valid = val > 0.5 * NEG
    q5 = q[:, :, :, None, :]
    sl = slopes[:, None]

    def sel_fn(qp, qb, ib, vb):
        return gathered_block_attention(qb, k_blk, v_blk, ib, vb, qp, sl, MOBA_BLOCK)

    o_sel, lse_sel = sweep_queries(sel_fn, MOBA_QBLOCK, q_pos, q5, idx, valid)
    if prompt:
        q_blk = pad_rows(q, lp).reshape(b, nb, MOBA_BLOCK, h, dk)
        o_own, lse_own = own_block_attention(q_blk, k_blk, v_blk, slopes)
        o_own = o_own[:, :t, :, None]
        lse_own = lse_own[:, :t, :, None]
    else:
        own = jnp.broadcast_to((q_pos // MOBA_BLOCK)[None, :, None, None], (b, t, h, 1))

        def own_fn(qp, qb, ib):
            return gathered_block_attention(qb, k_blk, v_blk, ib, jnp.ones(ib.shape, bool), qp, sl, MOBA_BLOCK)

        o_own, lse_own = sweep_queries(own_fn, MOBA_QBLOCK, q_pos, q5, own)
    o = merge_partials(o_sel, lse_sel, o_own, lse_own)
    return o.reshape(b, t, MOBA_WIDTH)


def peer_ffn(h, w_q, sub_keys, expert_u, expert_v):
    b, t, d = h.shape
    n = b * t
    n_blk = -(-n // PEER_TBLOCK)
    xt = h.reshape(n, d)
    if n_blk * PEER_TBLOCK != n:
        xt = jnp.pad(xt, ((0, n_blk * PEER_TBLOCK - n), (0, 0)))
    q = _mm(xt, w_q).reshape(-1, PEER_HEADS, 2, PEER_QDIM // 2)
    s = jnp.einsum('nphd,phkd->nphk', q, sub_keys, preferred_element_type=jnp.float32)
    top_s, top_i = lax.top_k(s, PEER_TOPK)
    cand_s = (top_s[:, :, 0, :, None] + top_s[:, :, 1, None, :]).reshape(-1, PEER_HEADS, PEER_TOPK * PEER_TOPK)
    cand_i = (top_i[:, :, 0, :, None] * PEER_KEYS + top_i[:, :, 1, None, :]).reshape(-1, PEER_HEADS, PEER_TOPK * PEER_TOPK)
    best_s, pos = lax.top_k(cand_s, PEER_TOPK)
    e_idx = jnp.take_along_axis(cand_i, pos, axis=-1)
    g = jax.nn.softmax(best_s, axis=-1)

    def block_fn(args):
        xb, eb, gb = args
        a = jax.nn.gelu(jnp.einsum('td,tpkd->tpk', xb, expert_u[eb], preferred_element_type=jnp.float32))
        return jnp.einsum('tpk,tpkd->td', (gb * a).astype(xb.dtype), expert_v[eb], preferred_element_type=jnp.float32)

    out = lax.map(block_fn, (xt.reshape(n_blk, PEER_TBLOCK, d),
                             e_idx.reshape(n_blk, PEER_TBLOCK, PEER_HEADS, PEER_TOPK),
                             g.reshape(n_blk, PEER_TBLOCK, PEER_HEADS, PEER_TOPK)))
    return out.reshape(-1, d)[:n].reshape(b, t, d).astype(h.dtype)


def layer_forward(x, ple, lw, slopes_nsa, slopes_moba, past_nsa=None, past_moba=None, win_buf=None):
    b, t, _ = x.shape
    h = rms_norm(x, lw['g_mix'])
    sizes = [NSA_WIDTH] + [NSA_KV_WIDTH] * 6 + [N_GATE] + [MOBA_WIDTH] * 3
    offsets = np.cumsum(sizes)[:-1].tolist()
    q_n, kc, vc, ks, vs, kw, vw, gt, q_m, k_m, v_m = jnp.split(_mm(h, lw['w_in']), offsets, axis=-1)

    def heads(a, nh):
        return a.reshape(b, t, nh, HEAD_DIM)

    q_n = rms_norm(heads(q_n, NSA_HEADS), lw['g_q_nsa']).reshape(b, t, NSA_KV_HEADS, NSA_GROUP, HEAD_DIM)
    kc = heads(kc, NSA_KV_HEADS)
    vc = heads(vc, NSA_KV_HEADS)
    ks = rms_norm(heads(ks, NSA_KV_HEADS), lw['g_k_slc'])
    vs = heads(vs, NSA_KV_HEADS)
    kw = rms_norm(heads(kw, NSA_KV_HEADS), lw['g_k_win'])
    vw = heads(vw, NSA_KV_HEADS)
    gates = jax.nn.sigmoid(gt.astype(jnp.float32)).reshape(b, t, NSA_KV_HEADS, NSA_GROUP, 3)
    q_m = rms_norm(heads(q_m, MOBA_HEADS), lw['g_q_moba'])
    k_m = rms_norm(heads(k_m, MOBA_HEADS), lw['g_k_moba'])
    v_m = heads(v_m, MOBA_HEADS)
    nsa_rows = jnp.stack([kc, vc, ks, vs], axis=2)
    moba_rows = jnp.stack([k_m, v_m], axis=2)
    win_rows = jnp.stack([kw, vw], axis=2)
    if past_nsa is None:
        q_pos = jnp.arange(t)
        nsa_ctx, moba_ctx = nsa_rows, moba_rows
        o_win = window_attention_banded(q_n, kw, vw, slopes_nsa.reshape(NSA_KV_HEADS, NSA_GROUP))
        new_win = win_rows[:, t - min(WINDOW, t):]
    else:
        p_len = past_nsa.shape[1]
        q_pos = p_len + jnp.arange(t)
        nsa_ctx = jnp.concatenate([past_nsa, nsa_rows], axis=1)
        moba_ctx = jnp.concatenate([past_moba, moba_rows], axis=1)
        buf = jnp.concatenate([win_buf, win_rows], axis=1)
        k_pos = p_len - win_buf.shape[1] + jnp.arange(buf.shape[1])
        o_win = window_attention_dense(q_n, buf[:, :, 0], buf[:, :, 1], q_pos, k_pos,
                                       slopes_nsa.reshape(NSA_KV_HEADS, NSA_GROUP))
        new_win = buf[:, t:]
    o_nsa = nsa_mixer(q_n, gates, nsa_ctx, o_win, q_pos, slopes_nsa, lw)
    o_moba = moba_mixer(q_m, moba_ctx[:, :, 0], moba_ctx[:, :, 1], q_pos, slopes_moba, past_nsa is None)
    mix = jnp.concatenate([rms_norm(o_nsa, lw['g_out_nsa']), rms_norm(o_moba, lw['g_out_moba'])], axis=-1)
    x = x + _mm(mix.astype(x.dtype), lw['w_out'])
    x = x + peer_ffn(rms_norm(x, lw['g_ffn']), lw['w_peer_q'], lw['peer_sub_keys'], lw['peer_u'], lw['peer_v'])
    gate = jax.nn.sigmoid(_mm(rms_norm(x, lw['g_ple']), lw['w_ple_gate']))
    x = x + gate * rms_norm(_mm(ple, lw['w_ple_proj']), lw['g_ple_post'])
    return x, nsa_rows, moba_rows, new_win


def kernel(x_prompt, x_sample, cache_nsa, cache_moba, state_win, page_table, p_prompt, p_sample,
           g_mix, w_in, g_q_nsa, g_k_cmp, g_k_slc, g_k_win, g_q_moba, g_k_moba,
           cmp_pe_k, cmp_w1_k, cmp_w2_k, cmp_pe_v, cmp_w1_v, cmp_w2_v,
           g_out_nsa, g_out_moba, w_out, g_ffn, w_peer_q, peer_sub_keys, peer_u, peer_v,
           g_ple, w_ple_gate, w_ple_proj, g_ple_post):
    slopes_nsa, slopes_moba = alibi_slopes()
    n_seq, n_pages = page_table.shape
    i = 0
    lw = {
        'g_mix': g_mix[i], 'w_in': w_in[i], 'g_q_nsa': g_q_nsa[i], 'g_k_cmp': g_k_cmp[i],
        'g_k_slc': g_k_slc[i], 'g_k_win': g_k_win[i], 'g_q_moba': g_q_moba[i], 'g_k_moba': g_k_moba[i],
        'cmp_pe_k': cmp_pe_k[i], 'cmp_w1_k': cmp_w1_k[i], 'cmp_w2_k': cmp_w2_k[i],
        'cmp_pe_v': cmp_pe_v[i], 'cmp_w1_v': cmp_w1_v[i], 'cmp_w2_v': cmp_w2_v[i],
        'g_out_nsa': g_out_nsa[i], 'g_out_moba': g_out_moba[i], 'w_out': w_out[i], 'g_ffn': g_ffn[i],
        'w_peer_q': w_peer_q[i], 'peer_sub_keys': peer_sub_keys[i], 'peer_u': peer_u[i], 'peer_v': peer_v[i],
        'g_ple': g_ple[i], 'w_ple_gate': w_ple_gate[i], 'w_ple_proj': w_ple_proj[i], 'g_ple_post': g_ple_post[i],
    }
    y_prompt, nsa_p, moba_p, win_p = layer_forward(x_prompt, p_prompt[i], lw, slopes_nsa, slopes_moba)
    past_nsa = cache_nsa[i][page_table].reshape((n_seq, n_pages * PAGE_SIZE) + cache_nsa.shape[3:])
    past_moba = cache_moba[i][page_table].reshape((n_seq, n_pages * PAGE_SIZE) + cache_moba.shape[3:])
    y_sample, nsa_s, moba_s, win_s = layer_forward(x_sample, p_sample[i], lw, slopes_nsa, slopes_moba,
                                                   past_nsa, past_moba, state_win[i])
    return (y_prompt, y_sample, nsa_p[None], moba_p[None], win_p[None], nsa_s[None], moba_s[None], win_s[None])
```

```python
import functools

import jax
import jax.numpy as jnp
from jax import lax
import numpy as np
from jax.experimental import pallas as pl
from jax.experimental.pallas import tpu as pltpu

D_MODEL = 2048
BATCH = 4
SEQ = 2048
DEPTH = 1
DEC_BATCH = 128
DEC_SEQ = 1
PAST_LEN = 2048
PAGE_SIZE = 128

HEAD_DIM = 64
NSA_HEADS = D_MODEL // (2 * HEAD_DIM)
NSA_KV_HEADS = max(1, NSA_HEADS // 4)
NSA_GROUP = NSA_HEADS // NSA_KV_HEADS
MOBA_HEADS = D_MODEL // (2 * HEAD_DIM)
NSA_WIDTH = NSA_HEADS * HEAD_DIM
NSA_KV_WIDTH = NSA_KV_HEADS * HEAD_DIM
MOBA_WIDTH = MOBA_HEADS * HEAD_DIM
N_GATE = 3 * NSA_HEADS
N_IN_COLS = NSA_WIDTH + 6 * NSA_KV_WIDTH + N_GATE + 3 * MOBA_WIDTH
CMP_LEN = 32
CMP_STRIDE = 16
CMP_HIDDEN = 64
SEL_BLOCK = 64
SEL_TOPK = 8
WINDOW = 512
MOBA_BLOCK = 256
MOBA_TOPK = 3
PEER_KEYS = 128
PEER_EXPERTS = PEER_KEYS * PEER_KEYS
PEER_HEADS = 8
PEER_TOPK = 16
PEER_QDIM = 256
PLE_DIM = 256
QBLOCK = 128
MOBA_QBLOCK = 64
PEER_TBLOCK = 128
RMS_EPS = 1e-6
NEG = -1e30
OWN_SCORE = 1e9
SCALE = HEAD_DIM ** -0.5


def _mm_kernel(x_ref, w_ref, o_ref):
    o_ref[...] = jnp.dot(x_ref[...].astype(jnp.bfloat16), w_ref[...].astype(jnp.bfloat16),
                         preferred_element_type=jnp.float32)


def _mm(x, w):
    lead = x.shape[:-1]
    k = x.shape[-1]
    n = w.shape[-1]
    x2 = x.reshape(-1, k)
    m = x2.shape[0]
    tm = 512 if m % 512 == 0 else m
    n_pad = -(-n // 128) * 128
    if n_pad != n:
        w = jnp.pad(w, ((0, 0), (0, n_pad - n)))
    tn = 512 if n_pad % 512 == 0 else (256 if n_pad % 256 == 0 else 128)
    out = pl.pallas_call(
        _mm_kernel,
        grid=(m // tm, n_pad // tn),
        in_specs=[pl.BlockSpec((tm, k), lambda i, j: (i, 0)),
                  pl.BlockSpec((k, tn), lambda i, j: (0, j))],
        out_specs=pl.BlockSpec((tm, tn), lambda i, j: (i, j)),
        out_shape=jax.ShapeDtypeStruct((m, n_pad), jnp.float32),
        compiler_params=pltpu.CompilerParams(dimension_semantics=("arbitrary", "arbitrary")),
        name="mm",
    )(x2, w)
    if n_pad != n:
        out = out[:, :n]
    return out.reshape(lead + (n,))


LANES = 128
ATTN_TILE = 256


def _attn_kernel(heads, n_masks, slopes_ref, q_ref, k_ref, v_ref, mask_ref, expand_ref, o_ref,
                 kb_ref, vb_ref, maskx_ref):
    s_idx = pl.program_id(1)
    i = pl.program_id(2)
    tq = ATTN_TILE
    n_heads = len(heads)

    @pl.when(i == 0)
    def _():
        kb_ref[...] = k_ref[0].astype(jnp.bfloat16)
        vb_ref[...] = v_ref[0].astype(jnp.bfloat16)

    for m in range(n_masks):
        mb = mask_ref[0, :, m * LANES:(m + 1) * LANES].astype(jnp.bfloat16)
        maskx_ref[m] = jnp.dot(mb, expand_ref[...], preferred_element_type=jnp.float32)

    lane = lax.broadcasted_iota(jnp.int32, (tq, LANES), 1)
    row = lax.broadcasted_iota(jnp.int32, (tq, tq), 0)
    col = lax.broadcasted_iota(jnp.int32, (tq, tq), 1)
    rc = (row - col).astype(jnp.float32)
    diag0 = pl.multiple_of(i * tq, tq)

    outs = []
    for h, (q_off, kv_half, m_idx) in enumerate(heads):
        slope = slopes_ref[s_idx * n_heads + h]
        chunk = q_off // LANES
        qc = q_ref[0, :, chunk * LANES:(chunk + 1) * LANES]
        if (q_off % LANES) // HEAD_DIM != kv_half:
            qc = pltpu.roll(qc, HEAD_DIM, 1)
        in_half = (lane >= HEAD_DIM) if kv_half else (lane < HEAD_DIM)
        qb = (jnp.where(in_half, qc, 0.0) * SCALE).astype(jnp.bfloat16)
        bias = slope * rc

        def scores(start):
            kblk = kb_ref[pl.ds(start, tq), :]
            return lax.dot_general(qb, kblk, (((1,), (1,)), ((), ())), preferred_element_type=jnp.float32)

        s = scores(diag0) - bias
        keep = (maskx_ref[m_idx, :, pl.ds(diag0, tq)] > 0.5) & (rc >= 0.0)
        s = jnp.where(keep, s, NEG)
        m0 = jnp.max(s, axis=1, keepdims=True)
        p = jnp.exp(s - m0)
        l0 = jnp.sum(p, axis=1, keepdims=True)
        acc0 = jnp.dot(p.astype(jnp.bfloat16), vb_ref[pl.ds(diag0, tq), :], preferred_element_type=jnp.float32)

        def body(n, carry):
            m_prev, l_prev, acc = carry
            start = pl.multiple_of(n * tq, tq)
            shift = slope * ((i - n) * tq).astype(jnp.float32)
            s = scores(start) - bias - shift
            s = jnp.where(maskx_ref[m_idx, :, pl.ds(start, tq)] > 0.5, s, NEG)
            m_new = jnp.maximum(m_prev, jnp.max(s, axis=1, keepdims=True))
            alpha = jnp.exp(m_prev - m_new)
            p = jnp.exp(s - m_new)
            l_new = alpha * l_prev + jnp.sum(p, axis=1, keepdims=True)
            acc = alpha * acc + jnp.dot(p.astype(jnp.bfloat16), vb_ref[pl.ds(start, tq), :],
                                        preferred_element_type=jnp.float32)
            return m_new, l_new, acc

        _, l_fin, acc = lax.fori_loop(0, i, body, (m0, l0, acc0))
        o = acc / l_fin
        if (q_off % LANES) // HEAD_DIM != kv_half:
            o = pltpu.roll(o, HEAD_DIM, 1)
        outs.append(o)

    for c in range(n_heads // 2):
        o_ref[0, :, c * LANES:(c + 1) * LANES] = jnp.where(lane < HEAD_DIM, outs[2 * c], outs[2 * c + 1])


def _block_attention(q, kv, mask, slopes, *, block, heads, n_masks, k_col0, v_col0):
    b, t, qcols = q.shape
    qw = (len(heads) // 2) * LANES
    n_steps = qcols // qw
    tq = ATTN_TILE
    expand = (np.arange(LANES)[:, None] == (np.arange(t)[None, :] // block)).astype(np.float32)
    expand = jnp.asarray(expand, jnp.bfloat16)
    return pl.pallas_call(
        functools.partial(_attn_kernel, heads, n_masks),
        grid=(b, n_steps, t // tq),
        in_specs=[pl.BlockSpec(memory_space=pltpu.SMEM),
                  pl.BlockSpec((1, tq, qw), lambda bi, s, i: (bi, i, s)),
                  pl.BlockSpec((1, t, LANES), lambda bi, s, i: (bi, 0, k_col0 + s)),
                  pl.BlockSpec((1, t, LANES), lambda bi, s, i: (bi, 0, v_col0 + s)),
                  pl.BlockSpec((1, tq, n_masks * LANES), lambda bi, s, i: (bi, i, s)),
                  pl.BlockSpec((LANES, t), lambda bi, s, i: (0, 0))],
        out_specs=pl.BlockSpec((1, tq, qw), lambda bi, s, i: (bi, i, s)),
        out_shape=jax.ShapeDtypeStruct((b, t, qcols), jnp.float32),
        scratch_shapes=[pltpu.VMEM((t, LANES), jnp.bfloat16), pltpu.VMEM((t, LANES), jnp.bfloat16),
                        pltpu.VMEM((n_masks, tq, t), jnp.float32)],
        compiler_params=pltpu.CompilerParams(dimension_semantics=("arbitrary", "arbitrary", "arbitrary")),
        name="block_attention",
    )(slopes, q, kv, kv, mask, expand)


PEER_CHUNK = 512
F32_MIN = float(np.finfo(np.float32).min)


def _extract_top(work, k):
    r_total, t = work.shape
    rowid = lax.broadcasted_iota(jnp.int32, (r_total, t), 0)
    kid = lax.broadcasted_iota(jnp.int32, (k, t), 0)

    def body(r, carry):
        w, vals = carry
        mx = jnp.max(w, axis=0, keepdims=True)
        first = jnp.min(jnp.where(w == mx, rowid, r_total), axis=0, keepdims=True)
        w = jnp.where(rowid == first, F32_MIN, w)
        vals = jnp.where(kid == r, mx, vals)
        return w, vals

    w, vals = lax.fori_loop(0, k, body, (work, jnp.zeros((k, t), jnp.float32)))
    return vals, w


def _peer_route_kernel(x_ref, g_ref, wq_ref, subk_ref, xb_ref, s0_ref, e0_ref, s1_ref, e1_ref, thr_ref):
    x = x_ref[...]
    xn = x * lax.rsqrt(jnp.mean(x * x, axis=-1, keepdims=True) + RMS_EPS) * g_ref[...]
    xb = xn.astype(jnp.bfloat16)
    xb_ref[...] = xb
    q = jnp.dot(xb, wq_ref[...], preferred_element_type=jnp.float32)
    half = PEER_QDIM // 2
    for p in range(PEER_HEADS):
        tabs = []
        for c in range(2):
            qpc = q[:, (2 * p + c) * half:(2 * p + c + 1) * half]
            s = lax.dot_general(qpc, subk_ref[2 * p + c], (((1,), (1,)), ((), ())),
                                precision=lax.Precision.HIGHEST, preferred_element_type=jnp.float32)
            st = s.T
            vals, w = _extract_top(st, PEER_TOPK)
            tabs.append((st, vals, w == F32_MIN))
        (s0, av, m0), (s1, bv, m1) = tabs
        cand = jnp.concatenate([av[a:a + 1, :] + bv for a in range(PEER_TOPK)], axis=0)
        best, _ = _extract_top(cand, PEER_TOPK)
        zsum = jnp.sum(jnp.exp(best - best[0:1, :]), axis=0, keepdims=True)
        s0_ref[p] = jnp.where(m0, s0, NEG)
        e0_ref[p] = jnp.where(m0, jnp.exp(s0 - av[0:1, :]), 0.0)
        s1_ref[p] = jnp.where(m1, s1, NEG)
        e1_ref[p] = jnp.where(m1, jnp.exp(s1 - bv[0:1, :]), 0.0) / zsum
        thr_ref[p:p + 1, :] = best[PEER_TOPK - 1:PEER_TOPK, :]


def _peer_expert_kernel(x_ref, xb_ref, s0_ref, e0_ref, s1_ref, e1_ref, thr_ref, u_ref, v_ref, o_ref):
    c = pl.program_id(1)

    @pl.when(c == 0)
    def _():
        o_ref[...] = x_ref[...]

    a = lax.dot_general(xb_ref[...], u_ref[...], (((1,), (1,)), ((), ())), preferred_element_type=jnp.float32)
    ga = jax.nn.gelu(a)
    hs = []
    for k in range(PEER_CHUNK // PEER_KEYS):
        i1 = c * (PEER_CHUNK // PEER_KEYS) + k
        wt = None
        for p in range(PEER_HEADS):
            row_s = s0_ref[p, pl.ds(i1, 1), :]
            row_e = e0_ref[p, pl.ds(i1, 1), :]
            term = jnp.where(row_s + s1_ref[p] >= thr_ref[p:p + 1, :], row_e * e1_ref[p], 0.0)
            wt = term if wt is None else wt + term
        hs.append((wt.T * ga[:, k * PEER_KEYS:(k + 1) * PEER_KEYS]).astype(jnp.bfloat16))
    h = jnp.concatenate(hs, axis=1)
    o_ref[...] += jnp.dot(h, v_ref[...], preferred_element_type=jnp.float32)


def _peer_residual(x, g_ffn, wq_b, subk, u_b, v_b):
    n, d = x.shape
    t1 = 256 if n % 256 == 0 else n
    hp = PEER_HEADS
    tab = jax.ShapeDtypeStruct((hp, PEER_KEYS, n), jnp.float32)
    tab_spec = pl.BlockSpec((hp, PEER_KEYS, t1), lambda i: (0, 0, i))
    xb, s0, e0, s1, e1, thr = pl.pallas_call(
        _peer_route_kernel,
        grid=(n // t1,),
        in_specs=[pl.BlockSpec((t1, d), lambda i: (i, 0)),
                  pl.BlockSpec((1, d), lambda i: (0, 0)),
                  pl.BlockSpec((d, hp * PEER_QDIM), lambda i: (0, 0)),
                  pl.BlockSpec((2 * hp, PEER_KEYS, PEER_QDIM // 2), lambda i: (0, 0, 0))],
        out_specs=[pl.BlockSpec((t1, d), lambda i: (i, 0)), tab_spec, tab_spec, tab_spec, tab_spec,
                   pl.BlockSpec((hp, t1), lambda i: (0, i))],
        out_shape=[jax.ShapeDtypeStruct((n, d), jnp.bfloat16), tab, tab, tab, tab,
                   jax.ShapeDtypeStruct((hp, n), jnp.float32)],
        compiler_params=pltpu.CompilerParams(dimension_semantics=("arbitrary",),
                                             vmem_limit_bytes=56 * 1024 * 1024),
        name="peer_route",
    )(x, g_ffn.reshape(1, d), wq_b, subk.reshape(2 * hp, PEER_KEYS, PEER_QDIM // 2))
    t2 = 512 if n % 512 == 0 else n
    tab_spec2 = pl.BlockSpec((hp, PEER_KEYS, t2), lambda i, c: (0, 0, i))
    return pl.pallas_call(
        _peer_expert_kernel,
        grid=(n // t2, PEER_EXPERTS // PEER_CHUNK),
        in_specs=[pl.BlockSpec((t2, d), lambda i, c: (i, 0)),
                  pl.BlockSpec((t2, d), lambda i, c: (i, 0)),
                  tab_spec2, tab_spec2, tab_spec2, tab_spec2,
                  pl.BlockSpec((hp, t2), lambda i, c: (0, i)),
                  pl.BlockSpec((PEER_CHUNK, d), lambda i, c: (c, 0)),
                  pl.BlockSpec((PEER_CHUNK, d), lambda i, c: (c, 0))],
        out_specs=pl.BlockSpec((t2, d), lambda i, c: (i, 0)),
        out_shape=jax.ShapeDtypeStruct((n, d), jnp.float32),
        compiler_params=pltpu.CompilerParams(dimension_semantics=("arbitrary", "arbitrary"),
                                             vmem_limit_bytes=56 * 1024 * 1024),
        name="peer_experts",
    )(x, xb, s0, e0, s1, e1, thr, u_b, v_b)


def rms_norm(x, g):
    xf = x.astype(jnp.float32)
    y = xf * lax.rsqrt(jnp.mean(xf * xf, axis=-1, keepdims=True) + RMS_EPS)
    return (y * g.astype(jnp.float32)).astype(x.dtype)


def alibi_slopes():
    n = NSA_HEADS + MOBA_HEADS
    s = jnp.exp2(-8.0 * jnp.arange(1, n + 1, dtype=jnp.float32) / n)
    return s[0::2], s[1::2]


def pad_rows(a, length):
    extra = length - a.shape[1]
    if extra == 0:
        return a
    return jnp.pad(a, [(0, 0), (0, extra)] + [(0, 0)] * (a.ndim - 2))


def masked_softmax(s, mask):
    s = jnp.where(mask, s, NEG)
    m = jnp.max(s, axis=-1, keepdims=True)
    e = jnp.where(mask, jnp.exp(s - m), 0.0)
    l = jnp.sum(e, axis=-1, keepdims=True)
    safe = jnp.where(l > 0, l, 1.0)
    lse = jnp.where(l > 0, m + jnp.log(safe), NEG)
    return e / safe, lse[..., 0]


def merge_partials(o1, lse1, o2, lse2):
    lse = jnp.logaddexp(lse1, lse2)
    return jnp.exp(lse1 - lse)[..., None] * o1 + jnp.exp(lse2 - lse)[..., None] * o2


def sweep_queries(fn, block, q_pos, *xs):
    t = q_pos.shape[0]
    if t <= block or t % block:
        return fn(q_pos, *xs)
    nb = t // block

    def split(a):
        return jnp.moveaxis(a.reshape((a.shape[0], nb, block) + a.shape[2:]), 1, 0)

    def join(a):
        a = jnp.moveaxis(a, 0, 1)
        return a.reshape((a.shape[0], t) + a.shape[3:])

    outs = lax.map(lambda args: fn(*args), (q_pos.reshape(nb, block),) + tuple(split(a) for a in xs))
    return jax.tree_util.tree_map(join, outs)


def gathered_block_attention(q, k_blocks, v_blocks, idx, valid, q_pos, slopes, block):
    b, t, g, r, dk = q.shape
    n = idx.shape[-1]
    bi = jnp.arange(b)[:, None, None, None]
    gi = jnp.arange(g)[None, None, :, None]
    kg = k_blocks[bi, idx, :, gi].reshape(b, t, g, n * block, dk)
    vg = v_blocks[bi, idx, :, gi].reshape(b, t, g, n * block, dk)
    k_pos = (idx[..., None] * block + jnp.arange(block)).reshape(b, t, g, n * block)
    dist = q_pos[None, :, None, None] - k_pos
    mask = (dist >= 0) & jnp.broadcast_to(valid[..., None], (b, t, g, n, block)).reshape(b, t, g, n * block)
    s = jnp.einsum('btgrd,btgmd->btgrm', q, kg, preferred_element_type=jnp.float32) * SCALE
    s = s - slopes[None, None, :, :, None] * dist.astype(jnp.float32)[:, :, :, None, :]
    p, lse = masked_softmax(s, mask[:, :, :, None, :])
    o = jnp.einsum('btgrm,btgmd->btgrd', p, vg, preferred_element_type=jnp.float32)
    return o, lse


def own_block_attention(q_blk, k_blk, v_blk, slopes):
    b, nb, mb, h, dk = q_blk.shape
    s = jnp.einsum('bnqhd,bnkhd->bnhqk', q_blk, k_blk, preferred_element_type=jnp.float32) * SCALE
    dist = jnp.arange(mb)[:, None] - jnp.arange(mb)[None, :]
    s = s - slopes[:, None, None] * dist.astype(jnp.float32)
    p, lse = masked_softmax(s, dist >= 0)
    o = jnp.einsum('bnhqk,bnkhd->bnqhd', p, v_blk, preferred_element_type=jnp.float32)
    return o.reshape(b, nb * mb, h, dk), jnp.moveaxis(lse, 2, 3).reshape(b, nb * mb, h)


def window_attention_banded(q, k, v, slopes):
    b, s_len, g, r, dk = q.shape
    nb = s_len // QBLOCK
    w = WINDOW // QBLOCK
    kp = jnp.concatenate([jnp.zeros((b, WINDOW, g, dk), k.dtype), k], axis=1).reshape(b, nb + w, QBLOCK, g, dk)
    vp = jnp.concatenate([jnp.zeros((b, WINDOW, g, dk), v.dtype), v], axis=1).reshape(b, nb + w, QBLOCK, g, dk)
    kband = jnp.concatenate([kp[:, i:i + nb] for i in range(w + 1)], axis=2)
    vband = jnp.concatenate([vp[:, i:i + nb] for i in range(w + 1)], axis=2)
    qb = q.reshape(b, nb, QBLOCK, g, r, dk)
    s = jnp.einsum('bnqgrd,bnkgd->bngrqk', qb, kband, preferred_element_type=jnp.float32) * SCALE
    q_pos = jnp.arange(nb)[:, None] * QBLOCK + jnp.arange(QBLOCK)[None, :]
    k_pos = jnp.arange(nb)[:, None] * QBLOCK - WINDOW + jnp.arange((w + 1) * QBLOCK)[None, :]
    dist = q_pos[:, :, None] - k_pos[:, None, :]
    mask = (dist >= 0) & (dist <= WINDOW) & (k_pos >= 0)[:, None, :]
    s = s - slopes[None, None, :, :, None, None] * dist.astype(jnp.float32)[None, :, None, None]
    p, _ = masked_softmax(s, mask[None, :, None, None])
    o = jnp.einsum('bngrqk,bnkgd->bnqgrd', p, vband, preferred_element_type=jnp.float32)
    return o.reshape(b, s_len, g, r, dk)


def window_attention_dense(q, k, v, q_pos, k_pos, slopes):
    s = jnp.einsum('btgrd,bkgd->btgrk', q, k, preferred_element_type=jnp.float32) * SCALE
    dist = q_pos[:, None] - k_pos[None, :]
    mask = (dist >= 0) & (dist <= WINDOW)
    s = s - slopes[None, None, :, :, None] * dist.astype(jnp.float32)[None, :, None, None, :]
    p, _ = masked_softmax(s, mask[None, :, None, None, :])
    return jnp.einsum('btgrk,bkgd->btgrd', p, v, preferred_element_type=jnp.float32)


def compress_blocks(kv, pos_emb, w1, w2):
    b, length, g, dk = kv.shape
    n_chunks = length // CMP_STRIDE
    n_sub = CMP_LEN // CMP_STRIDE
    n_cmp = n_chunks - n_sub + 1
    chunks = kv[:, :n_chunks * CMP_STRIDE].reshape(b, n_chunks, CMP_STRIDE, g, dk)
    blocks = jnp.concatenate([chunks[:, i:i + n_cmp] for i in range(n_sub)], axis=2)
    blocks = blocks + pos_emb[None, None, :, None, :]
    flat = jnp.moveaxis(blocks, 3, 2).reshape(b, n_cmp, g, CMP_LEN * dk)
    return jax.nn.gelu(flat @ w1) @ w2


MOBA_STEP_HEADS = ((0, 0, 0), (HEAD_DIM, 1, 1))
NSA_STEP_HEADS = tuple((g * NSA_GROUP * HEAD_DIM + r * HEAD_DIM, g, g) for g in range(2) for r in range(NSA_GROUP))


def _block_mask(idx, valid):
    jj = lax.broadcasted_iota(jnp.int32, idx.shape[:-1] + (1, LANES), idx.ndim)
    hit = (idx[..., None] == jj) & valid[..., None]
    return jnp.any(hit, axis=-2).astype(jnp.float32)


def nsa_mixer(q, gates, ctx, o_win, q_pos, slopes, lw, prompt):
    b, t = q.shape[:2]
    sl = slopes.reshape(NSA_KV_HEADS, NSA_GROUP)
    kc = rms_norm(compress_blocks(ctx[:, :, 0], lw['cmp_pe_k'], lw['cmp_w1_k'], lw['cmp_w2_k']), lw['g_k_cmp'])
    vc = compress_blocks(ctx[:, :, 1], lw['cmp_pe_v'], lw['cmp_w1_v'], lw['cmp_w2_v'])
    n_cmp = kc.shape[1]
    c_start = jnp.arange(n_cmp) * CMP_STRIDE
    dist = q_pos[:, None] - (c_start + CMP_LEN - 1)[None, :]
    s = jnp.einsum('btgrd,bcgd->btgrc', q, kc, preferred_element_type=jnp.float32) * SCALE
    s = s - sl[None, None, :, :, None] * dist.astype(jnp.float32)[None, :, None, None, :]
    p_cmp, _ = masked_softmax(s, (dist >= 0)[None, :, None, None, :])
    o_cmp = jnp.einsum('btgrc,bcgd->btgrd', p_cmp, vc, preferred_element_type=jnp.float32)
    length = ctx.shape[1]
    n_sel = -(-length // SEL_BLOCK)
    j_start = jnp.arange(n_sel) * SEL_BLOCK
    overlap = ((c_start[:, None] < j_start[None, :] + SEL_BLOCK)
               & (c_start[:, None] + CMP_LEN > j_start[None, :])).astype(jnp.float32)
    imp = jnp.einsum('btgrc,cj->btgj', p_cmp, overlap)
    jj = jnp.arange(n_sel)[None, :]
    own = (q_pos // SEL_BLOCK)[:, None]
    imp = jnp.where((jj == own)[None, :, None, :], OWN_SCORE,
                    jnp.where((jj < own)[None, :, None, :], imp, NEG))
    val, idx = lax.top_k(imp, min(SEL_TOPK, n_sel))
    valid = val > 0.5 * NEG
    if prompt:
        mask = _block_mask(idx, valid).reshape(b, t, NSA_KV_HEADS * LANES)
        o_slc = _block_attention(q.reshape(b, t, NSA_WIDTH), ctx.reshape(b, t, 4 * NSA_KV_WIDTH), mask, slopes,
                                 block=SEL_BLOCK, heads=NSA_STEP_HEADS, n_masks=2, k_col0=4, v_col0=6)
        o_slc = o_slc.reshape(q.shape)
    else:
        k_blk = pad_rows(ctx[:, :, 2], n_sel * SEL_BLOCK).reshape(b, n_sel, SEL_BLOCK, NSA_KV_HEADS, HEAD_DIM)
        v_blk = pad_rows(ctx[:, :, 3], n_sel * SEL_BLOCK).reshape(b, n_sel, SEL_BLOCK, NSA_KV_HEADS, HEAD_DIM)

        def sel_fn(qp, qb, ib, vb):
            return gathered_block_attention(qb, k_blk, v_blk, ib, vb, qp, sl, SEL_BLOCK)[0]

        o_slc = sweep_queries(sel_fn, QBLOCK, q_pos, q, idx, valid)
    o = gates[..., 0:1] * o_cmp + gates[..., 1:2] * o_slc + gates[..., 2:3] * o_win
    return o.reshape(b, t, NSA_WIDTH)


def moba_mixer(q, k_ctx, v_ctx, q_pos, slopes, prompt):
    b, t, h, dk = q.shape
    length = k_ctx.shape[1]
    nb = -(-length // MOBA_BLOCK)
    lp = nb * MOBA_BLOCK
    k_blk = pad_rows(k_ctx, lp).reshape(b, nb, MOBA_BLOCK, h, dk)
    v_blk = pad_rows(v_ctx, lp).reshape(b, nb, MOBA_BLOCK, h, dk)
    means = jnp.mean(k_blk.astype(jnp.float32), axis=2)
    gate = jnp.einsum('bthd,bnhd->bthn', q, means, preferred_element_type=jnp.float32)
    past = jnp.arange(nb)[None, :] < (q_pos // MOBA_BLOCK)[:, None]
    gate = jnp.where(past[None, :, None, :], gate, NEG)
    val, idx = lax.top_k(gate, min(MOBA_TOPK, nb))
    valid = val > 0.5 * NEG
    if prompt:
        own = (lax.broadcasted_iota(jnp.int32, (t, 1, LANES), 2) == (q_pos // MOBA_BLOCK)[:, None, None])
        mask = (_block_mask(idx, valid) + own.astype(jnp.float32)).reshape(b, t, h * LANES)
        kv = jnp.concatenate([k_ctx.reshape(b, t, h * dk), v_ctx.reshape(b, t, h * dk)], axis=-1)
        return _block_attention(q.reshape(b, t, h * dk), kv, mask, slopes, block=MOBA_BLOCK,
                                heads=MOBA_STEP_HEADS, n_masks=2, k_col0=0, v_col0=h * dk // LANES)
    q5 = q[:, :, :, None, :]
    sl = slopes[:, None]

    def sel_fn(qp, qb, ib, vb):
        return gathered_block_attention(qb, k_blk, v_blk, ib, vb, qp, sl, MOBA_BLOCK)

    o_sel, lse_sel = sweep_queries(sel_fn, MOBA_QBLOCK, q_pos, q5, idx, valid)
    if prompt:
        q_blk = pad_rows(q, lp).reshape(b, nb, MOBA_BLOCK, h, dk)
        o_own, lse_own = own_block_attention(q_blk, k_blk, v_blk, slopes)
        o_own = o_own[:, :t, :, None]
        lse_own = lse_own[:, :t, :, None]
    else:
        own = jnp.broadcast_to((q_pos // MOBA_BLOCK)[None, :, None, None], (b, t, h, 1))

        def own_fn(qp, qb, ib):
            return gathered_block_attention(qb, k_blk, v_blk, ib, jnp.ones(ib.shape, bool), qp, sl, MOBA_BLOCK)

        o_own, lse_own = sweep_queries(own_fn, MOBA_QBLOCK, q_pos, q5, own)
    o = merge_partials(o_sel, lse_sel, o_own, lse_own)
    return o.reshape(b, t, MOBA_WIDTH)


def peer_ffn(h, w_q, sub_keys, expert_u, expert_v):
    b, t, d = h.shape
    n = b * t
    n_blk = -(-n // PEER_TBLOCK)
    xt = h.reshape(n, d)
    if n_blk * PEER_TBLOCK != n:
        xt = jnp.pad(xt, ((0, n_blk * PEER_TBLOCK - n), (0, 0)))
    q = _mm(xt, w_q).reshape(-1, PEER_HEADS, 2, PEER_QDIM // 2)
    s = jnp.einsum('nphd,phkd->nphk', q, sub_keys, preferred_element_type=jnp.float32)
    top_s, top_i = lax.top_k(s, PEER_TOPK)
    cand_s = (top_s[:, :, 0, :, None] + top_s[:, :, 1, None, :]).reshape(-1, PEER_HEADS, PEER_TOPK * PEER_TOPK)
    cand_i = (top_i[:, :, 0, :, None] * PEER_KEYS + top_i[:, :, 1, None, :]).reshape(-1, PEER_HEADS, PEER_TOPK * PEER_TOPK)
    best_s, pos = lax.top_k(cand_s, PEER_TOPK)
    e_idx = jnp.take_along_axis(cand_i, pos, axis=-1)
    g = jax.nn.softmax(best_s, axis=-1)

    def block_fn(args):
        xb, eb, gb = args
        a = jax.nn.gelu(jnp.einsum('td,tpkd->tpk', xb, expert_u[eb], preferred_element_type=jnp.float32))
        return jnp.einsum('tpk,tpkd->td', (gb * a).astype(xb.dtype), expert_v[eb], preferred_element_type=jnp.float32)

    out = lax.map(block_fn, (xt.reshape(n_blk, PEER_TBLOCK, d),
                             e_idx.reshape(n_blk, PEER_TBLOCK, PEER_HEADS, PEER_TOPK),
                             g.reshape(n_blk, PEER_TBLOCK, PEER_HEADS, PEER_TOPK)))
    return out.reshape(-1, d)[:n].reshape(b, t, d).astype(h.dtype)


def layer_forward(x, ple, lw, slopes_nsa, slopes_moba, past_nsa=None, past_moba=None, win_buf=None):
    b, t, _ = x.shape
    h = rms_norm(x, lw['g_mix'])
    sizes = [NSA_WIDTH] + [NSA_KV_WIDTH] * 6 + [N_GATE] + [MOBA_WIDTH] * 3
    offsets = np.cumsum(sizes)[:-1].tolist()
    q_n, kc, vc, ks, vs, kw, vw, gt, q_m, k_m, v_m = jnp.split(_mm(h, lw['w_in']), offsets, axis=-1)

    def heads(a, nh):
        return a.reshape(b, t, nh, HEAD_DIM)

    q_n = rms_norm(heads(q_n, NSA_HEADS), lw['g_q_nsa']).reshape(b, t, NSA_KV_HEADS, NSA_GROUP, HEAD_DIM)
    kc = heads(kc, NSA_KV_HEADS)
    vc = heads(vc, NSA_KV_HEADS)
    ks = rms_norm(heads(ks, NSA_KV_HEADS), lw['g_k_slc'])
    vs = heads(vs, NSA_KV_HEADS)
    kw = rms_norm(heads(kw, NSA_KV_HEADS), lw['g_k_win'])
    vw = heads(vw, NSA_KV_HEADS)
    gates = jax.nn.sigmoid(gt.astype(jnp.float32)).reshape(b, t, NSA_KV_HEADS, NSA_GROUP, 3)
    q_m = rms_norm(heads(q_m, MOBA_HEADS), lw['g_q_moba'])
    k_m = rms_norm(heads(k_m, MOBA_HEADS), lw['g_k_moba'])
    v_m = heads(v_m, MOBA_HEADS)
    nsa_rows = jnp.stack([kc, vc, ks, vs], axis=2)
    moba_rows = jnp.stack([k_m, v_m], axis=2)
    win_rows = jnp.stack([kw, vw], axis=2)
    if past_nsa is None:
        q_pos = jnp.arange(t)
        nsa_ctx, moba_ctx = nsa_rows, moba_rows
        o_win = window_attention_banded(q_n, kw, vw, slopes_nsa.reshape(NSA_KV_HEADS, NSA_GROUP))
        new_win = win_rows[:, t - min(WINDOW, t):]
    else:
        p_len = past_nsa.shape[1]
        q_pos = p_len + jnp.arange(t)
        nsa_ctx = jnp.concatenate([past_nsa, nsa_rows], axis=1)
        moba_ctx = jnp.concatenate([past_moba, moba_rows], axis=1)
        buf = jnp.concatenate([win_buf, win_rows], axis=1)
        k_pos = p_len - win_buf.shape[1] + jnp.arange(buf.shape[1])
        o_win = window_attention_dense(q_n, buf[:, :, 0], buf[:, :, 1], q_pos, k_pos,
                                       slopes_nsa.reshape(NSA_KV_HEADS, NSA_GROUP))
        new_win = buf[:, t:]
    o_nsa = nsa_mixer(q_n, gates, nsa_ctx, o_win, q_pos, slopes_nsa, lw, past_nsa is None)
    o_moba = moba_mixer(q_m, moba_ctx[:, :, 0], moba_ctx[:, :, 1], q_pos, slopes_moba, past_nsa is None)
    mix = jnp.concatenate([rms_norm(o_nsa, lw['g_out_nsa']), rms_norm(o_moba, lw['g_out_moba'])], axis=-1)
    x = x + _mm(mix.astype(x.dtype), lw['w_out'])
    x = _peer_residual(x.reshape(b * t, D_MODEL), lw['g_ffn'], lw['w_peer_q_b'], lw['peer_sub_keys'],
                       lw['peer_u_b'], lw['peer_v_b']).reshape(b, t, D_MODEL)
    gate = jax.nn.sigmoid(_mm(rms_norm(x, lw['g_ple']), lw['w_ple_gate']))
    x = x + gate * rms_norm(_mm(ple, lw['w_ple_proj']), lw['g_ple_post'])
    return x, nsa_rows, moba_rows, new_win


def kernel(x_prompt, x_sample, cache_nsa, cache_moba, state_win, page_table, p_prompt, p_sample,
           g_mix, w_in, g_q_nsa, g_k_cmp, g_k_slc, g_k_win, g_q_moba, g_k_moba,
           cmp_pe_k, cmp_w1_k, cmp_w2_k, cmp_pe_v, cmp_w1_v, cmp_w2_v,
           g_out_nsa, g_out_moba, w_out, g_ffn, w_peer_q, peer_sub_keys, peer_u, peer_v,
           g_ple, w_ple_gate, w_ple_proj, g_ple_post):
    slopes_nsa, slopes_moba = alibi_slopes()
    n_seq, n_pages = page_table.shape
    i = 0
    lw = {
        'g_mix': g_mix[i], 'w_in': w_in[i], 'g_q_nsa': g_q_nsa[i], 'g_k_cmp': g_k_cmp[i],
        'g_k_slc': g_k_slc[i], 'g_k_win': g_k_win[i], 'g_q_moba': g_q_moba[i], 'g_k_moba': g_k_moba[i],
        'cmp_pe_k': cmp_pe_k[i], 'cmp_w1_k': cmp_w1_k[i], 'cmp_w2_k': cmp_w2_k[i],
        'cmp_pe_v': cmp_pe_v[i], 'cmp_w1_v': cmp_w1_v[i], 'cmp_w2_v': cmp_w2_v[i],
        'g_out_nsa': g_out_nsa[i], 'g_out_moba': g_out_moba[i], 'w_out': w_out[i], 'g_ffn': g_ffn[i],
        'w_peer_q': w_peer_q[i], 'peer_sub_keys': peer_sub_keys[i], 'peer_u': peer_u[i], 'peer_v': peer_v[i],
        'g_ple': g_ple[i], 'w_ple_gate': w_ple_gate[i], 'w_ple_proj': w_ple_proj[i], 'g_ple_post': g_ple_post[i],
        'w_peer_q_b': w_peer_q[i].astype(jnp.bfloat16), 'peer_u_b': peer_u[i].astype(jnp.bfloat16),
        'peer_v_b': peer_v[i].astype(jnp.bfloat16),
    }
    y_prompt, nsa_p, moba_p, win_p = layer_forward(x_prompt, p_prompt[i], lw, slopes_nsa, slopes_moba)
    past_nsa = cache_nsa[i][page_table].reshape((n_seq, n_pages * PAGE_SIZE) + cache_nsa.shape[3:])
    past_moba = cache_moba[i][page_table].reshape((n_seq, n_pages * PAGE_SIZE) + cache_moba.shape[3:])
    y_sample, nsa_s, moba_s, win_s = layer_forward(x_sample, p_sample[i], lw, slopes_nsa, slopes_moba,
                                                   past_nsa, past_moba, state_win[i])
    return (y_prompt, y_sample, nsa_p[None], moba_p[None], win_p[None], nsa_s[None], moba_s[None], win_s[None])
```

```python
import functools

import jax
import jax.numpy as jnp
from jax import lax
import numpy as np
from jax.experimental import pallas as pl
from jax.experimental.pallas import tpu as pltpu

D_MODEL = 2048
BATCH = 4
SEQ = 2048
DEPTH = 1
DEC_BATCH = 128
DEC_SEQ = 1
PAST_LEN = 2048
PAGE_SIZE = 128

HEAD_DIM = 64
NSA_HEADS = D_MODEL // (2 * HEAD_DIM)
NSA_KV_HEADS = max(1, NSA_HEADS // 4)
NSA_GROUP = NSA_HEADS // NSA_KV_HEADS
MOBA_HEADS = D_MODEL // (2 * HEAD_DIM)
NSA_WIDTH = NSA_HEADS * HEAD_DIM
NSA_KV_WIDTH = NSA_KV_HEADS * HEAD_DIM
MOBA_WIDTH = MOBA_HEADS * HEAD_DIM
N_GATE = 3 * NSA_HEADS
N_IN_COLS = NSA_WIDTH + 6 * NSA_KV_WIDTH + N_GATE + 3 * MOBA_WIDTH
CMP_LEN = 32
CMP_STRIDE = 16
CMP_HIDDEN = 64
SEL_BLOCK = 64
SEL_TOPK = 8
WINDOW = 512
MOBA_BLOCK = 256
MOBA_TOPK = 3
PEER_KEYS = 128
PEER_EXPERTS = PEER_KEYS * PEER_KEYS
PEER_HEADS = 8
PEER_TOPK = 16
PEER_QDIM = 256
PLE_DIM = 256
QBLOCK = 128
MOBA_QBLOCK = 64
PEER_TBLOCK = 128
RMS_EPS = 1e-6
NEG = -1e30
OWN_SCORE = 1e9
SCALE = HEAD_DIM ** -0.5


def _mm_kernel(x_ref, w_ref, o_ref):
    o_ref[...] = jnp.dot(x_ref[...].astype(jnp.bfloat16), w_ref[...].astype(jnp.bfloat16),
                         preferred_element_type=jnp.float32)


def _mm(x, w):
    lead = x.shape[:-1]
    k = x.shape[-1]
    n = w.shape[-1]
    x2 = x.reshape(-1, k)
    m = x2.shape[0]
    tm = 512 if m % 512 == 0 else m
    n_pad = -(-n // 128) * 128
    if n_pad != n:
        w = jnp.pad(w, ((0, 0), (0, n_pad - n)))
    tn = 512 if n_pad % 512 == 0 else (256 if n_pad % 256 == 0 else 128)
    out = pl.pallas_call(
        _mm_kernel,
        grid=(m // tm, n_pad // tn),
        in_specs=[pl.BlockSpec((tm, k), lambda i, j: (i, 0)),
                  pl.BlockSpec((k, tn), lambda i, j: (0, j))],
        out_specs=pl.BlockSpec((tm, tn), lambda i, j: (i, j)),
        out_shape=jax.ShapeDtypeStruct((m, n_pad), jnp.float32),
        compiler_params=pltpu.CompilerParams(dimension_semantics=("arbitrary", "arbitrary")),
        name="mm",
    )(x2, w)
    if n_pad != n:
        out = out[:, :n]
    return out.reshape(lead + (n,))


LANES = 128
ATTN_TILE = 256


F32_MIN = float(np.finfo(np.float32).min)


def _top_lanes(score, k):
    lane = lax.broadcasted_iota(jnp.int32, score.shape, 1)
    sel = jnp.zeros(score.shape, jnp.float32)
    for _ in range(k):
        m = jnp.max(score, axis=1, keepdims=True)
        first = jnp.min(jnp.where(score == m, lane, LANES), axis=1, keepdims=True)
        hit = lane == first
        sel = jnp.where(hit & (m > 0.5 * NEG), 1.0, sel)
        score = jnp.where(hit, F32_MIN, score)
    return sel


def _attn_kernel(heads, mode, *refs):
    if mode == 'mask':
        slopes_ref, q_ref, k_ref, v_ref, mask_ref, expand_ref, o_ref, kb_ref, vb_ref, maskx_ref = refs
    elif mode == 'gate':
        slopes_ref, q_ref, k_ref, v_ref, expand_ref, o_ref, kb_ref, vb_ref, maskx_ref, means_ref = refs
    else:
        slopes_ref, q_ref, k_ref, v_ref, o_ref, kb_ref, vb_ref = refs
    s_idx = pl.program_id(1)
    i = pl.program_id(2)
    tq = ATTN_TILE
    n_heads = len(heads)
    t_total = k_ref.shape[1]

    @pl.when(i == 0)
    def _():
        kb_ref[...] = k_ref[0].astype(jnp.bfloat16)
        vb_ref[...] = v_ref[0].astype(jnp.bfloat16)
        if mode == 'gate':
            means_ref[...] = jnp.zeros_like(means_ref)
            for n in range(t_total // MOBA_BLOCK):
                means_ref[n:n + 1, :] = jnp.mean(k_ref[0, n * MOBA_BLOCK:(n + 1) * MOBA_BLOCK, :], axis=0,
                                                 keepdims=True)

    lane = lax.broadcasted_iota(jnp.int32, (tq, LANES), 1)
    if mode == 'mask':
        for m in range(mask_ref.shape[2] // LANES):
            mb = mask_ref[0, :, m * LANES:(m + 1) * LANES].astype(jnp.bfloat16)
            maskx_ref[m] = jnp.dot(mb, expand_ref[...], preferred_element_type=jnp.float32)

    row = lax.broadcasted_iota(jnp.int32, (tq, tq), 0)
    col = lax.broadcasted_iota(jnp.int32, (tq, tq), 1)
    rc = (row - col).astype(jnp.float32)
    diag0 = pl.multiple_of(i * tq, tq)

    outs = []
    for h, (q_off, kv_half, m_idx) in enumerate(heads):
        slope = slopes_ref[s_idx * n_heads + h]
        chunk = q_off // LANES
        qc = q_ref[0, :, chunk * LANES:(chunk + 1) * LANES]
        if (q_off % LANES) // HEAD_DIM != kv_half:
            qc = pltpu.roll(qc, HEAD_DIM, 1)
        in_half = (lane >= HEAD_DIM) if kv_half else (lane < HEAD_DIM)
        qh = jnp.where(in_half, qc, 0.0)
        qb = (qh * SCALE).astype(jnp.bfloat16)
        bias = slope * rc
        if mode == 'gate':
            gate = _nt(qh, means_ref[...], precision=lax.Precision.HIGHEST)
            sel = _top_lanes(jnp.where(lane < i, gate, NEG), MOBA_TOPK)
            sel = jnp.where(lane == i, 1.0, sel).astype(jnp.bfloat16)
            maskx_ref[m_idx] = jnp.dot(sel, expand_ref[...], preferred_element_type=jnp.float32)

        def scores(start):
            kblk = kb_ref[pl.ds(start, tq), :]
            return lax.dot_general(qb, kblk, (((1,), (1,)), ((), ())), preferred_element_type=jnp.float32)

        s = scores(diag0) - bias
        keep = rc >= 0.0
        if mode != 'window':
            keep = keep & (maskx_ref[m_idx, :, pl.ds(diag0, tq)] > 0.5)
        s = jnp.where(keep, s, NEG)
        m0 = jnp.max(s, axis=1, keepdims=True)
        p = jnp.exp(s - m0)
        l0 = jnp.sum(p, axis=1, keepdims=True)
        acc0 = jnp.dot(p.astype(jnp.bfloat16), vb_ref[pl.ds(diag0, tq), :], preferred_element_type=jnp.float32)

        def body(n, carry):
            m_prev, l_prev, acc = carry
            start = pl.multiple_of(n * tq, tq)
            shift = slope * ((i - n) * tq).astype(jnp.float32)
            s = scores(start) - bias - shift
            if mode == 'window':
                limit = jnp.where(n == i - WINDOW // tq, 0.0, float(tq))
                s = jnp.where(rc > limit, NEG, s)
            else:
                s = jnp.where(maskx_ref[m_idx, :, pl.ds(start, tq)] > 0.5, s, NEG)
            m_new = jnp.maximum(m_prev, jnp.max(s, axis=1, keepdims=True))
            alpha = jnp.exp(m_prev - m_new)
            p = jnp.exp(s - m_new)
            l_new = alpha * l_prev + jnp.sum(p, axis=1, keepdims=True)
            acc = alpha * acc + jnp.dot(p.astype(jnp.bfloat16), vb_ref[pl.ds(start, tq), :],
                                        preferred_element_type=jnp.float32)
            return m_new, l_new, acc

        first_tile = jnp.maximum(i - WINDOW // tq, 0) if mode == 'window' else 0
        _, l_fin, acc = lax.fori_loop(first_tile, i, body, (m0, l0, acc0))
        o = acc / l_fin
        if (q_off % LANES) // HEAD_DIM != kv_half:
            o = pltpu.roll(o, HEAD_DIM, 1)
        outs.append(o)

    for c in range(n_heads // 2):
        o_ref[0, :, c * LANES:(c + 1) * LANES] = jnp.where(lane < HEAD_DIM, outs[2 * c], outs[2 * c + 1])


def _block_attention(q, kv, slopes, *, mode, heads, k_col0, v_col0, mask=None, block=None, n_masks=0):
    b, t, qcols = q.shape
    qw = (len(heads) // 2) * LANES
    n_steps = qcols // qw
    tq = ATTN_TILE
    assert mode != 'gate' or MOBA_BLOCK == tq
    in_specs = [pl.BlockSpec(memory_space=pltpu.SMEM),
                pl.BlockSpec((1, tq, qw), lambda bi, s, i: (bi, i, s)),
                pl.BlockSpec((1, t, LANES), lambda bi, s, i: (bi, 0, k_col0 + s)),
                pl.BlockSpec((1, t, LANES), lambda bi, s, i: (bi, 0, v_col0 + s))]
    args = [slopes, q, kv, kv]
    scratch = [pltpu.VMEM((t, LANES), jnp.bfloat16), pltpu.VMEM((t, LANES), jnp.bfloat16)]
    if mode == 'mask':
        in_specs.append(pl.BlockSpec((1, tq, n_masks * LANES), lambda bi, s, i: (bi, i, s)))
        args.append(mask)
    if mode != 'window':
        expand = (np.arange(LANES)[:, None] == (np.arange(t)[None, :] // block)).astype(np.float32)
        in_specs.append(pl.BlockSpec((LANES, t), lambda bi, s, i: (0, 0)))
        args.append(jnp.asarray(expand, jnp.bfloat16))
        scratch.append(pltpu.VMEM((max(n_masks, 1) if mode == 'mask' else len(heads), tq, t), jnp.float32))
    if mode == 'gate':
        scratch.append(pltpu.VMEM((LANES, LANES), jnp.float32))
    return pl.pallas_call(
        functools.partial(_attn_kernel, heads, mode),
        grid=(b, n_steps, t // tq),
        in_specs=in_specs,
        out_specs=pl.BlockSpec((1, tq, qw), lambda bi, s, i: (bi, i, s)),
        out_shape=jax.ShapeDtypeStruct((b, t, qcols), jnp.float32),
        scratch_shapes=scratch,
        compiler_params=pltpu.CompilerParams(dimension_semantics=("arbitrary", "arbitrary", "arbitrary")),
        name="attention_" + mode,
    )(*args)


def _cmp_select_kernel(slopes_ref, q_ref, abk_ref, abv_ref, pebk_ref, pebv_ref, w2k_ref, w2v_ref, gk_ref, bd_ref,
                       ov_ref, o_ref, mask_ref, kcb_ref, vcb_ref):
    i = pl.program_id(1)
    tq = ATTN_TILE
    kvw = NSA_KV_WIDTH

    @pl.when(i == 0)
    def _():
        def second_layer(ab_ref, peb_ref, w2_ref):
            ab = ab_ref[0]
            h = jax.nn.gelu(ab[:, :kvw] + pltpu.roll(ab[:, kvw:], N_CMP_CHUNKS - 1, 0) + peb_ref[...])
            return jnp.dot(h.astype(jnp.bfloat16), w2_ref[...], preferred_element_type=jnp.float32)

        kc = second_layer(abk_ref, pebk_ref, w2k_ref)
        ms = jnp.dot(kc * kc, bd_ref[...], precision=lax.Precision.HIGHEST, preferred_element_type=jnp.float32)
        kcb_ref[...] = (kc * lax.rsqrt(ms + RMS_EPS) * gk_ref[...]).astype(jnp.bfloat16)
        vcb_ref[...] = second_layer(abv_ref, pebv_ref, w2v_ref).astype(jnp.bfloat16)

    lane = lax.broadcasted_iota(jnp.int32, (tq, LANES), 1)
    t_pos = i * tq + lax.broadcasted_iota(jnp.int32, (tq, LANES), 0)
    dist_i = t_pos - (CMP_STRIDE * lane + CMP_LEN - 1)
    keep = (dist_i >= 0) & (lane < N_CMP_CHUNKS - 1)
    dist = dist_i.astype(jnp.float32)
    own = jnp.right_shift(t_pos, SEL_BLOCK.bit_length() - 1)
    outs = []
    for g in range(NSA_KV_HEADS):
        kchunk, khalf = divmod(g, 2)
        kc_g = kcb_ref[:, kchunk * LANES:(kchunk + 1) * LANES]
        vc_g = vcb_ref[:, kchunk * LANES:(kchunk + 1) * LANES]
        in_half = (lane >= HEAD_DIM) if khalf else (lane < HEAD_DIM)
        psum = None
        for r in range(NSA_GROUP):
            h = g * NSA_GROUP + r
            qc = q_ref[0, :, (h // 2) * LANES:(h // 2 + 1) * LANES]
            if h % 2 != khalf:
                qc = pltpu.roll(qc, HEAD_DIM, 1)
            qb = (jnp.where(in_half, qc, 0.0) * SCALE).astype(jnp.bfloat16)
            s = jnp.where(keep, _nt(qb, kc_g) - slopes_ref[h] * dist, NEG)
            m = jnp.max(s, axis=1, keepdims=True)
            e = jnp.where(keep, jnp.exp(s - m), 0.0)
            l = jnp.sum(e, axis=1, keepdims=True)
            p = e / jnp.where(l > 0.0, l, 1.0)
            o = jnp.dot(p.astype(jnp.bfloat16), vc_g, preferred_element_type=jnp.float32)
            if h % 2 != khalf:
                o = pltpu.roll(o, HEAD_DIM, 1)
            outs.append(o)
            psum = p if psum is None else psum + p
        imp = jnp.dot(psum, ov_ref[...], precision=lax.Precision.HIGHEST, preferred_element_type=jnp.float32)
        score = jnp.where(lane == own, OWN_SCORE, jnp.where(lane < own, imp, NEG))
        mask_ref[0, :, g * LANES:(g + 1) * LANES] = _top_lanes(score, SEL_TOPK)
    for c in range(NSA_HEADS // 2):
        o_ref[0, :, c * LANES:(c + 1) * LANES] = jnp.where(lane < HEAD_DIM, outs[2 * c], outs[2 * c + 1])


def _cmp_select(q, abk, abv, slopes, cw):
    b, t, _ = q.shape
    tq = ATTN_TILE
    kvw = NSA_KV_WIDTH
    c_start = np.arange(LANES) * CMP_STRIDE
    j_start = np.arange(LANES) * SEL_BLOCK
    ov = ((c_start[:, None] < j_start[None, :] + SEL_BLOCK) & (c_start[:, None] + CMP_LEN > j_start[None, :])
          & (np.arange(LANES)[:, None] < N_CMP_CHUNKS - 1)).astype(np.float32)
    bd = np.kron(np.eye(NSA_KV_HEADS, dtype=np.float32), np.full((HEAD_DIM, HEAD_DIM), 1.0 / HEAD_DIM, np.float32))
    consts = [cw['pebk'], cw['pebv'], cw['w2k'], cw['w2v'], cw['gk'], jnp.asarray(bd), jnp.asarray(ov)]
    return pl.pallas_call(
        _cmp_select_kernel,
        grid=(b, t // tq),
        in_specs=[pl.BlockSpec(memory_space=pltpu.SMEM),
                  pl.BlockSpec((1, tq, NSA_WIDTH), lambda bi, i: (bi, i, 0)),
                  pl.BlockSpec((1, N_CMP_CHUNKS, 2 * kvw), lambda bi, i: (bi, 0, 0)),
                  pl.BlockSpec((1, N_CMP_CHUNKS, 2 * kvw), lambda bi, i: (bi, 0, 0))]
                 + [pl.BlockSpec(c.shape, lambda bi, i: (0, 0)) for c in consts],
        out_specs=[pl.BlockSpec((1, tq, NSA_WIDTH), lambda bi, i: (bi, i, 0)),
                   pl.BlockSpec((1, tq, NSA_KV_HEADS * LANES), lambda bi, i: (bi, i, 0))],
        out_shape=[jax.ShapeDtypeStruct((b, t, NSA_WIDTH), jnp.float32),
                   jax.ShapeDtypeStruct((b, t, NSA_KV_HEADS * LANES), jnp.float32)],
        scratch_shapes=[pltpu.VMEM((N_CMP_CHUNKS, kvw), jnp.bfloat16), pltpu.VMEM((N_CMP_CHUNKS, kvw), jnp.bfloat16)],
        compiler_params=pltpu.CompilerParams(dimension_semantics=("arbitrary", "arbitrary")),
        name="cmp_select",
    )(slopes, q, abk, abv, *consts)


PEER_CHUNK = 512
F32_MIN = float(np.finfo(np.float32).min)


def _extract_top(work, k):
    r_total, t = work.shape
    rowid = lax.broadcasted_iota(jnp.int32, (r_total, t), 0)
    kid = lax.broadcasted_iota(jnp.int32, (k, t), 0)

    def body(r, carry):
        w, vals = carry
        mx = jnp.max(w, axis=0, keepdims=True)
        first = jnp.min(jnp.where(w == mx, rowid, r_total), axis=0, keepdims=True)
        w = jnp.where(rowid == first, F32_MIN, w)
        vals = jnp.where(kid == r, mx, vals)
        return w, vals

    w, vals = lax.fori_loop(0, k, body, (work, jnp.zeros((k, t), jnp.float32)))
    return vals, w


def _peer_route_kernel(x_ref, g_ref, wq_ref, subk_ref, xb_ref, s0_ref, e0_ref, s1_ref, e1_ref, thr_ref):
    x = x_ref[...]
    xn = x * lax.rsqrt(jnp.mean(x * x, axis=-1, keepdims=True) + RMS_EPS) * g_ref[...]
    xb = xn.astype(jnp.bfloat16)
    xb_ref[...] = xb
    q = jnp.dot(xb, wq_ref[...], preferred_element_type=jnp.float32)
    half = PEER_QDIM // 2
    for p in range(PEER_HEADS):
        tabs = []
        for c in range(2):
            qpc = q[:, (2 * p + c) * half:(2 * p + c + 1) * half]
            s = lax.dot_general(qpc, subk_ref[2 * p + c], (((1,), (1,)), ((), ())),
                                precision=lax.Precision.HIGHEST, preferred_element_type=jnp.float32)
            st = s.T
            vals, w = _extract_top(st, PEER_TOPK)
            tabs.append((st, vals, w == F32_MIN))
        (s0, av, m0), (s1, bv, m1) = tabs
        cand = jnp.concatenate([av[a:a + 1, :] + bv for a in range(PEER_TOPK)], axis=0)
        best, _ = _extract_top(cand, PEER_TOPK)
        zsum = jnp.sum(jnp.exp(best - best[0:1, :]), axis=0, keepdims=True)
        s0_ref[p] = jnp.where(m0, s0, NEG)
        e0_ref[p] = jnp.where(m0, jnp.exp(s0 - av[0:1, :]), 0.0)
        s1_ref[p] = jnp.where(m1, s1, NEG)
        e1_ref[p] = jnp.where(m1, jnp.exp(s1 - bv[0:1, :]), 0.0) / zsum
        thr_ref[p:p + 1, :] = best[PEER_TOPK - 1:PEER_TOPK, :]


def _peer_expert_kernel(x_ref, xb_ref, s0_ref, e0_ref, s1_ref, e1_ref, thr_ref, u_ref, v_ref, o_ref):
    c = pl.program_id(1)

    @pl.when(c == 0)
    def _():
        o_ref[...] = x_ref[...]

    a = lax.dot_general(xb_ref[...], u_ref[...], (((1,), (1,)), ((), ())), preferred_element_type=jnp.float32)
    ga = jax.nn.gelu(a)
    hs = []
    for k in range(PEER_CHUNK // PEER_KEYS):
        i1 = c * (PEER_CHUNK // PEER_KEYS) + k
        wt = None
        for p in range(PEER_HEADS):
            row_s = s0_ref[p, pl.ds(i1, 1), :]
            row_e = e0_ref[p, pl.ds(i1, 1), :]
            term = jnp.where(row_s + s1_ref[p] >= thr_ref[p:p + 1, :], row_e * e1_ref[p], 0.0)
            wt = term if wt is None else wt + term
        hs.append((wt.T * ga[:, k * PEER_KEYS:(k + 1) * PEER_KEYS]).astype(jnp.bfloat16))
    h = jnp.concatenate(hs, axis=1)
    o_ref[...] += jnp.dot(h, v_ref[...], preferred_element_type=jnp.float32)


def _peer_residual(x, g_ffn, wq_b, subk, u_b, v_b):
    n, d = x.shape
    t1 = 256 if n % 256 == 0 else n
    hp = PEER_HEADS
    tab = jax.ShapeDtypeStruct((hp, PEER_KEYS, n), jnp.float32)
    tab_spec = pl.BlockSpec((hp, PEER_KEYS, t1), lambda i: (0, 0, i))
    xb, s0, e0, s1, e1, thr = pl.pallas_call(
        _peer_route_kernel,
        grid=(n // t1,),
        in_specs=[pl.BlockSpec((t1, d), lambda i: (i, 0)),
                  pl.BlockSpec((1, d), lambda i: (0, 0)),
                  pl.BlockSpec((d, hp * PEER_QDIM), lambda i: (0, 0)),
                  pl.BlockSpec((2 * hp, PEER_KEYS, PEER_QDIM // 2), lambda i: (0, 0, 0))],
        out_specs=[pl.BlockSpec((t1, d), lambda i: (i, 0)), tab_spec, tab_spec, tab_spec, tab_spec,
                   pl.BlockSpec((hp, t1), lambda i: (0, i))],
        out_shape=[jax.ShapeDtypeStruct((n, d), jnp.bfloat16), tab, tab, tab, tab,
                   jax.ShapeDtypeStruct((hp, n), jnp.float32)],
        compiler_params=pltpu.CompilerParams(dimension_semantics=("arbitrary",),
                                             vmem_limit_bytes=56 * 1024 * 1024),
        name="peer_route",
    )(x, g_ffn.reshape(1, d), wq_b, subk.reshape(2 * hp, PEER_KEYS, PEER_QDIM // 2))
    t2 = 512 if n % 512 == 0 else n
    tab_spec2 = pl.BlockSpec((hp, PEER_KEYS, t2), lambda i, c: (0, 0, i))
    return pl.pallas_call(
        _peer_expert_kernel,
        grid=(n // t2, PEER_EXPERTS // PEER_CHUNK),
        in_specs=[pl.BlockSpec((t2, d), lambda i, c: (i, 0)),
                  pl.BlockSpec((t2, d), lambda i, c: (i, 0)),
                  tab_spec2, tab_spec2, tab_spec2, tab_spec2,
                  pl.BlockSpec((hp, t2), lambda i, c: (0, i)),
                  pl.BlockSpec((PEER_CHUNK, d), lambda i, c: (c, 0)),
                  pl.BlockSpec((PEER_CHUNK, d), lambda i, c: (c, 0))],
        out_specs=pl.BlockSpec((t2, d), lambda i, c: (i, 0)),
        out_shape=jax.ShapeDtypeStruct((n, d), jnp.float32),
        compiler_params=pltpu.CompilerParams(dimension_semantics=("arbitrary", "arbitrary"),
                                             vmem_limit_bytes=56 * 1024 * 1024),
        name="peer_experts",
    )(x, xb, s0, e0, s1, e1, thr, u_b, v_b)


N_PAGES = PAST_LEN // PAGE_SIZE
CTX_ROWS = PAST_LEN + PAGE_SIZE
N_CMP_CHUNKS = PAST_LEN // CMP_STRIDE
WIN_ROWS = WINDOW + PAGE_SIZE


def _cmp_chunk_kernel(x_ref, wk_ref, wv_ref, abk_ref, abv_ref):
    @pl.when(pl.program_id(1) == 0)
    def _():
        abk_ref[...] = jnp.zeros_like(abk_ref)
        abv_ref[...] = jnp.zeros_like(abv_ref)

    x = x_ref[...]
    abk_ref[...] += jnp.dot(x[:, :NSA_KV_WIDTH].astype(jnp.bfloat16), wk_ref[0], preferred_element_type=jnp.float32)
    abv_ref[...] += jnp.dot(x[:, NSA_KV_WIDTH:].astype(jnp.bfloat16), wv_ref[0], preferred_element_type=jnp.float32)


def _cmp_chunks(rows, wk_bd, wv_bd):
    n = rows.shape[0]
    tm = next(c for c in (1024, 512, 256, 128) if n % c == 0)
    width = 2 * NSA_KV_WIDTH
    out = jax.ShapeDtypeStruct((n, width), jnp.float32)
    return pl.pallas_call(
        _cmp_chunk_kernel,
        grid=(n // tm, CMP_STRIDE),
        in_specs=[pl.BlockSpec((tm, width), lambda i, r: (i, 2 * r)),
                  pl.BlockSpec((1, NSA_KV_WIDTH, width), lambda i, r: (r, 0, 0)),
                  pl.BlockSpec((1, NSA_KV_WIDTH, width), lambda i, r: (r, 0, 0))],
        out_specs=[pl.BlockSpec((tm, width), lambda i, r: (i, 0)), pl.BlockSpec((tm, width), lambda i, r: (i, 0))],
        out_shape=[out, out],
        compiler_params=pltpu.CompilerParams(dimension_semantics=("arbitrary", "arbitrary")),
        name="cmp_chunks",
    )(rows, wk_bd, wv_bd)


def _cmp_weights(w1, w2, pe):
    w1r = w1.reshape(2, CMP_STRIDE, HEAD_DIM, CMP_HIDDEN)
    eye = jnp.eye(NSA_KV_HEADS, dtype=w1.dtype)
    w1_bd = jnp.einsum('gG,ardh->rgdaGh', eye, w1r).reshape(CMP_STRIDE, NSA_KV_WIDTH, 2 * NSA_KV_WIDTH)
    w2_bd = jnp.kron(eye, w2)
    bias = jnp.dot(pe.reshape(1, CMP_LEN * HEAD_DIM), w1, precision=lax.Precision.HIGHEST)
    return w1_bd.astype(jnp.bfloat16), w2_bd.astype(jnp.bfloat16), jnp.tile(bias, (1, NSA_KV_HEADS))


def _softmax_rows(s, keep):
    s = jnp.where(keep, s, NEG)
    m = jnp.max(s, axis=0, keepdims=True)
    e = jnp.where(keep, jnp.exp(s - m), 0.0)
    return e / jnp.sum(e, axis=0, keepdims=True)


def _nt(a, b, **kw):
    return lax.dot_general(a, b, (((1,), (1,)), ((), ())), preferred_element_type=jnp.float32, **kw)


def _nsa_sample_kernel(pt_ref, qb_ref, page_ref, abk_ref, abv_ref, win_ref, new_ref, gates_ref,
                       slopes_ref, pebk_ref, pebv_ref, w2k_ref, w2v_ref, gk_ref, ovt_ref, grp_ref, bd_ref,
                       eslc_ref, hmask_ref, o_ref,
                       qpad_ref, s_all, v_all, abk_all, abv_all, kw_ext, vw_ext):
    p = pl.program_id(1)
    kvw = NSA_KV_WIDTH

    @pl.when(p == 0)
    def _():
        qpad_ref[...] = jnp.zeros_like(qpad_ref)
        qpad_ref[0:NSA_HEADS, :] = qb_ref[0].astype(jnp.bfloat16)

    row0 = pl.multiple_of(p * PAGE_SIZE, PAGE_SIZE)
    page = page_ref[0]
    s_all[pl.ds(row0, PAGE_SIZE), :] = _nt(page[:, :kvw].astype(jnp.bfloat16), qpad_ref[...])
    v_all[pl.ds(row0, PAGE_SIZE), :] = page[:, kvw:].astype(jnp.bfloat16)
    c0 = pl.multiple_of(p * (PAGE_SIZE // CMP_STRIDE), PAGE_SIZE // CMP_STRIDE)
    abk_all[pl.ds(c0, PAGE_SIZE // CMP_STRIDE), :] = abk_ref[0]
    abv_all[pl.ds(c0, PAGE_SIZE // CMP_STRIDE), :] = abv_ref[0]

    @pl.when(p == N_PAGES - 1)
    def _():
        qpad = qpad_ref[...]
        slopes = slopes_ref[...]
        new = new_ref[0]

        def tail(x):
            return jnp.broadcast_to(x, (PAGE_SIZE, kvw)).astype(jnp.bfloat16)

        def fold(full):
            m = full[0:NSA_HEADS, :] * hmask_ref[...]
            return m[0:4] + m[4:8] + m[8:12] + m[12:16]

        def second_layer(ab_ref, peb_ref, w2_ref):
            ab = ab_ref[...]
            h = jax.nn.gelu(ab[:, :kvw] + pltpu.roll(ab[:, kvw:], N_CMP_CHUNKS - 1, 0) + peb_ref[...])
            return jnp.dot(h.astype(jnp.bfloat16), w2_ref[...], preferred_element_type=jnp.float32)

        kc = second_layer(abk_all, pebk_ref, w2k_ref)
        ms = jnp.dot(kc * kc, bd_ref[...], precision=lax.Precision.HIGHEST, preferred_element_type=jnp.float32)
        kc = kc * lax.rsqrt(ms + RMS_EPS) * gk_ref[...]
        vc = second_layer(abv_all, pebv_ref, w2v_ref)
        cidx = lax.broadcasted_iota(jnp.int32, (N_CMP_CHUNKS, 1), 0)
        dist_c = (PAST_LEN - (CMP_LEN - 1) - CMP_STRIDE * cidx).astype(jnp.float32)
        p_cmp = _softmax_rows(_nt(kc.astype(jnp.bfloat16), qpad) - slopes * dist_c, cidx < N_CMP_CHUNKS - 1)
        o_cmp = jnp.dot(p_cmp.T.astype(jnp.bfloat16), vc.astype(jnp.bfloat16), preferred_element_type=jnp.float32)

        imp = jnp.dot(ovt_ref[...], p_cmp, precision=lax.Precision.HIGHEST, preferred_element_type=jnp.float32)
        imp = jnp.dot(imp, grp_ref[...], precision=lax.Precision.HIGHEST, preferred_element_type=jnp.float32)
        jrow = lax.broadcasted_iota(jnp.int32, (LANES, LANES), 0)
        _, taken = _extract_top(jnp.where(jrow < PAST_LEN // SEL_BLOCK, imp, NEG), SEL_TOPK - 1)
        sel = (taken == F32_MIN).astype(jnp.bfloat16)

        s_all[PAST_LEN:CTX_ROWS, :] = _nt(tail(new[:, 0:kvw]), qpad)
        v_all[PAST_LEN:CTX_ROWS, :] = tail(new[:, kvw:2 * kvw])
        krow = lax.broadcasted_iota(jnp.int32, (CTX_ROWS, 1), 0)
        picked = jnp.dot(eslc_ref[...], sel, preferred_element_type=jnp.float32) > 0.5
        keep = (picked & (krow < PAST_LEN)) | (krow == PAST_LEN)
        p_slc = _softmax_rows(s_all[...] - slopes * (PAST_LEN - krow).astype(jnp.float32), keep)
        o_slc = jnp.dot(p_slc.T.astype(jnp.bfloat16), v_all[...], preferred_element_type=jnp.float32)

        win = win_ref[0]
        kw_ext[0:WINDOW, :] = win[:, :kvw].astype(jnp.bfloat16)
        vw_ext[0:WINDOW, :] = win[:, kvw:].astype(jnp.bfloat16)
        kw_ext[WINDOW:WIN_ROWS, :] = tail(new[:, 2 * kvw:3 * kvw])
        vw_ext[WINDOW:WIN_ROWS, :] = tail(new[:, 3 * kvw:4 * kvw])
        wrow = lax.broadcasted_iota(jnp.int32, (WIN_ROWS, 1), 0)
        p_win = _softmax_rows(_nt(kw_ext[...], qpad) - slopes * (WINDOW - wrow).astype(jnp.float32), wrow <= WINDOW)
        o_win = jnp.dot(p_win.T.astype(jnp.bfloat16), vw_ext[...], preferred_element_type=jnp.float32)

        g = gates_ref[0]
        o_ref[0] = g[0] * fold(o_cmp) + g[1] * fold(o_slc) + g[2] * fold(o_win)


def _moba_sample_kernel(pt_ref, qb_ref, page_ref, new_ref, slopes_ref, emoba_ref, hmask_ref, o_ref,
                        qpad_ref, qf_ref, s_all, v_all, ksum_ref):
    p = pl.program_id(1)
    w = MOBA_WIDTH

    @pl.when(p == 0)
    def _():
        qf_ref[...] = jnp.zeros_like(qf_ref)
        qf_ref[0:MOBA_HEADS, :] = qb_ref[0]
        qpad_ref[...] = qf_ref[...].astype(jnp.bfloat16)
        ksum_ref[...] = jnp.zeros_like(ksum_ref)

    row0 = pl.multiple_of(p * PAGE_SIZE, PAGE_SIZE)
    page = page_ref[0]
    k = page[:, :w]
    s_all[pl.ds(row0, PAGE_SIZE), :] = _nt(k.astype(jnp.bfloat16), qpad_ref[...])
    v_all[pl.ds(row0, PAGE_SIZE), :] = page[:, w:].astype(jnp.bfloat16)
    blk = lax.broadcasted_iota(jnp.int32, ksum_ref.shape, 0)
    ksum_ref[...] += jnp.where(blk == p // (MOBA_BLOCK // PAGE_SIZE), jnp.sum(k, axis=0, keepdims=True), 0.0)

    @pl.when(p == N_PAGES - 1)
    def _():
        qpad = qpad_ref[...]
        slopes = slopes_ref[...]
        new = new_ref[0]
        gate = _nt(ksum_ref[...], qf_ref[...], precision=lax.Precision.HIGHEST)
        _, taken = _extract_top(gate, MOBA_TOPK)
        sel = jnp.concatenate([(taken == F32_MIN).astype(jnp.bfloat16),
                               jnp.zeros((LANES - gate.shape[0], LANES), jnp.bfloat16)], axis=0)
        s_all[PAST_LEN:CTX_ROWS, :] = _nt(jnp.broadcast_to(new[:, :w], (PAGE_SIZE, w)).astype(jnp.bfloat16), qpad)
        v_all[PAST_LEN:CTX_ROWS, :] = jnp.broadcast_to(new[:, w:], (PAGE_SIZE, w)).astype(jnp.bfloat16)
        krow = lax.broadcasted_iota(jnp.int32, (CTX_ROWS, 1), 0)
        picked = jnp.dot(emoba_ref[...], sel, preferred_element_type=jnp.float32) > 0.5
        keep = (picked & (krow < PAST_LEN)) | (krow == PAST_LEN)
        prob = _softmax_rows(s_all[...] - slopes * (PAST_LEN - krow).astype(jnp.float32), keep)
        full = jnp.dot(prob.T.astype(jnp.bfloat16), v_all[...], preferred_element_type=jnp.float32)
        o_ref[0] = jnp.sum(full[0:MOBA_HEADS, :] * hmask_ref[...], axis=0, keepdims=True)


def _head_rows(q, lane_head):
    n = int(max(lane_head)) + 1
    onehot = jnp.asarray(np.eye(n, dtype=np.float32)[np.asarray(lane_head)])
    return jnp.einsum('bhd,hn->bhnd', q, onehot).reshape(q.shape[0], q.shape[1], n * HEAD_DIM)


def _block_expand(block):
    e = np.zeros((CTX_ROWS, LANES), np.float32)
    e[np.arange(PAST_LEN), np.arange(PAST_LEN) // block] = 1.0
    return jnp.asarray(e, jnp.bfloat16)


def _const_spec(shape):
    nd = len(shape)
    return pl.BlockSpec(shape, lambda b, p, pt: (0,) * nd)


def _nsa_sample(page_table, q, cache, abk, abv, win, new, gates, slopes, cw):
    nseq = q.shape[0]
    kvw = NSA_KV_WIDTH
    group_of_head = [h // NSA_GROUP for h in range(NSA_HEADS)]
    qb = _head_rows(q * SCALE, group_of_head)
    hmask = _head_rows(jnp.ones((1, NSA_HEADS, HEAD_DIM), jnp.float32), group_of_head)[0]
    c_start = np.arange(LANES) * CMP_STRIDE
    j_start = np.arange(LANES) * SEL_BLOCK
    ovt = ((c_start[None, :] < j_start[:, None] + SEL_BLOCK) & (c_start[None, :] + CMP_LEN > j_start[:, None])
           & (np.arange(LANES)[None, :] < N_CMP_CHUNKS - 1)).astype(np.float32)
    hh = np.arange(LANES)
    grp = ((hh[:, None] // NSA_GROUP) == (hh[None, :] // NSA_GROUP)).astype(np.float32)
    bd = np.kron(np.eye(NSA_KV_HEADS, dtype=np.float32), np.full((HEAD_DIM, HEAD_DIM), 1.0 / HEAD_DIM, np.float32))
    slopes_row = jnp.zeros((1, LANES), jnp.float32).at[0, :NSA_HEADS].set(slopes)
    consts = [slopes_row, cw['pebk'], cw['pebv'], cw['w2k'], cw['w2v'], cw['gk'],
              jnp.asarray(ovt), jnp.asarray(grp), jnp.asarray(bd), _block_expand(SEL_BLOCK), hmask]
    grid_spec = pltpu.PrefetchScalarGridSpec(
        num_scalar_prefetch=1,
        grid=(nseq, N_PAGES),
        in_specs=[pl.BlockSpec((1, NSA_HEADS, kvw), lambda b, p, pt: (b, 0, 0)),
                  pl.BlockSpec((1, PAGE_SIZE, 2 * kvw), lambda b, p, pt: (pt[b, p], 0, 1)),
                  pl.BlockSpec((1, PAGE_SIZE // CMP_STRIDE, 2 * kvw), lambda b, p, pt: (pt[b, p], 0, 0)),
                  pl.BlockSpec((1, PAGE_SIZE // CMP_STRIDE, 2 * kvw), lambda b, p, pt: (pt[b, p], 0, 0)),
                  pl.BlockSpec((1, WINDOW, 2 * kvw), lambda b, p, pt: (b, 0, 0)),
                  pl.BlockSpec((1, 1, 4 * kvw), lambda b, p, pt: (b, 0, 0)),
                  pl.BlockSpec((1, 3, NSA_GROUP, kvw), lambda b, p, pt: (b, 0, 0, 0))]
                 + [_const_spec(c.shape) for c in consts],
        out_specs=pl.BlockSpec((1, NSA_GROUP, kvw), lambda b, p, pt: (b, 0, 0)),
        scratch_shapes=[pltpu.VMEM((LANES, kvw), jnp.bfloat16),
                        pltpu.VMEM((CTX_ROWS, LANES), jnp.float32),
                        pltpu.VMEM((CTX_ROWS, kvw), jnp.bfloat16),
                        pltpu.VMEM((N_CMP_CHUNKS, 2 * kvw), jnp.float32),
                        pltpu.VMEM((N_CMP_CHUNKS, 2 * kvw), jnp.float32),
                        pltpu.VMEM((WIN_ROWS, kvw), jnp.bfloat16),
                        pltpu.VMEM((WIN_ROWS, kvw), jnp.bfloat16)])
    return pl.pallas_call(
        _nsa_sample_kernel,
        grid_spec=grid_spec,
        out_shape=jax.ShapeDtypeStruct((nseq, NSA_GROUP, kvw), jnp.float32),
        compiler_params=pltpu.CompilerParams(dimension_semantics=("arbitrary", "arbitrary")),
        name="nsa_sample",
    )(page_table, qb, cache, abk, abv, win, new, gates, *consts)


def _moba_sample(page_table, q, cache, new, slopes):
    nseq = q.shape[0]
    w = MOBA_WIDTH
    own_head = list(range(MOBA_HEADS))
    qb = _head_rows(q * SCALE, own_head)
    hmask = _head_rows(jnp.ones((1, MOBA_HEADS, HEAD_DIM), jnp.float32), own_head)[0]
    slopes_row = jnp.zeros((1, LANES), jnp.float32).at[0, :MOBA_HEADS].set(slopes)
    consts = [slopes_row, _block_expand(MOBA_BLOCK), hmask]
    grid_spec = pltpu.PrefetchScalarGridSpec(
        num_scalar_prefetch=1,
        grid=(nseq, N_PAGES),
        in_specs=[pl.BlockSpec((1, MOBA_HEADS, w), lambda b, p, pt: (b, 0, 0)),
                  pl.BlockSpec((1, PAGE_SIZE, 2 * w), lambda b, p, pt: (pt[b, p], 0, 0)),
                  pl.BlockSpec((1, 1, 2 * w), lambda b, p, pt: (b, 0, 0))]
                 + [_const_spec(c.shape) for c in consts],
        out_specs=pl.BlockSpec((1, 1, w), lambda b, p, pt: (b, 0, 0)),
        scratch_shapes=[pltpu.VMEM((LANES, w), jnp.bfloat16),
                        pltpu.VMEM((LANES, w), jnp.float32),
                        pltpu.VMEM((CTX_ROWS, LANES), jnp.float32),
                        pltpu.VMEM((CTX_ROWS, w), jnp.bfloat16),
                        pltpu.VMEM((PAST_LEN // MOBA_BLOCK, w), jnp.float32)])
    return pl.pallas_call(
        _moba_sample_kernel,
        grid_spec=grid_spec,
        out_shape=jax.ShapeDtypeStruct((nseq, 1, w), jnp.float32),
        compiler_params=pltpu.CompilerParams(dimension_semantics=("arbitrary", "arbitrary")),
        name="moba_sample",
    )(page_table, qb, cache, new, *consts)


def rms_norm(x, g):
    xf = x.astype(jnp.float32)
    y = xf * lax.rsqrt(jnp.mean(xf * xf, axis=-1, keepdims=True) + RMS_EPS)
    return (y * g.astype(jnp.float32)).astype(x.dtype)


def alibi_slopes():
    n = NSA_HEADS + MOBA_HEADS
    s = jnp.exp2(-8.0 * jnp.arange(1, n + 1, dtype=jnp.float32) / n)
    return s[0::2], s[1::2]


def pad_rows(a, length):
    extra = length - a.shape[1]
    if extra == 0:
        return a
    return jnp.pad(a, [(0, 0), (0, extra)] + [(0, 0)] * (a.ndim - 2))


def masked_softmax(s, mask):
    s = jnp.where(mask, s, NEG)
    m = jnp.max(s, axis=-1, keepdims=True)
    e = jnp.where(mask, jnp.exp(s - m), 0.0)
    l = jnp.sum(e, axis=-1, keepdims=True)
    safe = jnp.where(l > 0, l, 1.0)
    lse = jnp.where(l > 0, m + jnp.log(safe), NEG)
    return e / safe, lse[..., 0]


def merge_partials(o1, lse1, o2, lse2):
    lse = jnp.logaddexp(lse1, lse2)
    return jnp.exp(lse1 - lse)[..., None] * o1 + jnp.exp(lse2 - lse)[..., None] * o2


def sweep_queries(fn, block, q_pos, *xs):
    t = q_pos.shape[0]
    if t <= block or t % block:
        return fn(q_pos, *xs)
    nb = t // block

    def split(a):
        return jnp.moveaxis(a.reshape((a.shape[0], nb, block) + a.shape[2:]), 1, 0)

    def join(a):
        a = jnp.moveaxis(a, 0, 1)
        return a.reshape((a.shape[0], t) + a.shape[3:])

    outs = lax.map(lambda args: fn(*args), (q_pos.reshape(nb, block),) + tuple(split(a) for a in xs))
    return jax.tree_util.tree_map(join, outs)


def gathered_block_attention(q, k_blocks, v_blocks, idx, valid, q_pos, slopes, block):
    b, t, g, r, dk = q.shape
    n = idx.shape[-1]
    bi = jnp.arange(b)[:, None, None, None]
    gi = jnp.arange(g)[None, None, :, None]
    kg = k_blocks[bi, idx, :, gi].reshape(b, t, g, n * block, dk)
    vg = v_blocks[bi, idx, :, gi].reshape(b, t, g, n * block, dk)
    k_pos = (idx[..., None] * block + jnp.arange(block)).reshape(b, t, g, n * block)
    dist = q_pos[None, :, None, None] - k_pos
    mask = (dist >= 0) & jnp.broadcast_to(valid[..., None], (b, t, g, n, block)).reshape(b, t, g, n * block)
    s = jnp.einsum('btgrd,btgmd->btgrm', q, kg, preferred_element_type=jnp.float32) * SCALE
    s = s - slopes[None, None, :, :, None] * dist.astype(jnp.float32)[:, :, :, None, :]
    p, lse = masked_softmax(s, mask[:, :, :, None, :])
    o = jnp.einsum('btgrm,btgmd->btgrd', p, vg, preferred_element_type=jnp.float32)
    return o, lse


def own_block_attention(q_blk, k_blk, v_blk, slopes):
    b, nb, mb, h, dk = q_blk.shape
    s = jnp.einsum('bnqhd,bnkhd->bnhqk', q_blk, k_blk, preferred_element_type=jnp.float32) * SCALE
    dist = jnp.arange(mb)[:, None] - jnp.arange(mb)[None, :]
    s = s - slopes[:, None, None] * dist.astype(jnp.float32)
    p, lse = masked_softmax(s, dist >= 0)
    o = jnp.einsum('bnhqk,bnkhd->bnqhd', p, v_blk, preferred_element_type=jnp.float32)
    return o.reshape(b, nb * mb, h, dk), jnp.moveaxis(lse, 2, 3).reshape(b, nb * mb, h)


def window_attention_banded(q, k, v, slopes):
    b, s_len, g, r, dk = q.shape
    nb = s_len // QBLOCK
    w = WINDOW // QBLOCK
    kp = jnp.concatenate([jnp.zeros((b, WINDOW, g, dk), k.dtype), k], axis=1).reshape(b, nb + w, QBLOCK, g, dk)
    vp = jnp.concatenate([jnp.zeros((b, WINDOW, g, dk), v.dtype), v], axis=1).reshape(b, nb + w, QBLOCK, g, dk)
    kband = jnp.concatenate([kp[:, i:i + nb] for i in range(w + 1)], axis=2)
    vband = jnp.concatenate([vp[:, i:i + nb] for i in range(w + 1)], axis=2)
    qb = q.reshape(b, nb, QBLOCK, g, r, dk)
    s = jnp.einsum('bnqgrd,bnkgd->bngrqk', qb, kband, preferred_element_type=jnp.float32) * SCALE
    q_pos = jnp.arange(nb)[:, None] * QBLOCK + jnp.arange(QBLOCK)[None, :]
    k_pos = jnp.arange(nb)[:, None] * QBLOCK - WINDOW + jnp.arange((w + 1) * QBLOCK)[None, :]
    dist = q_pos[:, :, None] - k_pos[:, None, :]
    mask = (dist >= 0) & (dist <= WINDOW) & (k_pos >= 0)[:, None, :]
    s = s - slopes[None, None, :, :, None, None] * dist.astype(jnp.float32)[None, :, None, None]
    p, _ = masked_softmax(s, mask[None, :, None, None])
    o = jnp.einsum('bngrqk,bnkgd->bnqgrd', p, vband, preferred_element_type=jnp.float32)
    return o.reshape(b, s_len, g, r, dk)


def window_attention_dense(q, k, v, q_pos, k_pos, slopes):
    s = jnp.einsum('btgrd,bkgd->btgrk', q, k, preferred_element_type=jnp.float32) * SCALE
    dist = q_pos[:, None] - k_pos[None, :]
    mask = (dist >= 0) & (dist <= WINDOW)
    s = s - slopes[None, None, :, :, None] * dist.astype(jnp.float32)[None, :, None, None, :]
    p, _ = masked_softmax(s, mask[None, :, None, None, :])
    return jnp.einsum('btgrk,bkgd->btgrd', p, v, preferred_element_type=jnp.float32)


def compress_blocks(kv, pos_emb, w1, w2):
    b, length, g, dk = kv.shape
    n_chunks = length // CMP_STRIDE
    n_sub = CMP_LEN // CMP_STRIDE
    n_cmp = n_chunks - n_sub + 1
    chunks = kv[:, :n_chunks * CMP_STRIDE].reshape(b, n_chunks, CMP_STRIDE, g, dk)
    blocks = jnp.concatenate([chunks[:, i:i + n_cmp] for i in range(n_sub)], axis=2)
    blocks = blocks + pos_emb[None, None, :, None, :]
    flat = jnp.moveaxis(blocks, 3, 2).reshape(b, n_cmp, g, CMP_LEN * dk)
    return jax.nn.gelu(flat @ w1) @ w2


MOBA_STEP_HEADS = ((0, 0, 0), (HEAD_DIM, 1, 1))
NSA_STEP_HEADS = tuple((g * NSA_GROUP * HEAD_DIM + r * HEAD_DIM, g, g) for g in range(2) for r in range(NSA_GROUP))


def _block_mask(idx, valid):
    jj = lax.broadcasted_iota(jnp.int32, idx.shape[:-1] + (1, LANES), idx.ndim)
    hit = (idx[..., None] == jj) & valid[..., None]
    return jnp.any(hit, axis=-2).astype(jnp.float32)


def nsa_mixer(q, gates, ctx, o_win, q_pos, slopes, lw, prompt):
    b, t = q.shape[:2]
    sl = slopes.reshape(NSA_KV_HEADS, NSA_GROUP)
    kc = rms_norm(compress_blocks(ctx[:, :, 0], lw['cmp_pe_k'], lw['cmp_w1_k'], lw['cmp_w2_k']), lw['g_k_cmp'])
    vc = compress_blocks(ctx[:, :, 1], lw['cmp_pe_v'], lw['cmp_w1_v'], lw['cmp_w2_v'])
    n_cmp = kc.shape[1]
    c_start = jnp.arange(n_cmp) * CMP_STRIDE
    dist = q_pos[:, None] - (c_start + CMP_LEN - 1)[None, :]
    s = jnp.einsum('btgrd,bcgd->btgrc', q, kc, preferred_element_type=jnp.float32) * SCALE
    s = s - sl[None, None, :, :, None] * dist.astype(jnp.float32)[None, :, None, None, :]
    p_cmp, _ = masked_softmax(s, (dist >= 0)[None, :, None, None, :])
    o_cmp = jnp.einsum('btgrc,bcgd->btgrd', p_cmp, vc, preferred_element_type=jnp.float32)
    length = ctx.shape[1]
    n_sel = -(-length // SEL_BLOCK)
    j_start = jnp.arange(n_sel) * SEL_BLOCK
    overlap = ((c_start[:, None] < j_start[None, :] + SEL_BLOCK)
               & (c_start[:, None] + CMP_LEN > j_start[None, :])).astype(jnp.float32)
    imp = jnp.einsum('btgrc,cj->btgj', p_cmp, overlap)
    jj = jnp.arange(n_sel)[None, :]
    own = (q_pos // SEL_BLOCK)[:, None]
    imp = jnp.where((jj == own)[None, :, None, :], OWN_SCORE,
                    jnp.where((jj < own)[None, :, None, :], imp, NEG))
    val, idx = lax.top_k(imp, min(SEL_TOPK, n_sel))
    valid = val > 0.5 * NEG
    if prompt:
        mask = _block_mask(idx, valid).reshape(b, t, NSA_KV_HEADS * LANES)
        o_slc = _block_attention(q.reshape(b, t, NSA_WIDTH), ctx.reshape(b, t, 4 * NSA_KV_WIDTH), mask, slopes,
                                 block=SEL_BLOCK, heads=NSA_STEP_HEADS, n_masks=2, k_col0=4, v_col0=6)
        o_slc = o_slc.reshape(q.shape)
    else:
        k_blk = pad_rows(ctx[:, :, 2], n_sel * SEL_BLOCK).reshape(b, n_sel, SEL_BLOCK, NSA_KV_HEADS, HEAD_DIM)
        v_blk = pad_rows(ctx[:, :, 3], n_sel * SEL_BLOCK).reshape(b, n_sel, SEL_BLOCK, NSA_KV_HEADS, HEAD_DIM)

        def sel_fn(qp, qb, ib, vb):
            return gathered_block_attention(qb, k_blk, v_blk, ib, vb, qp, sl, SEL_BLOCK)[0]

        o_slc = sweep_queries(sel_fn, QBLOCK, q_pos, q, idx, valid)
    o = gates[..., 0:1] * o_cmp + gates[..., 1:2] * o_slc + gates[..., 2:3] * o_win
    return o.reshape(b, t, NSA_WIDTH)


def moba_mixer(q, k_ctx, v_ctx, q_pos, slopes, prompt):
    b, t, h, dk = q.shape
    length = k_ctx.shape[1]
    nb = -(-length // MOBA_BLOCK)
    lp = nb * MOBA_BLOCK
    k_blk = pad_rows(k_ctx, lp).reshape(b, nb, MOBA_BLOCK, h, dk)
    v_blk = pad_rows(v_ctx, lp).reshape(b, nb, MOBA_BLOCK, h, dk)
    means = jnp.mean(k_blk.astype(jnp.float32), axis=2)
    gate = jnp.einsum('bthd,bnhd->bthn', q, means, preferred_element_type=jnp.float32)
    past = jnp.arange(nb)[None, :] < (q_pos // MOBA_BLOCK)[:, None]
    gate = jnp.where(past[None, :, None, :], gate, NEG)
    val, idx = lax.top_k(gate, min(MOBA_TOPK, nb))
    valid = val > 0.5 * NEG
    if prompt:
        own = (lax.broadcasted_iota(jnp.int32, (t, 1, LANES), 2) == (q_pos // MOBA_BLOCK)[:, None, None])
        mask = (_block_mask(idx, valid) + own.astype(jnp.float32)).reshape(b, t, h * LANES)
        kv = jnp.concatenate([k_ctx.reshape(b, t, h * dk), v_ctx.reshape(b, t, h * dk)], axis=-1)
        return _block_attention(q.reshape(b, t, h * dk), kv, mask, slopes, block=MOBA_BLOCK,
                                heads=MOBA_STEP_HEADS, n_masks=2, k_col0=0, v_col0=h * dk // LANES)
    q5 = q[:, :, :, None, :]
    sl = slopes[:, None]

    def sel_fn(qp, qb, ib, vb):
        return gathered_block_attention(qb, k_blk, v_blk, ib, vb, qp, sl, MOBA_BLOCK)

    o_sel, lse_sel = sweep_queries(sel_fn, MOBA_QBLOCK, q_pos, q5, idx, valid)
    if prompt:
        q_blk = pad_rows(q, lp).reshape(b, nb, MOBA_BLOCK, h, dk)
        o_own, lse_own = own_block_attention(q_blk, k_blk, v_blk, slopes)
        o_own = o_own[:, :t, :, None]
        lse_own = lse_own[:, :t, :, None]
    else:
        own = jnp.broadcast_to((q_pos // MOBA_BLOCK)[None, :, None, None], (b, t, h, 1))

        def own_fn(qp, qb, ib):
            return gathered_block_attention(qb, k_blk, v_blk, ib, jnp.ones(ib.shape, bool), qp, sl, MOBA_BLOCK)

        o_own, lse_own = sweep_queries(own_fn, MOBA_QBLOCK, q_pos, q5, own)
    o = merge_partials(o_sel, lse_sel, o_own, lse_own)
    return o.reshape(b, t, MOBA_WIDTH)


def peer_ffn(h, w_q, sub_keys, expert_u, expert_v):
    b, t, d = h.shape
    n = b * t
    n_blk = -(-n // PEER_TBLOCK)
    xt = h.reshape(n, d)
    if n_blk * PEER_TBLOCK != n:
        xt = jnp.pad(xt, ((0, n_blk * PEER_TBLOCK - n), (0, 0)))
    q = _mm(xt, w_q).reshape(-1, PEER_HEADS, 2, PEER_QDIM // 2)
    s = jnp.einsum('nphd,phkd->nphk', q, sub_keys, preferred_element_type=jnp.float32)
    top_s, top_i = lax.top_k(s, PEER_TOPK)
    cand_s = (top_s[:, :, 0, :, None] + top_s[:, :, 1, None, :]).reshape(-1, PEER_HEADS, PEER_TOPK * PEER_TOPK)
    cand_i = (top_i[:, :, 0, :, None] * PEER_KEYS + top_i[:, :, 1, None, :]).reshape(-1, PEER_HEADS, PEER_TOPK * PEER_TOPK)
    best_s, pos = lax.top_k(cand_s, PEER_TOPK)
    e_idx = jnp.take_along_axis(cand_i, pos, axis=-1)
    g = jax.nn.softmax(best_s, axis=-1)

    def block_fn(args):
        xb, eb, gb = args
        a = jax.nn.gelu(jnp.einsum('td,tpkd->tpk', xb, expert_u[eb], preferred_element_type=jnp.float32))
        return jnp.einsum('tpk,tpkd->td', (gb * a).astype(xb.dtype), expert_v[eb], preferred_element_type=jnp.float32)

    out = lax.map(block_fn, (xt.reshape(n_blk, PEER_TBLOCK, d),
                             e_idx.reshape(n_blk, PEER_TBLOCK, PEER_HEADS, PEER_TOPK),
                             g.reshape(n_blk, PEER_TBLOCK, PEER_HEADS, PEER_TOPK)))
    return out.reshape(-1, d)[:n].reshape(b, t, d).astype(h.dtype)


def layer_forward(x, ple, lw, slopes_nsa, slopes_moba, paged=None):
    b, t, _ = x.shape
    h = rms_norm(x, lw['g_mix'])
    sizes = [NSA_WIDTH] + [NSA_KV_WIDTH] * 6 + [N_GATE] + [MOBA_WIDTH] * 3
    offsets = np.cumsum(sizes)[:-1].tolist()
    q_n, kc, vc, ks, vs, kw, vw, gt, q_m, k_m, v_m = jnp.split(_mm(h, lw['w_in']), offsets, axis=-1)

    def heads(a, nh):
        return a.reshape(b, t, nh, HEAD_DIM)

    q_n = rms_norm(heads(q_n, NSA_HEADS), lw['g_q_nsa']).reshape(b, t, NSA_KV_HEADS, NSA_GROUP, HEAD_DIM)
    kc = heads(kc, NSA_KV_HEADS)
    vc = heads(vc, NSA_KV_HEADS)
    ks = rms_norm(heads(ks, NSA_KV_HEADS), lw['g_k_slc'])
    vs = heads(vs, NSA_KV_HEADS)
    kw = rms_norm(heads(kw, NSA_KV_HEADS), lw['g_k_win'])
    vw = heads(vw, NSA_KV_HEADS)
    gates = jax.nn.sigmoid(gt.astype(jnp.float32)).reshape(b, t, NSA_KV_HEADS, NSA_GROUP, 3)
    q_m = rms_norm(heads(q_m, MOBA_HEADS), lw['g_q_moba'])
    k_m = rms_norm(heads(k_m, MOBA_HEADS), lw['g_k_moba'])
    v_m = heads(v_m, MOBA_HEADS)
    nsa_rows = jnp.stack([kc, vc, ks, vs], axis=2)
    moba_rows = jnp.stack([k_m, v_m], axis=2)
    win_rows = jnp.stack([kw, vw], axis=2)
    if paged is None:
        new_win = win_rows[:, t - min(WINDOW, t):]
        kvw = NSA_KV_WIDTH
        q2 = q_n.reshape(b, t, NSA_WIDTH)
        rows2 = nsa_rows.reshape(b, t, 4 * kvw)
        abk, abv = _cmp_chunks(rows2.reshape(b * t // CMP_STRIDE, CMP_STRIDE * 4 * kvw),
                               lw['cmp_w1k_bd'], lw['cmp_w1v_bd'])
        o_cmp, mask = _cmp_select(q2, abk.reshape(b, t // CMP_STRIDE, 2 * kvw), abv.reshape(b, t // CMP_STRIDE, 2 * kvw),
                                  slopes_nsa, lw)
        o_slc = _block_attention(q2, rows2, slopes_nsa, mode='mask', heads=NSA_STEP_HEADS, k_col0=4, v_col0=6,
                                 mask=mask, block=SEL_BLOCK, n_masks=2)
        o_win = _block_attention(q2, win_rows.reshape(b, t, 2 * kvw), slopes_nsa, mode='window',
                                 heads=NSA_STEP_HEADS, k_col0=0, v_col0=2)
        g4 = gates.reshape(b, t, NSA_HEADS, 3)

        def per_head(a):
            return a.reshape(b, t, NSA_HEADS, HEAD_DIM)

        o_nsa = (g4[..., 0:1] * per_head(o_cmp) + g4[..., 1:2] * per_head(o_slc)
                 + g4[..., 2:3] * per_head(o_win)).reshape(b, t, NSA_WIDTH)
        o_moba = _block_attention(q_m.reshape(b, t, MOBA_WIDTH), moba_rows.reshape(b, t, 2 * MOBA_WIDTH), slopes_moba,
                                  mode='gate', heads=MOBA_STEP_HEADS, k_col0=0, v_col0=MOBA_WIDTH // LANES,
                                  block=MOBA_BLOCK)
    else:
        cache_n, cache_m, win_buf, page_table = paged
        n_phys = cache_n.shape[0]
        kvw = NSA_KV_WIDTH
        new_win = jnp.concatenate([win_buf, win_rows], axis=1)[:, t:]
        chunks_per_page = PAGE_SIZE // CMP_STRIDE
        abk, abv = _cmp_chunks(cache_n.reshape(n_phys * chunks_per_page, CMP_STRIDE * 4 * kvw),
                               lw['cmp_w1k_bd'], lw['cmp_w1v_bd'])
        new_n = jnp.concatenate([ks.reshape(b, 1, kvw), vs.reshape(b, 1, kvw),
                                 kw.reshape(b, 1, kvw), vw.reshape(b, 1, kvw)], axis=-1)
        g3 = jnp.transpose(gates[:, 0], (0, 3, 2, 1))[..., None]
        g3 = jnp.broadcast_to(g3, (b, 3, NSA_GROUP, NSA_KV_HEADS, HEAD_DIM)).reshape(b, 3, NSA_GROUP, kvw)
        o_nsa = _nsa_sample(page_table, q_n.reshape(b, NSA_HEADS, HEAD_DIM),
                            cache_n.reshape(n_phys, PAGE_SIZE, 4 * kvw),
                            abk.reshape(n_phys, chunks_per_page, 2 * kvw), abv.reshape(n_phys, chunks_per_page, 2 * kvw),
                            win_buf.reshape(b, WINDOW, 2 * kvw), new_n, g3, slopes_nsa, lw)
        o_nsa = jnp.transpose(o_nsa.reshape(b, NSA_GROUP, NSA_KV_HEADS, HEAD_DIM), (0, 2, 1, 3)).reshape(b, 1, NSA_WIDTH)
        new_m = jnp.concatenate([k_m.reshape(b, 1, MOBA_WIDTH), v_m.reshape(b, 1, MOBA_WIDTH)], axis=-1)
        o_moba = _moba_sample(page_table, q_m.reshape(b, MOBA_HEADS, HEAD_DIM),
                              cache_m.reshape(n_phys, PAGE_SIZE, 2 * MOBA_WIDTH), new_m, slopes_moba)
    mix = jnp.concatenate([rms_norm(o_nsa, lw['g_out_nsa']), rms_norm(o_moba, lw['g_out_moba'])], axis=-1)
    x = x + _mm(mix.astype(x.dtype), lw['w_out'])
    x = _peer_residual(x.reshape(b * t, D_MODEL), lw['g_ffn'], lw['w_peer_q_b'], lw['peer_sub_keys'],
                       lw['peer_u_b'], lw['peer_v_b']).reshape(b, t, D_MODEL)
    gate = jax.nn.sigmoid(_mm(rms_norm(x, lw['g_ple']), lw['w_ple_gate']))
    x = x + gate * rms_norm(_mm(ple, lw['w_ple_proj']), lw['g_ple_post'])
    return x, nsa_rows, moba_rows, new_win


def kernel(x_prompt, x_sample, cache_nsa, cache_moba, state_win, page_table, p_prompt, p_sample,
           g_mix, w_in, g_q_nsa, g_k_cmp, g_k_slc, g_k_win, g_q_moba, g_k_moba,
           cmp_pe_k, cmp_w1_k, cmp_w2_k, cmp_pe_v, cmp_w1_v, cmp_w2_v,
           g_out_nsa, g_out_moba, w_out, g_ffn, w_peer_q, peer_sub_keys, peer_u, peer_v,
           g_ple, w_ple_gate, w_ple_proj, g_ple_post):
    slopes_nsa, slopes_moba = alibi_slopes()
    n_seq, n_pages = page_table.shape
    i = 0
    lw = {
        'g_mix': g_mix[i], 'w_in': w_in[i], 'g_q_nsa': g_q_nsa[i], 'g_k_cmp': g_k_cmp[i],
        'g_k_slc': g_k_slc[i], 'g_k_win': g_k_win[i], 'g_q_moba': g_q_moba[i], 'g_k_moba': g_k_moba[i],
        'cmp_pe_k': cmp_pe_k[i], 'cmp_w1_k': cmp_w1_k[i], 'cmp_w2_k': cmp_w2_k[i],
        'cmp_pe_v': cmp_pe_v[i], 'cmp_w1_v': cmp_w1_v[i], 'cmp_w2_v': cmp_w2_v[i],
        'g_out_nsa': g_out_nsa[i], 'g_out_moba': g_out_moba[i], 'w_out': w_out[i], 'g_ffn': g_ffn[i],
        'w_peer_q': w_peer_q[i], 'peer_sub_keys': peer_sub_keys[i], 'peer_u': peer_u[i], 'peer_v': peer_v[i],
        'g_ple': g_ple[i], 'w_ple_gate': w_ple_gate[i], 'w_ple_proj': w_ple_proj[i], 'g_ple_post': g_ple_post[i],
        'w_peer_q_b': w_peer_q[i].astype(jnp.bfloat16), 'peer_u_b': peer_u[i].astype(jnp.bfloat16),
        'peer_v_b': peer_v[i].astype(jnp.bfloat16),
    }
    lw['cmp_w1k_bd'], lw['w2k'], lw['pebk'] = _cmp_weights(cmp_w1_k[i], cmp_w2_k[i], cmp_pe_k[i])
    lw['cmp_w1v_bd'], lw['w2v'], lw['pebv'] = _cmp_weights(cmp_w1_v[i], cmp_w2_v[i], cmp_pe_v[i])
    lw['gk'] = jnp.tile(g_k_cmp[i].reshape(1, HEAD_DIM), (1, NSA_KV_HEADS))
    y_prompt, nsa_p, moba_p, win_p = layer_forward(x_prompt, p_prompt[i], lw, slopes_nsa, slopes_moba)
    y_sample, nsa_s, moba_s, win_s = layer_forward(x_sample, p_sample[i], lw, slopes_nsa, slopes_moba,
                                                   (cache_nsa[i], cache_moba[i], state_win[i], page_table))
    return (y_prompt, y_sample, nsa_p[None], moba_p[None], win_p[None], nsa_s[None], moba_s[None], win_s[None])
```

```python
import functools

import jax
import jax.numpy as jnp
from jax import lax
import numpy as np
from jax.experimental import pallas as pl
from jax.experimental.pallas import tpu as pltpu

D_MODEL = 2048
PAST_LEN = 2048
PAGE_SIZE = 128
HEAD_DIM = 64
NSA_HEADS = D_MODEL // (2 * HEAD_DIM)
NSA_KV_HEADS = max(1, NSA_HEADS // 4)
NSA_GROUP = NSA_HEADS // NSA_KV_HEADS
MOBA_HEADS = D_MODEL // (2 * HEAD_DIM)
NSA_WIDTH = NSA_HEADS * HEAD_DIM
NSA_KV_WIDTH = NSA_KV_HEADS * HEAD_DIM
MOBA_WIDTH = MOBA_HEADS * HEAD_DIM
N_GATE = 3 * NSA_HEADS
CMP_LEN = 32
CMP_STRIDE = 16
CMP_HIDDEN = 64
SEL_BLOCK = 64
SEL_TOPK = 8
WINDOW = 512
MOBA_BLOCK = 256
MOBA_TOPK = 3
PEER_KEYS = 128
PEER_EXPERTS = PEER_KEYS * PEER_KEYS
PEER_HEADS = 8
PEER_TOPK = 16
PEER_QDIM = 256
PLE_DIM = 256
RMS_EPS = 1e-6
NEG = -1e30
OWN_SCORE = 1e9
SCALE = HEAD_DIM ** -0.5

LANES = 128
ATTN_TILE = 256
ROW_TILE = 512
PEER_CHUNK = 512
VMEM_LIMIT = 56 * 1024 * 1024
N_PAGES = PAST_LEN // PAGE_SIZE
N_CMP_CHUNKS = PAST_LEN // CMP_STRIDE
CHUNKS_PER_PAGE = PAGE_SIZE // CMP_STRIDE
F32_MIN = float(np.finfo(np.float32).min)

MOBA_STEP_HEADS = ((0, 0, 0), (HEAD_DIM, 1, 1))
NSA_STEP_HEADS = tuple((g * NSA_GROUP * HEAD_DIM + r * HEAD_DIM, g, g) for g in range(2) for r in range(NSA_GROUP))


def _nt(a, b, **kw):
    return lax.dot_general(a, b, (((1,), (1,)), ((), ())), preferred_element_type=jnp.float32, **kw)


def _top_lanes(score, k):
    lane = lax.broadcasted_iota(jnp.int32, score.shape, 1)
    sel = jnp.zeros(score.shape, jnp.float32)
    for _ in range(k):
        m = jnp.max(score, axis=1, keepdims=True)
        first = jnp.min(jnp.where(score == m, lane, LANES), axis=1, keepdims=True)
        hit = lane == first
        sel = jnp.where(hit & (m > 0.5 * NEG), 1.0, sel)
        score = jnp.where(hit, F32_MIN, score)
    return sel


def _extract_top(work, k):
    r_total, t = work.shape
    rowid = lax.broadcasted_iota(jnp.int32, (r_total, t), 0)
    kid = lax.broadcasted_iota(jnp.int32, (k, t), 0)

    def body(r, carry):
        w, vals = carry
        mx = jnp.max(w, axis=0, keepdims=True)
        first = jnp.min(jnp.where(w == mx, rowid, r_total), axis=0, keepdims=True)
        w = jnp.where(rowid == first, F32_MIN, w)
        vals = jnp.where(kid == r, mx, vals)
        return w, vals

    w, vals = lax.fori_loop(0, k, body, (work, jnp.zeros((k, t), jnp.float32)))
    return vals, w


def _norm_t_kernel(x_ref, g_ref, o_ref):
    x = x_ref[...]
    xn = x * lax.rsqrt(jnp.mean(x * x, axis=-1, keepdims=True) + RMS_EPS) * g_ref[...]
    o_ref[...] = xn.T.astype(jnp.bfloat16)


def _norm_transpose(x, g):
    n, d = x.shape
    tm = min(ROW_TILE, n)
    return pl.pallas_call(
        _norm_t_kernel,
        grid=(n // tm,),
        in_specs=[pl.BlockSpec((tm, d), lambda i: (i, 0)), pl.BlockSpec((1, d), lambda i: (0, 0))],
        out_specs=pl.BlockSpec((d, tm), lambda i: (0, i)),
        out_shape=jax.ShapeDtypeStruct((d, n), jnp.bfloat16),
        compiler_params=pltpu.CompilerParams(dimension_semantics=("arbitrary",)),
        name="norm_transpose",
    )(x, g.reshape(1, d))


def _proj_t_kernel(act, natural, w_ref, ht_ref, gain_ref, flag_ref, o_ref):
    y = jnp.dot(w_ref[...], ht_ref[...], preferred_element_type=jnp.float32)
    r, tm = y.shape
    y3 = y.reshape(r // HEAD_DIM, HEAD_DIM, tm)
    ms = jnp.mean(y3 * y3, axis=1, keepdims=True)
    yn = (y3 * lax.rsqrt(ms + RMS_EPS)).reshape(r, tm) * gain_ref[...]
    y = jnp.where(flag_ref[...] > 0.5, yn, y)
    if act == 'sigmoid':
        y = jax.nn.sigmoid(y)
    if natural:
        o_ref[...] = y.T
    else:
        o_ref[0] = y


def _proj_t(ht, w_t, gain, flag, *, batch, rows, act=None, natural=False):
    c, d = w_t.shape
    n = ht.shape[1]
    t = n // batch
    tm = min(ROW_TILE, t)
    nt = t // tm
    if natural:
        out_shape = jax.ShapeDtypeStruct((n, c), jnp.float32)
        out_spec = pl.BlockSpec((tm, rows), lambda b, i, j: (b * nt + i, j))
    else:
        out_shape = jax.ShapeDtypeStruct((batch, c, t), jnp.float32)
        out_spec = pl.BlockSpec((1, rows, tm), lambda b, i, j: (b, j, i))
    return pl.pallas_call(
        functools.partial(_proj_t_kernel, act, natural),
        grid=(batch, nt, c // rows),
        in_specs=[pl.BlockSpec((rows, d), lambda b, i, j: (j, 0)),
                  pl.BlockSpec((d, tm), lambda b, i, j: (0, b * nt + i)),
                  pl.BlockSpec((rows, 1), lambda b, i, j: (j, 0)),
                  pl.BlockSpec((rows, 1), lambda b, i, j: (j, 0))],
        out_specs=out_spec,
        out_shape=out_shape,
        compiler_params=pltpu.CompilerParams(dimension_semantics=("arbitrary",) * 3),
        name="proj_t",
    )(w_t, ht, gain, flag)


def _out_proj_kernel(on_ref, om_ref, x_ref, gn_ref, gm_ref, w_ref, o_ref, mix_ref):
    @pl.when(pl.program_id(1) == 0)
    def _():
        for src, g_ref, lo in ((on_ref, gn_ref, 0), (om_ref, gm_ref, NSA_WIDTH)):
            a = src[...]
            an = a * lax.rsqrt(jnp.mean(a * a, axis=-1, keepdims=True) + RMS_EPS) * g_ref[...]
            mix_ref[:, lo:lo + a.shape[1]] = an.astype(jnp.bfloat16)

    o_ref[...] = x_ref[...] + jnp.dot(mix_ref[...], w_ref[...], preferred_element_type=jnp.float32)


def _out_proj(o_nsa, o_moba, x, g_nsa, g_moba, w_b):
    n, d = x.shape
    tm = min(ROW_TILE, n)
    tn = 512
    return pl.pallas_call(
        _out_proj_kernel,
        grid=(n // tm, d // tn),
        in_specs=[pl.BlockSpec((tm, NSA_WIDTH), lambda i, j: (i, 0)),
                  pl.BlockSpec((tm, MOBA_WIDTH), lambda i, j: (i, 0)),
                  pl.BlockSpec((tm, tn), lambda i, j: (i, j)),
                  pl.BlockSpec((1, NSA_WIDTH), lambda i, j: (0, 0)),
                  pl.BlockSpec((1, MOBA_WIDTH), lambda i, j: (0, 0)),
                  pl.BlockSpec((NSA_WIDTH + MOBA_WIDTH, tn), lambda i, j: (0, j))],
        out_specs=pl.BlockSpec((tm, tn), lambda i, j: (i, j)),
        out_shape=jax.ShapeDtypeStruct((n, d), jnp.float32),
        scratch_shapes=[pltpu.VMEM((tm, NSA_WIDTH + MOBA_WIDTH), jnp.bfloat16)],
        compiler_params=pltpu.CompilerParams(dimension_semantics=("arbitrary", "arbitrary")),
        name="out_proj",
    )(o_nsa, o_moba, x, g_nsa.reshape(1, -1), g_moba.reshape(1, -1), w_b)


def _ple_kernel(tn, x_ref, p_ref, g_ref, wg_ref, wp_ref, gp_ref, o_ref, xn_ref, pn_ref):
    j = pl.program_id(1)

    @pl.when(j == 0)
    def _():
        x = x_ref[...]
        xn_ref[...] = (x * lax.rsqrt(jnp.mean(x * x, axis=-1, keepdims=True) + RMS_EPS) * g_ref[...]
                       ).astype(jnp.bfloat16)
        pr = jnp.dot(p_ref[...].astype(jnp.bfloat16), wp_ref[...], preferred_element_type=jnp.float32)
        pn_ref[...] = pr * lax.rsqrt(jnp.mean(pr * pr, axis=-1, keepdims=True) + RMS_EPS) * gp_ref[...]

    col = pl.multiple_of(j * tn, tn)
    gate = jax.nn.sigmoid(jnp.dot(xn_ref[...], wg_ref[...], preferred_element_type=jnp.float32))
    o_ref[...] = x_ref[:, pl.ds(col, tn)] + gate * pn_ref[:, pl.ds(col, tn)]


def _ple(x, ple, g_ple, wg_b, wp_b, g_post):
    n, d = x.shape
    tm = min(ROW_TILE, n)
    tn = 512
    return pl.pallas_call(
        functools.partial(_ple_kernel, tn),
        grid=(n // tm, d // tn),
        in_specs=[pl.BlockSpec((tm, d), lambda i, j: (i, 0)),
                  pl.BlockSpec((tm, PLE_DIM), lambda i, j: (i, 0)),
                  pl.BlockSpec((1, d), lambda i, j: (0, 0)),
                  pl.BlockSpec((d, tn), lambda i, j: (0, j)),
                  pl.BlockSpec((PLE_DIM, d), lambda i, j: (0, 0)),
                  pl.BlockSpec((1, d), lambda i, j: (0, 0))],
        out_specs=pl.BlockSpec((tm, tn), lambda i, j: (i, j)),
        out_shape=jax.ShapeDtypeStruct((n, d), jnp.float32),
        scratch_shapes=[pltpu.VMEM((tm, d), jnp.bfloat16), pltpu.VMEM((tm, d), jnp.float32)],
        compiler_params=pltpu.CompilerParams(dimension_semantics=("arbitrary", "arbitrary")),
        name="ple_gate",
    )(x, ple, g_ple.reshape(1, d), wg_b, wp_b, g_post.reshape(1, d))


def _attn_kernel(heads, mode, *refs):
    if mode == 'mask':
        slopes_ref, q_ref, k_ref, v_ref, mask_ref, expand_ref, o_ref, kb_ref, vb_ref, maskx_ref = refs
    elif mode == 'gate':
        slopes_ref, q_ref, k_ref, v_ref, expand_ref, o_ref, kb_ref, vb_ref, maskx_ref, means_ref = refs
    else:
        slopes_ref, q_ref, k_ref, v_ref, ocmp_ref, oslc_ref, gates_ref, o_ref, kb_ref, vb_ref = refs
    s_idx = pl.program_id(1)
    i = pl.program_id(2)
    tq = ATTN_TILE
    n_heads = len(heads)
    t_total = k_ref.shape[2]
    lane = lax.broadcasted_iota(jnp.int32, (tq, LANES), 1)

    @pl.when(i == 0)
    def _():
        kb_ref[...] = k_ref[0].astype(jnp.bfloat16)
        vb_ref[...] = v_ref[0].astype(jnp.bfloat16)
        if mode == 'gate':
            sq = lax.broadcasted_iota(jnp.int32, (LANES, LANES), 1)
            means = jnp.zeros((LANES, LANES), jnp.float32)
            for n in range(t_total // MOBA_BLOCK):
                col = jnp.mean(k_ref[0, :, n * MOBA_BLOCK:(n + 1) * MOBA_BLOCK], axis=1, keepdims=True)
                means = jnp.where(sq == n, col, means)
            means_ref[...] = means

    if mode == 'mask':
        for m in range(mask_ref.shape[2] // LANES):
            mb = mask_ref[0, :, m * LANES:(m + 1) * LANES].astype(jnp.bfloat16)
            maskx_ref[m] = jnp.dot(mb, expand_ref[...], preferred_element_type=jnp.float32)

    row = lax.broadcasted_iota(jnp.int32, (tq, tq), 0)
    col = lax.broadcasted_iota(jnp.int32, (tq, tq), 1)
    rc = (row - col).astype(jnp.float32)
    diag0 = pl.multiple_of(i * tq, tq)

    outs = []
    for h, (q_off, kv_half, m_idx) in enumerate(heads):
        slope = slopes_ref[s_idx * n_heads + h]
        chunk = q_off // LANES
        qc = q_ref[0, :, chunk * LANES:(chunk + 1) * LANES]
        if (q_off % LANES) // HEAD_DIM != kv_half:
            qc = pltpu.roll(qc, HEAD_DIM, 1)
        in_half = (lane >= HEAD_DIM) if kv_half else (lane < HEAD_DIM)
        qh = jnp.where(in_half, qc, 0.0)
        qb = (qh * SCALE).astype(jnp.bfloat16)
        bias = slope * rc
        if mode == 'gate':
            gate = jnp.dot(qh, means_ref[...], precision=lax.Precision.HIGHEST,
                           preferred_element_type=jnp.float32)
            sel = _top_lanes(jnp.where(lane < i, gate, NEG), MOBA_TOPK)
            sel = jnp.where(lane == i, 1.0, sel).astype(jnp.bfloat16)
            maskx_ref[m_idx] = jnp.dot(sel, expand_ref[...], preferred_element_type=jnp.float32)

        def scores(start):
            return jnp.dot(qb, kb_ref[:, pl.ds(start, tq)], preferred_element_type=jnp.float32)

        s = scores(diag0) - bias
        keep = rc >= 0.0
        if mode != 'window':
            keep = keep & (maskx_ref[m_idx, :, pl.ds(diag0, tq)] > 0.5)
        s = jnp.where(keep, s, NEG)
        m0 = jnp.max(s, axis=1, keepdims=True)
        p = jnp.exp(s - m0)
        l0 = jnp.sum(p, axis=1, keepdims=True)
        acc0 = _nt(p.astype(jnp.bfloat16), vb_ref[:, pl.ds(diag0, tq)])

        def body(n, carry):
            m_prev, l_prev, acc = carry
            start = pl.multiple_of(n * tq, tq)
            shift = slope * ((i - n) * tq).astype(jnp.float32)
            s = scores(start) - bias - shift
            if mode == 'window':
                limit = jnp.where(n == i - WINDOW // tq, 0.0, float(tq))
                s = jnp.where(rc > limit, NEG, s)
            else:
                s = jnp.where(maskx_ref[m_idx, :, pl.ds(start, tq)] > 0.5, s, NEG)
            m_new = jnp.maximum(m_prev, jnp.max(s, axis=1, keepdims=True))
            alpha = jnp.exp(m_prev - m_new)
            p = jnp.exp(s - m_new)
            l_new = alpha * l_prev + jnp.sum(p, axis=1, keepdims=True)
            acc = alpha * acc + _nt(p.astype(jnp.bfloat16), vb_ref[:, pl.ds(start, tq)])
            return m_new, l_new, acc

        first_tile = jnp.maximum(i - WINDOW // tq, 0) if mode == 'window' else 0
        _, l_fin, acc = lax.fori_loop(first_tile, i, body, (m0, l0, acc0))
        o = acc / l_fin
        if (q_off % LANES) // HEAD_DIM != kv_half:
            o = pltpu.roll(o, HEAD_DIM, 1)
        outs.append(o)

    for c in range(n_heads // 2):
        o = jnp.where(lane < HEAD_DIM, outs[2 * c], outs[2 * c + 1])
        if mode == 'window':
            g = gates_ref[0]

            def gate_of(branch):
                lo = g[:, 3 * (2 * c) + branch:3 * (2 * c) + branch + 1]
                hi = g[:, 3 * (2 * c + 1) + branch:3 * (2 * c + 1) + branch + 1]
                return jnp.where(lane < HEAD_DIM, lo, hi)

            sl = slice(c * LANES, (c + 1) * LANES)
            o = gate_of(0) * ocmp_ref[0, :, sl] + gate_of(1) * oslc_ref[0, :, sl] + gate_of(2) * o
        o_ref[0, :, c * LANES:(c + 1) * LANES] = o


def _attention(q, kv_t, slopes, *, mode, heads, k_row0, v_row0, mask=None, block=None, n_masks=0, combine=None):
    b, t, qcols = q.shape
    qw = (len(heads) // 2) * LANES
    n_steps = qcols // qw
    tq = ATTN_TILE
    assert mode != 'gate' or MOBA_BLOCK == tq
    q_spec = pl.BlockSpec((1, tq, qw), lambda bi, s, i: (bi, i, s))
    in_specs = [pl.BlockSpec(memory_space=pltpu.SMEM), q_spec,
                pl.BlockSpec((1, LANES, t), lambda bi, s, i: (bi, k_row0 + s, 0)),
                pl.BlockSpec((1, LANES, t), lambda bi, s, i: (bi, v_row0 + s, 0))]
    args = [slopes, q, kv_t, kv_t]
    scratch = [pltpu.VMEM((LANES, t), jnp.bfloat16), pltpu.VMEM((LANES, t), jnp.bfloat16)]
    if mode == 'mask':
        in_specs.append(pl.BlockSpec((1, tq, n_masks * LANES), lambda bi, s, i: (bi, i, s)))
        args.append(mask)
    if mode == 'window':
        in_specs += [q_spec, q_spec, pl.BlockSpec((1, tq, LANES), lambda bi, s, i: (bi, i, s))]
        args += list(combine)
    else:
        expand = (np.arange(LANES)[:, None] == (np.arange(t)[None, :] // block)).astype(np.float32)
        in_specs.append(pl.BlockSpec((LANES, t), lambda bi, s, i: (0, 0)))
        args.append(jnp.asarray(expand, jnp.bfloat16))
        scratch.append(pltpu.VMEM((n_masks if mode == 'mask' else len(heads), tq, t), jnp.float32))
    if mode == 'gate':
        scratch.append(pltpu.VMEM((LANES, LANES), jnp.float32))
    return pl.pallas_call(
        functools.partial(_attn_kernel, heads, mode),
        grid=(b, n_steps, t // tq),
        in_specs=in_specs,
        out_specs=q_spec,
        out_shape=jax.ShapeDtypeStruct((b, t, qcols), jnp.float32),
        scratch_shapes=scratch,
        compiler_params=pltpu.CompilerParams(dimension_semantics=("arbitrary",) * 3),
        name="attention_" + mode,
    )(*args)


def _cmp_chunk_kernel(x_ref, wk_ref, wv_ref, abk_ref, abv_ref):
    @pl.when(pl.program_id(1) == 0)
    def _():
        abk_ref[...] = jnp.zeros_like(abk_ref)
        abv_ref[...] = jnp.zeros_like(abv_ref)

    x = x_ref[...]
    abk_ref[...] += jnp.dot(x[:, :NSA_KV_WIDTH].astype(jnp.bfloat16), wk_ref[0], preferred_element_type=jnp.float32)
    abv_ref[...] += jnp.dot(x[:, NSA_KV_WIDTH:].astype(jnp.bfloat16), wv_ref[0], preferred_element_type=jnp.float32)


def _cmp_chunks(rows, wk_bd, wv_bd):
    n = rows.shape[0]
    tm = next(c for c in (1024, 512, 256, 128) if n % c == 0)
    width = 2 * NSA_KV_WIDTH
    out = jax.ShapeDtypeStruct((n, width), jnp.float32)
    return pl.pallas_call(
        _cmp_chunk_kernel,
        grid=(n // tm, CMP_STRIDE),
        in_specs=[pl.BlockSpec((tm, width), lambda i, r: (i, r)),
                  pl.BlockSpec((1, NSA_KV_WIDTH, width), lambda i, r: (r, 0, 0)),
                  pl.BlockSpec((1, NSA_KV_WIDTH, width), lambda i, r: (r, 0, 0))],
        out_specs=[pl.BlockSpec((tm, width), lambda i, r: (i, 0)), pl.BlockSpec((tm, width), lambda i, r: (i, 0))],
        out_shape=[out, out],
        compiler_params=pltpu.CompilerParams(dimension_semantics=("arbitrary", "arbitrary")),
        name="cmp_chunks",
    )(rows, wk_bd, wv_bd)


def _cmp_weights(w1, w2, pe):
    w1r = w1.reshape(2, CMP_STRIDE, HEAD_DIM, CMP_HIDDEN)
    eye = jnp.eye(NSA_KV_HEADS, dtype=w1.dtype)
    w1_bd = jnp.einsum('gG,ardh->rgdaGh', eye, w1r).reshape(CMP_STRIDE, NSA_KV_WIDTH, 2 * NSA_KV_WIDTH)
    w2_bd = jnp.kron(eye, w2)
    bias = jnp.dot(pe.reshape(1, CMP_LEN * HEAD_DIM), w1, precision=lax.Precision.HIGHEST)
    return w1_bd.astype(jnp.bfloat16), w2_bd.astype(jnp.bfloat16), jnp.tile(bias, (1, NSA_KV_HEADS))


def _cmp_second_layer(ab, peb_ref, w2_ref):
    kvw = NSA_KV_WIDTH
    h = jax.nn.gelu(ab[:, :kvw] + pltpu.roll(ab[:, kvw:], N_CMP_CHUNKS - 1, 0) + peb_ref[...])
    return jnp.dot(h.astype(jnp.bfloat16), w2_ref[...], preferred_element_type=jnp.float32)


def _cmp_keys(ab, peb_ref, w2_ref, gk_ref, bd_ref):
    kc = _cmp_second_layer(ab, peb_ref, w2_ref)
    ms = jnp.dot(kc * kc, bd_ref[...], precision=lax.Precision.HIGHEST, preferred_element_type=jnp.float32)
    return kc * lax.rsqrt(ms + RMS_EPS) * gk_ref[...]


def _cmp_select_kernel(slopes_ref, q_ref, abk_ref, abv_ref, pebk_ref, pebv_ref, w2k_ref, w2v_ref, gk_ref, bd_ref,
                       ov_ref, o_ref, mask_ref, kcb_ref, vcb_ref):
    i = pl.program_id(1)
    tq = ATTN_TILE

    @pl.when(i == 0)
    def _():
        kcb_ref[...] = _cmp_keys(abk_ref[0], pebk_ref, w2k_ref, gk_ref, bd_ref).astype(jnp.bfloat16)
        vcb_ref[...] = _cmp_second_layer(abv_ref[0], pebv_ref, w2v_ref).astype(jnp.bfloat16)

    lane = lax.broadcasted_iota(jnp.int32, (tq, LANES), 1)
    t_pos = i * tq + lax.broadcasted_iota(jnp.int32, (tq, LANES), 0)
    dist_i = t_pos - (CMP_STRIDE * lane + CMP_LEN - 1)
    keep = (dist_i >= 0) & (lane < N_CMP_CHUNKS - 1)
    dist = dist_i.astype(jnp.float32)
    own = jnp.right_shift(t_pos, SEL_BLOCK.bit_length() - 1)
    outs = []
    for g in range(NSA_KV_HEADS):
        kchunk, khalf = divmod(g, 2)
        kc_g = kcb_ref[:, kchunk * LANES:(kchunk + 1) * LANES]
        vc_g = vcb_ref[:, kchunk * LANES:(kchunk + 1) * LANES]
        in_half = (lane >= HEAD_DIM) if khalf else (lane < HEAD_DIM)
        psum = None
        for r in range(NSA_GROUP):
            h = g * NSA_GROUP + r
            qc = q_ref[0, :, (h // 2) * LANES:(h // 2 + 1) * LANES]
            if h % 2 != khalf:
                qc = pltpu.roll(qc, HEAD_DIM, 1)
            qb = (jnp.where(in_half, qc, 0.0) * SCALE).astype(jnp.bfloat16)
            s = jnp.where(keep, _nt(qb, kc_g) - slopes_ref[h] * dist, NEG)
            m = jnp.max(s, axis=1, keepdims=True)
            e = jnp.where(keep, jnp.exp(s - m), 0.0)
            l = jnp.sum(e, axis=1, keepdims=True)
            p = e / jnp.where(l > 0.0, l, 1.0)
            o = jnp.dot(p.astype(jnp.bfloat16), vc_g, preferred_element_type=jnp.float32)
            if h % 2 != khalf:
                o = pltpu.roll(o, HEAD_DIM, 1)
            outs.append(o)
            psum = p if psum is None else psum + p
        imp = jnp.dot(psum, ov_ref[...], precision=lax.Precision.HIGHEST, preferred_element_type=jnp.float32)
        score = jnp.where(lane == own, OWN_SCORE, jnp.where(lane < own, imp, NEG))
        mask_ref[0, :, g * LANES:(g + 1) * LANES] = _top_lanes(score, SEL_TOPK)
    for c in range(NSA_HEADS // 2):
        o_ref[0, :, c * LANES:(c + 1) * LANES] = jnp.where(lane < HEAD_DIM, outs[2 * c], outs[2 * c + 1])


def _cmp_consts(cw):
    c_start = np.arange(LANES) * CMP_STRIDE
    j_start = np.arange(LANES) * SEL_BLOCK
    ov = ((c_start[:, None] < j_start[None, :] + SEL_BLOCK) & (c_start[:, None] + CMP_LEN > j_start[None, :])
          & (np.arange(LANES)[:, None] < N_CMP_CHUNKS - 1)).astype(np.float32)
    bd = np.kron(np.eye(NSA_KV_HEADS, dtype=np.float32), np.full((HEAD_DIM, HEAD_DIM), 1.0 / HEAD_DIM, np.float32))
    return [cw['pebk'], cw['pebv'], cw['w2k'], cw['w2v'], cw['gk'], jnp.asarray(bd), jnp.asarray(ov)]


def _cmp_select(q, abk, abv, slopes, cw):
    b, t, _ = q.shape
    tq = ATTN_TILE
    kvw = NSA_KV_WIDTH
    consts = _cmp_consts(cw)
    return pl.pallas_call(
        _cmp_select_kernel,
        grid=(b, t // tq),
        in_specs=[pl.BlockSpec(memory_space=pltpu.SMEM),
                  pl.BlockSpec((1, tq, NSA_WIDTH), lambda bi, i: (bi, i, 0)),
                  pl.BlockSpec((1, N_CMP_CHUNKS, 2 * kvw), lambda bi, i: (bi, 0, 0)),
                  pl.BlockSpec((1, N_CMP_CHUNKS, 2 * kvw), lambda bi, i: (bi, 0, 0))]
                 + [pl.BlockSpec(c.shape, lambda bi, i: (0, 0)) for c in consts],
        out_specs=[pl.BlockSpec((1, tq, NSA_WIDTH), lambda bi, i: (bi, i, 0)),
                   pl.BlockSpec((1, tq, NSA_KV_HEADS * LANES), lambda bi, i: (bi, i, 0))],
        out_shape=[jax.ShapeDtypeStruct((b, t, NSA_WIDTH), jnp.float32),
                   jax.ShapeDtypeStruct((b, t, NSA_KV_HEADS * LANES), jnp.float32)],
        scratch_shapes=[pltpu.VMEM((N_CMP_CHUNKS, kvw), jnp.bfloat16), pltpu.VMEM((N_CMP_CHUNKS, kvw), jnp.bfloat16)],
        compiler_params=pltpu.CompilerParams(dimension_semantics=("arbitrary", "arbitrary")),
        name="cmp_select",
    )(slopes, q, abk, abv, *consts)


def _pages_natural_kernel(x_ref, o_ref):
    for j in range(x_ref.shape[0]):
        o_ref[j * PAGE_SIZE:(j + 1) * PAGE_SIZE, :] = x_ref[j].T


def _pages_natural(cache_t):
    p = cache_t.shape[0]
    per = 8
    width = 2 * NSA_KV_WIDTH
    return pl.pallas_call(
        _pages_natural_kernel,
        grid=(p // per,),
        in_specs=[pl.BlockSpec((per, width, PAGE_SIZE), lambda i: (i, 0, 0))],
        out_specs=pl.BlockSpec((per * PAGE_SIZE, width), lambda i: (i, 0)),
        out_shape=jax.ShapeDtypeStruct((p * PAGE_SIZE, width), jnp.float32),
        compiler_params=pltpu.CompilerParams(dimension_semantics=("arbitrary",)),
        name="pages_natural",
    )(cache_t)


def _one_query_softmax(s, keep, s_own):
    s = jnp.where(keep, s, NEG)
    m = jnp.maximum(jnp.max(s, axis=1, keepdims=True), s_own)
    e = jnp.where(keep, jnp.exp(s - m), 0.0)
    e_own = jnp.exp(s_own - m)
    inv = 1.0 / (jnp.sum(e, axis=1, keepdims=True) + e_own)
    return e * inv, e_own * inv


def _own_score(qb, k_row):
    return jnp.sum(qb.astype(jnp.float32) * k_row.astype(jnp.bfloat16).astype(jnp.float32), axis=1, keepdims=True)


def _nsa_sample_kernel(pt_ref, qb_ref, page_ref, abk_ref, abv_ref, win_ref, new_ref, gates_ref,
                       slopes_ref, pebk_ref, pebv_ref, w2k_ref, w2v_ref, gk_ref, bd_ref, ov_ref,
                       grp_ref, eslc_ref, hmask_ref, o_ref, s_all, v_all, abk_all, abv_all):
    p = pl.program_id(1)
    kvw = NSA_KV_WIDTH
    qb = qb_ref[0].astype(jnp.bfloat16)
    lane0 = pl.multiple_of(p * PAGE_SIZE, PAGE_SIZE)
    page = page_ref[0]
    s_all[:, pl.ds(lane0, PAGE_SIZE)] = jnp.dot(qb, page[:kvw].astype(jnp.bfloat16),
                                                preferred_element_type=jnp.float32)
    v_all[:, pl.ds(lane0, PAGE_SIZE)] = page[kvw:].astype(jnp.bfloat16)
    c0 = pl.multiple_of(p * CHUNKS_PER_PAGE, CHUNKS_PER_PAGE)
    abk_all[pl.ds(c0, CHUNKS_PER_PAGE), :] = abk_ref[0]
    abv_all[pl.ds(c0, CHUNKS_PER_PAGE), :] = abv_ref[0]

    @pl.when(p == N_PAGES - 1)
    def _():
        slopes = slopes_ref[...]
        new = new_ref[0]
        lane = lax.broadcasted_iota(jnp.int32, (NSA_HEADS, LANES), 1)

        def fold(full):
            m = full * hmask_ref[...]
            return m[0:4] + m[4:8] + m[8:12] + m[12:16]

        kc = _cmp_keys(abk_all[...], pebk_ref, w2k_ref, gk_ref, bd_ref)
        vc = _cmp_second_layer(abv_all[...], pebv_ref, w2v_ref)
        dist_c = (PAST_LEN - (CMP_LEN - 1) - CMP_STRIDE * lane).astype(jnp.float32)
        keep_c = lane < N_CMP_CHUNKS - 1
        s = jnp.where(keep_c, _nt(qb, kc.astype(jnp.bfloat16)) - slopes * dist_c, NEG)
        e = jnp.where(keep_c, jnp.exp(s - jnp.max(s, axis=1, keepdims=True)), 0.0)
        p_cmp = e / jnp.sum(e, axis=1, keepdims=True)
        o_cmp = jnp.dot(p_cmp.astype(jnp.bfloat16), vc.astype(jnp.bfloat16), preferred_element_type=jnp.float32)

        imp = jnp.dot(p_cmp, ov_ref[...], precision=lax.Precision.HIGHEST, preferred_element_type=jnp.float32)
        imp = jnp.dot(grp_ref[...], imp, precision=lax.Precision.HIGHEST, preferred_element_type=jnp.float32)
        sel = _top_lanes(jnp.where(lane < PAST_LEN // SEL_BLOCK, imp, NEG), SEL_TOPK - 1)
        picked = jnp.dot(sel.astype(jnp.bfloat16), eslc_ref[...], preferred_element_type=jnp.float32) > 0.5
        kpos = lax.broadcasted_iota(jnp.int32, (NSA_HEADS, PAST_LEN), 1)
        s = s_all[...] - slopes * (PAST_LEN - kpos).astype(jnp.float32)
        prob, p_own = _one_query_softmax(s, picked, _own_score(qb, new[:, 0:kvw]))
        o_slc = _nt(prob.astype(jnp.bfloat16), v_all[...]) + p_own * new[:, kvw:2 * kvw]

        win = win_ref[0]
        wpos = lax.broadcasted_iota(jnp.int32, (NSA_HEADS, WINDOW), 1)
        s = (jnp.dot(qb, win[:kvw].astype(jnp.bfloat16), preferred_element_type=jnp.float32)
             - slopes * (WINDOW - wpos).astype(jnp.float32))
        prob, p_own = _one_query_softmax(s, wpos >= 0, _own_score(qb, new[:, 2 * kvw:3 * kvw]))
        o_win = _nt(prob.astype(jnp.bfloat16), win[kvw:].astype(jnp.bfloat16)) + p_own * new[:, 3 * kvw:4 * kvw]

        g = gates_ref[0]
        o_ref[0] = g[0] * fold(o_cmp) + g[1] * fold(o_slc) + g[2] * fold(o_win)


def _moba_sample_kernel(pt_ref, qb_ref, page_ref, new_ref, slopes_ref, emoba_ref, hmask_ref, o_ref,
                        s_all, v_all, ksum_ref):
    p = pl.program_id(1)
    w = MOBA_WIDTH
    qf = qb_ref[0]
    qb = qf.astype(jnp.bfloat16)

    @pl.when(p == 0)
    def _():
        ksum_ref[...] = jnp.zeros_like(ksum_ref)

    lane0 = pl.multiple_of(p * PAGE_SIZE, PAGE_SIZE)
    page = page_ref[0]
    k_t = page[:w]
    s_all[:, pl.ds(lane0, PAGE_SIZE)] = jnp.dot(qb, k_t.astype(jnp.bfloat16), preferred_element_type=jnp.float32)
    v_all[:, pl.ds(lane0, PAGE_SIZE)] = page[w:].astype(jnp.bfloat16)
    blk = lax.broadcasted_iota(jnp.int32, ksum_ref.shape, 1)
    ksum_ref[...] += jnp.where(blk == p // (MOBA_BLOCK // PAGE_SIZE), jnp.sum(k_t, axis=1, keepdims=True), 0.0)

    @pl.when(p == N_PAGES - 1)
    def _():
        slopes = slopes_ref[...]
        new = new_ref[0]
        lane = lax.broadcasted_iota(jnp.int32, (MOBA_HEADS, LANES), 1)
        gate = jnp.dot(qf, ksum_ref[...], precision=lax.Precision.HIGHEST, preferred_element_type=jnp.float32)
        sel = _top_lanes(jnp.where(lane < PAST_LEN // MOBA_BLOCK, gate, NEG), MOBA_TOPK)
        picked = jnp.dot(sel.astype(jnp.bfloat16), emoba_ref[...], preferred_element_type=jnp.float32) > 0.5
        kpos = lax.broadcasted_iota(jnp.int32, (MOBA_HEADS, PAST_LEN), 1)
        s = s_all[...] - slopes * (PAST_LEN - kpos).astype(jnp.float32)
        prob, p_own = _one_query_softmax(s, picked, _own_score(qb, new[:, :w]))
        full = _nt(prob.astype(jnp.bfloat16), v_all[...]) + p_own * new[:, w:]
        o_ref[0] = jnp.sum(full * hmask_ref[...], axis=0, keepdims=True)


def _head_rows(q, lane_head):
    n = int(max(lane_head)) + 1
    onehot = jnp.asarray(np.eye(n, dtype=np.float32)[np.asarray(lane_head)])
    return jnp.einsum('bhd,hn->bhnd', q, onehot).reshape(q.shape[0], q.shape[1], n * HEAD_DIM)


def _block_expand_t(block):
    e = (np.arange(LANES)[:, None] == (np.arange(PAST_LEN)[None, :] // block)).astype(np.float32)
    return jnp.asarray(e, jnp.bfloat16)


def _const_spec(shape):
    nd = len(shape)
    return pl.BlockSpec(shape, lambda b, p, pt: (0,) * nd)


def _nsa_sample(page_table, q, cache_t, abk, abv, win_t, new, gates, slopes, cw):
    nseq = q.shape[0]
    kvw = NSA_KV_WIDTH
    group_of_head = [h // NSA_GROUP for h in range(NSA_HEADS)]
    qb = _head_rows(q * SCALE, group_of_head)
    hmask = _head_rows(jnp.ones((1, NSA_HEADS, HEAD_DIM), jnp.float32), group_of_head)[0]
    hh = np.arange(NSA_HEADS)
    grp = ((hh[:, None] // NSA_GROUP) == (hh[None, :] // NSA_GROUP)).astype(np.float32)
    consts = [slopes.reshape(NSA_HEADS, 1)] + _cmp_consts(cw) + [jnp.asarray(grp), _block_expand_t(SEL_BLOCK), hmask]
    grid_spec = pltpu.PrefetchScalarGridSpec(
        num_scalar_prefetch=1,
        grid=(nseq, N_PAGES),
        in_specs=[pl.BlockSpec((1, NSA_HEADS, kvw), lambda b, p, pt: (b, 0, 0)),
                  pl.BlockSpec((1, 2 * kvw, PAGE_SIZE), lambda b, p, pt: (pt[b, p], 1, 0)),
                  pl.BlockSpec((1, CHUNKS_PER_PAGE, 2 * kvw), lambda b, p, pt: (pt[b, p], 0, 0)),
                  pl.BlockSpec((1, CHUNKS_PER_PAGE, 2 * kvw), lambda b, p, pt: (pt[b, p], 0, 0)),
                  pl.BlockSpec((1, 2 * kvw, WINDOW), lambda b, p, pt: (b, 0, 0)),
                  pl.BlockSpec((1, 1, 4 * kvw), lambda b, p, pt: (b, 0, 0)),
                  pl.BlockSpec((1, 3, NSA_GROUP, kvw), lambda b, p, pt: (b, 0, 0, 0))]
                 + [_const_spec(c.shape) for c in consts],
        out_specs=pl.BlockSpec((1, NSA_GROUP, kvw), lambda b, p, pt: (b, 0, 0)),
        scratch_shapes=[pltpu.VMEM((NSA_HEADS, PAST_LEN), jnp.float32),
                        pltpu.VMEM((kvw, PAST_LEN), jnp.bfloat16),
                        pltpu.VMEM((N_CMP_CHUNKS, 2 * kvw), jnp.float32),
                        pltpu.VMEM((N_CMP_CHUNKS, 2 * kvw), jnp.float32)])
    return pl.pallas_call(
        _nsa_sample_kernel,
        grid_spec=grid_spec,
        out_shape=jax.ShapeDtypeStruct((nseq, NSA_GROUP, kvw), jnp.float32),
        compiler_params=pltpu.CompilerParams(dimension_semantics=("arbitrary", "arbitrary")),
        name="nsa_sample",
    )(page_table, qb, cache_t, abk, abv, win_t, new, gates, *consts)


def _moba_sample(page_table, q, cache_t, new, slopes):
    nseq = q.shape[0]
    w = MOBA_WIDTH
    own_head = list(range(MOBA_HEADS))
    qb = _head_rows(q * SCALE, own_head)
    hmask = _head_rows(jnp.ones((1, MOBA_HEADS, HEAD_DIM), jnp.float32), own_head)[0]
    consts = [slopes.reshape(MOBA_HEADS, 1), _block_expand_t(MOBA_BLOCK), hmask]
    grid_spec = pltpu.PrefetchScalarGridSpec(
        num_scalar_prefetch=1,
        grid=(nseq, N_PAGES),
        in_specs=[pl.BlockSpec((1, MOBA_HEADS, w), lambda b, p, pt: (b, 0, 0)),
                  pl.BlockSpec((1, 2 * w, PAGE_SIZE), lambda b, p, pt: (pt[b, p], 0, 0)),
                  pl.BlockSpec((1, 1, 2 * w), lambda b, p, pt: (b, 0, 0))]
                 + [_const_spec(c.shape) for c in consts],
        out_specs=pl.BlockSpec((1, 1, w), lambda b, p, pt: (b, 0, 0)),
        scratch_shapes=[pltpu.VMEM((MOBA_HEADS, PAST_LEN), jnp.float32),
                        pltpu.VMEM((w, PAST_LEN), jnp.bfloat16),
                        pltpu.VMEM((w, LANES), jnp.float32)])
    return pl.pallas_call(
        _moba_sample_kernel,
        grid_spec=grid_spec,
        out_shape=jax.ShapeDtypeStruct((nseq, 1, w), jnp.float32),
        compiler_params=pltpu.CompilerParams(dimension_semantics=("arbitrary", "arbitrary")),
        name="moba_sample",
    )(page_table, qb, cache_t, new, *consts)


def _peer_route_kernel(x_ref, g_ref, wq_ref, subk_ref, xb_ref, s0_ref, e0_ref, s1_ref, e1_ref, thr_ref):
    x = x_ref[...]
    xn = x * lax.rsqrt(jnp.mean(x * x, axis=-1, keepdims=True) + RMS_EPS) * g_ref[...]
    xb = xn.astype(jnp.bfloat16)
    xb_ref[...] = xb
    q = jnp.dot(xb, wq_ref[...], preferred_element_type=jnp.float32)
    half = PEER_QDIM // 2
    for p in range(PEER_HEADS):
        tabs = []
        for c in range(2):
            qpc = q[:, (2 * p + c) * half:(2 * p + c + 1) * half]
            s = _nt(qpc, subk_ref[2 * p + c], precision=lax.Precision.HIGHEST)
            st = s.T
            vals, w = _extract_top(st, PEER_TOPK)
            tabs.append((st, vals, w == F32_MIN))
        (s0, av, m0), (s1, bv, m1) = tabs
        cand = jnp.concatenate([av[a:a + 1, :] + bv for a in range(PEER_TOPK)], axis=0)
        best, _ = _extract_top(cand, PEER_TOPK)
        zsum = jnp.sum(jnp.exp(best - best[0:1, :]), axis=0, keepdims=True)
        s0_ref[p] = jnp.where(m0, s0, NEG)
        e0_ref[p] = jnp.where(m0, jnp.exp(s0 - av[0:1, :]), 0.0)
        s1_ref[p] = jnp.where(m1, s1, NEG)
        e1_ref[p] = jnp.where(m1, jnp.exp(s1 - bv[0:1, :]), 0.0) / zsum
        thr_ref[p:p + 1, :] = best[PEER_TOPK - 1:PEER_TOPK, :]


def _peer_expert_kernel(x_ref, xb_ref, s0_ref, e0_ref, s1_ref, e1_ref, thr_ref, u_ref, v_ref, o_ref):
    c = pl.program_id(1)

    @pl.when(c == 0)
    def _():
        o_ref[...] = x_ref[...]

    a = _nt(xb_ref[...], u_ref[...])
    ga = jax.nn.gelu(a)
    hs = []
    for k in range(PEER_CHUNK // PEER_KEYS):
        i1 = c * (PEER_CHUNK // PEER_KEYS) + k
        wt = None
        for p in range(PEER_HEADS):
            row_s = s0_ref[p, pl.ds(i1, 1), :]
            row_e = e0_ref[p, pl.ds(i1, 1), :]
            term = jnp.where(row_s + s1_ref[p] >= thr_ref[p:p + 1, :], row_e * e1_ref[p], 0.0)
            wt = term if wt is None else wt + term
        hs.append((wt.T * ga[:, k * PEER_KEYS:(k + 1) * PEER_KEYS]).astype(jnp.bfloat16))
    h = jnp.concatenate(hs, axis=1)
    o_ref[...] += jnp.dot(h, v_ref[...], preferred_element_type=jnp.float32)


def _peer_residual(x, g_ffn, wq_b, subk, u_b, v_b):
    n, d = x.shape
    t1 = 256 if n % 256 == 0 else n
    hp = PEER_HEADS
    tab = jax.ShapeDtypeStruct((hp, PEER_KEYS, n), jnp.float32)
    tab_spec = pl.BlockSpec((hp, PEER_KEYS, t1), lambda i: (0, 0, i))
    xb, s0, e0, s1, e1, thr = pl.pallas_call(
        _peer_route_kernel,
        grid=(n // t1,),
        in_specs=[pl.BlockSpec((t1, d), lambda i: (i, 0)),
                  pl.BlockSpec((1, d), lambda i: (0, 0)),
                  pl.BlockSpec((d, hp * PEER_QDIM), lambda i: (0, 0)),
                  pl.BlockSpec((2 * hp, PEER_KEYS, PEER_QDIM // 2), lambda i: (0, 0, 0))],
        out_specs=[pl.BlockSpec((t1, d), lambda i: (i, 0)), tab_spec, tab_spec, tab_spec, tab_spec,
                   pl.BlockSpec((hp, t1), lambda i: (0, i))],
        out_shape=[jax.ShapeDtypeStruct((n, d), jnp.bfloat16), tab, tab, tab, tab,
                   jax.ShapeDtypeStruct((hp, n), jnp.float32)],
        compiler_params=pltpu.CompilerParams(dimension_semantics=("arbitrary",), vmem_limit_bytes=VMEM_LIMIT),
        name="peer_route",
    )(x, g_ffn.reshape(1, d), wq_b, subk.reshape(2 * hp, PEER_KEYS, PEER_QDIM // 2))
    t2 = min(ROW_TILE, n)
    tab_spec2 = pl.BlockSpec((hp, PEER_KEYS, t2), lambda i, c: (0, 0, i))
    return pl.pallas_call(
        _peer_expert_kernel,
        grid=(n // t2, PEER_EXPERTS // PEER_CHUNK),
        in_specs=[pl.BlockSpec((t2, d), lambda i, c: (i, 0)),
                  pl.BlockSpec((t2, d), lambda i, c: (i, 0)),
                  tab_spec2, tab_spec2, tab_spec2, tab_spec2,
                  pl.BlockSpec((hp, t2), lambda i, c: (0, i)),
                  pl.BlockSpec((PEER_CHUNK, d), lambda i, c: (c, 0)),
                  pl.BlockSpec((PEER_CHUNK, d), lambda i, c: (c, 0))],
        out_specs=pl.BlockSpec((t2, d), lambda i, c: (i, 0)),
        out_shape=jax.ShapeDtypeStruct((n, d), jnp.float32),
        compiler_params=pltpu.CompilerParams(dimension_semantics=("arbitrary", "arbitrary"),
                                             vmem_limit_bytes=VMEM_LIMIT),
        name="peer_experts",
    )(x, xb, s0, e0, s1, e1, thr, u_b, v_b)


def _alibi_slopes():
    n = NSA_HEADS + MOBA_HEADS
    s = jnp.exp2(-8.0 * jnp.arange(1, n + 1, dtype=jnp.float32) / n)
    return s[0::2], s[1::2]


def _in_proj_weights(w_in, gains):
    w_t = jnp.transpose(w_in).astype(jnp.bfloat16)
    kvw = NSA_KV_WIDTH
    edges = np.cumsum([0, NSA_WIDTH] + [kvw] * 6 + [N_GATE] + [MOBA_WIDTH] * 3)
    q_n, kc, vc, ks, vs, kw, vw, gt, q_m, k_m, v_m = [w_t[a:b] for a, b in zip(edges[:-1], edges[1:])]

    def gain(key, n_heads):
        return jnp.tile(gains[key].reshape(HEAD_DIM), n_heads)

    def raw(n):
        return jnp.ones((n,), jnp.float32)

    def pack(parts, gain_parts, flags):
        flag = jnp.concatenate([jnp.full((p.shape[0],), f, jnp.float32) for p, f in zip(parts, flags)])
        return jnp.concatenate(parts, axis=0), jnp.concatenate(gain_parts).reshape(-1, 1), flag.reshape(-1, 1)

    per_step = N_GATE // 2
    pad = jnp.zeros((LANES - per_step, D_MODEL), jnp.bfloat16)
    gt_rows = [gt[:per_step], pad, gt[per_step:], pad]
    return {
        'q_n': pack([q_n], [gain('g_q_nsa', NSA_HEADS)], [1.0]),
        'nsa': pack([kc, vc, ks, vs], [raw(kvw), raw(kvw), gain('g_k_slc', NSA_KV_HEADS), raw(kvw)], [0., 0., 1., 0.]),
        'win': pack([kw, vw], [gain('g_k_win', NSA_KV_HEADS), raw(kvw)], [1., 0.]),
        'cmp': pack([kc, vc], [raw(kvw), raw(kvw)], [0., 0.]),
        'gates': pack(gt_rows, [raw(2 * LANES)], [0., 0., 0., 0.]),
        'q_m': pack([q_m], [gain('g_q_moba', MOBA_HEADS)], [1.0]),
        'moba': pack([k_m, v_m], [gain('g_k_moba', MOBA_HEADS), raw(MOBA_WIDTH)], [1., 0.]),
    }


def _in_proj(x, g_mix, pw, batch):
    h_t = _norm_transpose(x, g_mix)
    out = {}
    for key in ('nsa', 'win', 'moba'):
        out[key] = _proj_t(h_t, *pw[key], batch=batch, rows=512)
    for key in ('q_n', 'q_m', 'cmp'):
        out[key] = _proj_t(h_t, *pw[key], batch=batch, rows=512, natural=True)
    out['gates'] = _proj_t(h_t, *pw['gates'], batch=batch, rows=2 * LANES, natural=True, act='sigmoid')
    return out


def _rows_leaf(rows_t, kinds, heads):
    b, _, t = rows_t.shape
    return jnp.transpose(rows_t.reshape(b, kinds, heads, HEAD_DIM, t), (0, 4, 1, 2, 3))[None]


def kernel(x_prompt, x_sample, cache_nsa, cache_moba, state_win, page_table, p_prompt, p_sample,
           g_mix, w_in, g_q_nsa, g_k_cmp, g_k_slc, g_k_win, g_q_moba, g_k_moba,
           cmp_pe_k, cmp_w1_k, cmp_w2_k, cmp_pe_v, cmp_w1_v, cmp_w2_v,
           g_out_nsa, g_out_moba, w_out, g_ffn, w_peer_q, peer_sub_keys, peer_u, peer_v,
           g_ple, w_ple_gate, w_ple_proj, g_ple_post):
    assert w_in.shape[0] == 1, "single-layer trunk"
    slopes_nsa, slopes_moba = _alibi_slopes()
    kvw = NSA_KV_WIDTH
    bf = jnp.bfloat16
    pw = _in_proj_weights(w_in[0], {'g_q_nsa': g_q_nsa[0], 'g_k_slc': g_k_slc[0], 'g_k_win': g_k_win[0],
                                    'g_q_moba': g_q_moba[0], 'g_k_moba': g_k_moba[0]})
    cw = {}
    w1k_bd, cw['w2k'], cw['pebk'] = _cmp_weights(cmp_w1_k[0], cmp_w2_k[0], cmp_pe_k[0])
    w1v_bd, cw['w2v'], cw['pebv'] = _cmp_weights(cmp_w1_v[0], cmp_w2_v[0], cmp_pe_v[0])
    cw['gk'] = jnp.tile(g_k_cmp[0].reshape(1, HEAD_DIM), (1, NSA_KV_HEADS))
    w_out_b, wq_b, u_b, v_b = w_out[0].astype(bf), w_peer_q[0].astype(bf), peer_u[0].astype(bf), peer_v[0].astype(bf)
    wg_b, wp_b = w_ple_gate[0].astype(bf), w_ple_proj[0].astype(bf)

    def tail(x, o_nsa, o_moba, ple):
        x = _out_proj(o_nsa, o_moba, x, g_out_nsa[0], g_out_moba[0], w_out_b)
        x = _peer_residual(x, g_ffn[0], wq_b, peer_sub_keys[0], u_b, v_b)
        return _ple(x, ple, g_ple[0], wg_b, wp_b, g_ple_post[0])

    b, t, d = x_prompt.shape
    xp = x_prompt.reshape(b * t, d)
    pr = _in_proj(xp, g_mix[0], pw, b)
    q_n = pr['q_n'].reshape(b, t, NSA_WIDTH)
    abk, abv = _cmp_chunks(pr['cmp'].reshape(b * t // CMP_STRIDE, CMP_STRIDE * 2 * kvw), w1k_bd, w1v_bd)
    o_cmp, mask = _cmp_select(q_n, abk.reshape(b, t // CMP_STRIDE, 2 * kvw), abv.reshape(b, t // CMP_STRIDE, 2 * kvw),
                              slopes_nsa, cw)
    o_slc = _attention(q_n, pr['nsa'], slopes_nsa, mode='mask', heads=NSA_STEP_HEADS, k_row0=4, v_row0=6,
                       mask=mask, block=SEL_BLOCK, n_masks=2)
    o_nsa = _attention(q_n, pr['win'], slopes_nsa, mode='window', heads=NSA_STEP_HEADS, k_row0=0, v_row0=2,
                       combine=(o_cmp, o_slc, pr['gates'].reshape(b, t, 2 * LANES)))
    o_moba = _attention(pr['q_m'].reshape(b, t, MOBA_WIDTH), pr['moba'], slopes_moba, mode='gate',
                        heads=MOBA_STEP_HEADS, k_row0=0, v_row0=MOBA_WIDTH // LANES, block=MOBA_BLOCK)
    y_prompt = tail(xp, o_nsa.reshape(b * t, NSA_WIDTH), o_moba.reshape(b * t, MOBA_WIDTH),
                    p_prompt[0].reshape(b * t, PLE_DIM)).reshape(b, t, d)

    ns = x_sample.shape[0]
    xs = x_sample.reshape(ns, d)
    sr = _in_proj(xs, g_mix[0], pw, 1)
    n_phys = cache_nsa.shape[1]
    cache_n_t = jnp.transpose(cache_nsa[0], (0, 2, 3, 4, 1)).reshape(n_phys, 4 * kvw, PAGE_SIZE)
    cache_m_t = jnp.transpose(cache_moba[0], (0, 2, 3, 4, 1)).reshape(n_phys, 2 * MOBA_WIDTH, PAGE_SIZE)
    win_t = jnp.transpose(state_win[0], (0, 2, 3, 4, 1)).reshape(ns, 2 * kvw, WINDOW)
    nat = _pages_natural(cache_n_t)
    abk, abv = _cmp_chunks(nat.reshape(n_phys * CHUNKS_PER_PAGE, CMP_STRIDE * 2 * kvw), w1k_bd, w1v_bd)
    nsa_new = jnp.transpose(sr['nsa'][0])
    win_new = jnp.transpose(sr['win'][0])
    moba_new = jnp.transpose(sr['moba'][0])
    new_n = jnp.concatenate([nsa_new[:, 2 * kvw:], win_new], axis=1)[:, None, :]
    gs = sr['gates']
    gs = jnp.concatenate([gs[:, :N_GATE // 2], gs[:, LANES:LANES + N_GATE // 2]], axis=1)
    g3 = jnp.transpose(gs.reshape(ns, NSA_KV_HEADS, NSA_GROUP, 3), (0, 3, 2, 1))[..., None]
    g3 = jnp.broadcast_to(g3, (ns, 3, NSA_GROUP, NSA_KV_HEADS, HEAD_DIM)).reshape(ns, 3, NSA_GROUP, kvw)
    o_nsa_s = _nsa_sample(page_table, sr['q_n'].reshape(ns, NSA_HEADS, HEAD_DIM), cache_n_t,
                          abk.reshape(n_phys, CHUNKS_PER_PAGE, 2 * kvw), abv.reshape(n_phys, CHUNKS_PER_PAGE, 2 * kvw),
                          win_t, new_n, g3, slopes_nsa, cw)
    o_nsa_s = jnp.transpose(o_nsa_s.reshape(ns, NSA_GROUP, NSA_KV_HEADS, HEAD_DIM), (0, 2, 1, 3)).reshape(ns, NSA_WIDTH)
    o_moba_s = _moba_sample(page_table, sr['q_m'].reshape(ns, MOBA_HEADS, HEAD_DIM), cache_m_t,
                            moba_new[:, None, :], slopes_moba).reshape(ns, MOBA_WIDTH)
    y_sample = tail(xs, o_nsa_s, o_moba_s, p_sample[0].reshape(ns, PLE_DIM)).reshape(ns, 1, d)

    win_prompt = _rows_leaf(pr['win'][:, :, t - WINDOW:], 2, NSA_KV_HEADS)
    nsa_rows_s = nsa_new.reshape(1, ns, 1, 4, NSA_KV_HEADS, HEAD_DIM)
    moba_rows_s = moba_new.reshape(1, ns, 1, 2, MOBA_HEADS, HEAD_DIM)
    win_s = jnp.concatenate([state_win[0][:, 1:], win_new.reshape(ns, 1, 2, NSA_KV_HEADS, HEAD_DIM)], axis=1)[None]
    return (y_prompt, y_sample, _rows_leaf(pr['nsa'], 4, NSA_KV_HEADS), _rows_leaf(pr['moba'], 2, MOBA_HEADS),
            win_prompt, nsa_rows_s, moba_rows_s, win_s)
```

```python
import functools

import jax
import jax.numpy as jnp
from jax import lax
import numpy as np
from jax.experimental import pallas as pl
from jax.experimental.pallas import tpu as pltpu

D_MODEL = 2048
PAST_LEN = 2048
PAGE_SIZE = 128
HEAD_DIM = 64
NSA_HEADS = D_MODEL // (2 * HEAD_DIM)
NSA_KV_HEADS = max(1, NSA_HEADS // 4)
NSA_GROUP = NSA_HEADS // NSA_KV_HEADS
MOBA_HEADS = D_MODEL // (2 * HEAD_DIM)
NSA_WIDTH = NSA_HEADS * HEAD_DIM
NSA_KV_WIDTH = NSA_KV_HEADS * HEAD_DIM
MOBA_WIDTH = MOBA_HEADS * HEAD_DIM
N_GATE = 3 * NSA_HEADS
CMP_LEN = 32
CMP_STRIDE = 16
CMP_HIDDEN = 64
SEL_BLOCK = 64
SEL_TOPK = 8
WINDOW = 512
MOBA_BLOCK = 256
MOBA_TOPK = 3
PEER_KEYS = 128
PEER_EXPERTS = PEER_KEYS * PEER_KEYS
PEER_HEADS = 8
PEER_TOPK = 16
PEER_QDIM = 256
PLE_DIM = 256
RMS_EPS = 1e-6
NEG = -1e30
OWN_SCORE = 1e9
SCALE = HEAD_DIM ** -0.5

LANES = 128
ATTN_TILE = 256
ROW_TILE = 512
PEER_CHUNK = 512
VMEM_LIMIT = 56 * 1024 * 1024
N_PAGES = PAST_LEN // PAGE_SIZE
PAGES_PER_STEP = 4
N_CMP_CHUNKS = PAST_LEN // CMP_STRIDE
CHUNKS_PER_PAGE = PAGE_SIZE // CMP_STRIDE
F32_MIN = float(np.finfo(np.float32).min)

MOBA_STEP_HEADS = ((0, 0, 0), (HEAD_DIM, 1, 1))
NSA_STEP_HEADS = tuple((g * NSA_GROUP * HEAD_DIM + r * HEAD_DIM, g, g) for g in range(2) for r in range(NSA_GROUP))


def _nt(a, b, **kw):
    return lax.dot_general(a, b, (((1,), (1,)), ((), ())), preferred_element_type=jnp.float32, **kw)


def _top_lanes(score, k):
    lane = lax.broadcasted_iota(jnp.int32, score.shape, 1)
    sel = jnp.zeros(score.shape, jnp.float32)
    for _ in range(k):
        m = jnp.max(score, axis=1, keepdims=True)
        first = jnp.min(jnp.where(score == m, lane, LANES), axis=1, keepdims=True)
        hit = lane == first
        sel = jnp.where(hit & (m > 0.5 * NEG), 1.0, sel)
        score = jnp.where(hit, F32_MIN, score)
    return sel


def _extract_top(work, k):
    r_total, t = work.shape
    rowid = lax.broadcasted_iota(jnp.int32, (r_total, t), 0)
    kid = lax.broadcasted_iota(jnp.int32, (k, t), 0)

    def body(r, carry):
        w, vals = carry
        mx = jnp.max(w, axis=0, keepdims=True)
        first = jnp.min(jnp.where(w == mx, rowid, r_total), axis=0, keepdims=True)
        w = jnp.where(rowid == first, F32_MIN, w)
        vals = jnp.where(kid == r, mx, vals)
        return w, vals

    w, vals = lax.fori_loop(0, k, body, (work, jnp.zeros((k, t), jnp.float32)))
    return vals, w


def _norm_t_kernel(x_ref, g_ref, o_ref):
    x = x_ref[...]
    xn = x * lax.rsqrt(jnp.mean(x * x, axis=-1, keepdims=True) + RMS_EPS) * g_ref[...]
    o_ref[...] = xn.T.astype(jnp.bfloat16)


def _norm_transpose(x, g):
    n, d = x.shape
    tm = min(ROW_TILE, n)
    return pl.pallas_call(
        _norm_t_kernel,
        grid=(n // tm,),
        in_specs=[pl.BlockSpec((tm, d), lambda i: (i, 0)), pl.BlockSpec((1, d), lambda i: (0, 0))],
        out_specs=pl.BlockSpec((d, tm), lambda i: (0, i)),
        out_shape=jax.ShapeDtypeStruct((d, n), jnp.bfloat16),
        compiler_params=pltpu.CompilerParams(dimension_semantics=("arbitrary",)),
        name="norm_transpose",
    )(x, g.reshape(1, d))


def _proj_t_kernel(act, natural, w_ref, ht_ref, gain_ref, flag_ref, o_ref):
    y = jnp.dot(w_ref[...], ht_ref[...], preferred_element_type=jnp.float32)
    r, tm = y.shape
    y3 = y.reshape(r // HEAD_DIM, HEAD_DIM, tm)
    ms = jnp.mean(y3 * y3, axis=1, keepdims=True)
    yn = (y3 * lax.rsqrt(ms + RMS_EPS)).reshape(r, tm) * gain_ref[...]
    y = jnp.where(flag_ref[...] > 0.5, yn, y)
    if act == 'sigmoid':
        y = jax.nn.sigmoid(y)
    if natural:
        o_ref[...] = y.T
    else:
        o_ref[0] = y


def _proj_t(ht, w_t, gain, flag, *, batch, rows, act=None, natural=False):
    c, d = w_t.shape
    n = ht.shape[1]
    t = n // batch
    tm = min(ROW_TILE, t)
    nt = t // tm
    if natural:
        out_shape = jax.ShapeDtypeStruct((n, c), jnp.float32)
        out_spec = pl.BlockSpec((tm, rows), lambda b, i, j: (b * nt + i, j))
    else:
        out_shape = jax.ShapeDtypeStruct((batch, c, t), jnp.float32)
        out_spec = pl.BlockSpec((1, rows, tm), lambda b, i, j: (b, j, i))
    return pl.pallas_call(
        functools.partial(_proj_t_kernel, act, natural),
        grid=(batch, nt, c // rows),
        in_specs=[pl.BlockSpec((rows, d), lambda b, i, j: (j, 0)),
                  pl.BlockSpec((d, tm), lambda b, i, j: (0, b * nt + i)),
                  pl.BlockSpec((rows, 1), lambda b, i, j: (j, 0)),
                  pl.BlockSpec((rows, 1), lambda b, i, j: (j, 0))],
        out_specs=out_spec,
        out_shape=out_shape,
        compiler_params=pltpu.CompilerParams(dimension_semantics=("arbitrary",) * 3),
        name="proj_t",
    )(w_t, ht, gain, flag)


def _out_proj_kernel(on_ref, om_ref, x_ref, gn_ref, gm_ref, w_ref, o_ref, mix_ref):
    @pl.when(pl.program_id(1) == 0)
    def _():
        for src, g_ref, lo in ((on_ref, gn_ref, 0), (om_ref, gm_ref, NSA_WIDTH)):
            a = src[...]
            an = a * lax.rsqrt(jnp.mean(a * a, axis=-1, keepdims=True) + RMS_EPS) * g_ref[...]
            mix_ref[:, lo:lo + a.shape[1]] = an.astype(jnp.bfloat16)

    o_ref[...] = x_ref[...] + jnp.dot(mix_ref[...], w_ref[...], preferred_element_type=jnp.float32)


def _out_proj(o_nsa, o_moba, x, g_nsa, g_moba, w_b):
    n, d = x.shape
    tm = min(ROW_TILE, n)
    tn = 512
    return pl.pallas_call(
        _out_proj_kernel,
        grid=(n // tm, d // tn),
        in_specs=[pl.BlockSpec((tm, NSA_WIDTH), lambda i, j: (i, 0)),
                  pl.BlockSpec((tm, MOBA_WIDTH), lambda i, j: (i, 0)),
                  pl.BlockSpec((tm, tn), lambda i, j: (i, j)),
                  pl.BlockSpec((1, NSA_WIDTH), lambda i, j: (0, 0)),
                  pl.BlockSpec((1, MOBA_WIDTH), lambda i, j: (0, 0)),
                  pl.BlockSpec((NSA_WIDTH + MOBA_WIDTH, tn), lambda i, j: (0, j))],
        out_specs=pl.BlockSpec((tm, tn), lambda i, j: (i, j)),
        out_shape=jax.ShapeDtypeStruct((n, d), jnp.float32),
        scratch_shapes=[pltpu.VMEM((tm, NSA_WIDTH + MOBA_WIDTH), jnp.bfloat16)],
        compiler_params=pltpu.CompilerParams(dimension_semantics=("arbitrary", "arbitrary")),
        name="out_proj",
    )(o_nsa, o_moba, x, g_nsa.reshape(1, -1), g_moba.reshape(1, -1), w_b)


def _ple_kernel(tn, x_ref, p_ref, g_ref, wg_ref, wp_ref, gp_ref, o_ref, xn_ref, pn_ref):
    j = pl.program_id(1)

    @pl.when(j == 0)
    def _():
        x = x_ref[...]
        xn_ref[...] = (x * lax.rsqrt(jnp.mean(x * x, axis=-1, keepdims=True) + RMS_EPS) * g_ref[...]
                       ).astype(jnp.bfloat16)
        pr = jnp.dot(p_ref[...].astype(jnp.bfloat16), wp_ref[...], preferred_element_type=jnp.float32)
        pn_ref[...] = pr * lax.rsqrt(jnp.mean(pr * pr, axis=-1, keepdims=True) + RMS_EPS) * gp_ref[...]

    col = pl.multiple_of(j * tn, tn)
    gate = jax.nn.sigmoid(jnp.dot(xn_ref[...], wg_ref[...], preferred_element_type=jnp.float32))
    o_ref[...] = x_ref[:, pl.ds(col, tn)] + gate * pn_ref[:, pl.ds(col, tn)]


def _ple(x, ple, g_ple, wg_b, wp_b, g_post):
    n, d = x.shape
    tm = min(ROW_TILE, n)
    tn = 512
    return pl.pallas_call(
        functools.partial(_ple_kernel, tn),
        grid=(n // tm, d // tn),
        in_specs=[pl.BlockSpec((tm, d), lambda i, j: (i, 0)),
                  pl.BlockSpec((tm, PLE_DIM), lambda i, j: (i, 0)),
                  pl.BlockSpec((1, d), lambda i, j: (0, 0)),
                  pl.BlockSpec((d, tn), lambda i, j: (0, j)),
                  pl.BlockSpec((PLE_DIM, d), lambda i, j: (0, 0)),
                  pl.BlockSpec((1, d), lambda i, j: (0, 0))],
        out_specs=pl.BlockSpec((tm, tn), lambda i, j: (i, j)),
        out_shape=jax.ShapeDtypeStruct((n, d), jnp.float32),
        scratch_shapes=[pltpu.VMEM((tm, d), jnp.bfloat16), pltpu.VMEM((tm, d), jnp.float32)],
        compiler_params=pltpu.CompilerParams(dimension_semantics=("arbitrary", "arbitrary")),
        name="ple_gate",
    )(x, ple, g_ple.reshape(1, d), wg_b, wp_b, g_post.reshape(1, d))


def _attn_kernel(heads, mode, *refs):
    if mode == 'mask':
        slopes_ref, q_ref, k_ref, v_ref, mask_ref, expand_ref, o_ref, kb_ref, vb_ref = refs
    elif mode == 'gate':
        slopes_ref, q_ref, k_ref, v_ref, expand_ref, o_ref, kb_ref, vb_ref, means_ref = refs
    else:
        slopes_ref, q_ref, k_ref, v_ref, ocmp_ref, oslc_ref, gates_ref, o_ref, kb_ref, vb_ref = refs
    s_idx = pl.program_id(1)
    i = pl.program_id(2)
    tq = ATTN_TILE
    n_heads = len(heads)
    t_total = k_ref.shape[2]
    lane = lax.broadcasted_iota(jnp.int32, (tq, LANES), 1)

    @pl.when(i == 0)
    def _():
        kb_ref[...] = k_ref[0].astype(jnp.bfloat16)
        vb_ref[...] = v_ref[0].astype(jnp.bfloat16)
        if mode == 'gate':
            sq = lax.broadcasted_iota(jnp.int32, (LANES, LANES), 1)
            means = jnp.zeros((LANES, LANES), jnp.float32)
            for n in range(t_total // MOBA_BLOCK):
                col = jnp.mean(k_ref[0, :, n * MOBA_BLOCK:(n + 1) * MOBA_BLOCK], axis=1, keepdims=True)
                means = jnp.where(sq == n, col, means)
            means_ref[...] = means.T

    sel_b = {}
    if mode == 'mask':
        for m in range(mask_ref.shape[2] // LANES):
            sel_b[m] = mask_ref[0, :, m * LANES:(m + 1) * LANES].astype(jnp.bfloat16)

    row = lax.broadcasted_iota(jnp.int32, (tq, tq), 0)
    col = lax.broadcasted_iota(jnp.int32, (tq, tq), 1)
    rc = (row - col).astype(jnp.float32)
    diag0 = pl.multiple_of(i * tq, tq)

    qbs, slopes = [], []
    for h, (q_off, kv_half, m_idx) in enumerate(heads):
        slopes.append(slopes_ref[s_idx * n_heads + h])
        chunk = q_off // LANES
        qc = q_ref[0, :, chunk * LANES:(chunk + 1) * LANES]
        if (q_off % LANES) // HEAD_DIM != kv_half:
            qc = pltpu.roll(qc, HEAD_DIM, 1)
        in_half = (lane >= HEAD_DIM) if kv_half else (lane < HEAD_DIM)
        qh = jnp.where(in_half, qc, 0.0)
        qbs.append((qh * SCALE).astype(jnp.bfloat16))
        if mode == 'gate':
            n_blk = t_total // MOBA_BLOCK
            gate = _nt(means_ref[...], qh, precision=lax.Precision.HIGHEST)[0:n_blk]
            blk = lax.broadcasted_iota(jnp.int32, (n_blk, tq), 0)
            gate = jnp.where(blk < i, gate, NEG)
            _, taken = _extract_top(gate, MOBA_TOPK)
            sel = ((taken == F32_MIN) & (gate > 0.5 * NEG)) | (blk == i)
            sel = jnp.concatenate([sel.astype(jnp.float32), jnp.zeros((LANES - n_blk, tq), jnp.float32)], axis=0)
            sel_b[m_idx] = sel.T.astype(jnp.bfloat16)

    def tile_step(start, dist, tile_keep, states):
        k_tile = kb_ref[:, pl.ds(start, tq)]
        v_tile = vb_ref[:, pl.ds(start, tq)]
        picked = {m: jnp.dot(sb, expand_ref[:, pl.ds(start, tq)], preferred_element_type=jnp.float32) > 0.5
                  for m, sb in sel_b.items()}
        new_states = []
        for h, (_, _, m_idx) in enumerate(heads):
            keep = picked[m_idx] if mode != 'window' else tile_keep
            if mode != 'window' and tile_keep is not None:
                keep = keep & tile_keep
            s = jnp.dot(qbs[h], k_tile, preferred_element_type=jnp.float32) - slopes[h] * dist
            s = jnp.where(keep, s, NEG)
            m_tile = jnp.max(s, axis=1, keepdims=True)
            if states is None:
                p = jnp.exp(s - m_tile)
                new_states += [m_tile, jnp.sum(p, axis=1, keepdims=True), _nt(p.astype(jnp.bfloat16), v_tile)]
            else:
                m_prev, l_prev, acc = states[3 * h:3 * h + 3]
                m_new = jnp.maximum(m_prev, m_tile)
                alpha = jnp.exp(m_prev - m_new)
                p = jnp.exp(s - m_new)
                new_states += [m_new, alpha * l_prev + jnp.sum(p, axis=1, keepdims=True),
                               alpha * acc + _nt(p.astype(jnp.bfloat16), v_tile)]
        return tuple(new_states)

    states = tile_step(diag0, rc, rc >= 0.0, None)

    def body(n, states):
        start = pl.multiple_of(n * tq, tq)
        dist = rc + ((i - n) * tq).astype(jnp.float32)
        tile_keep = None
        if mode == 'window':
            tile_keep = rc <= jnp.where(n == i - WINDOW // tq, 0.0, float(tq))
        return tile_step(start, dist, tile_keep, states)

    first_tile = jnp.maximum(i - WINDOW // tq, 0) if mode == 'window' else 0
    states = lax.fori_loop(first_tile, i, body, states)
    outs = []
    for h, (q_off, kv_half, _) in enumerate(heads):
        o = states[3 * h + 2] / states[3 * h + 1]
        if (q_off % LANES) // HEAD_DIM != kv_half:
            o = pltpu.roll(o, HEAD_DIM, 1)
        outs.append(o)

    for c in range(n_heads // 2):
        o = jnp.where(lane < HEAD_DIM, outs[2 * c], outs[2 * c + 1])
        if mode == 'window':
            g = gates_ref[0]

            def gate_of(branch):
                lo = g[:, 3 * (2 * c) + branch:3 * (2 * c) + branch + 1]
                hi = g[:, 3 * (2 * c + 1) + branch:3 * (2 * c + 1) + branch + 1]
                return jnp.where(lane < HEAD_DIM, lo, hi)

            sl = slice(c * LANES, (c + 1) * LANES)
            o = gate_of(0) * ocmp_ref[0, :, sl] + gate_of(1) * oslc_ref[0, :, sl] + gate_of(2) * o
        o_ref[0, :, c * LANES:(c + 1) * LANES] = o


def _attention(q, kv_t, slopes, *, mode, heads, k_row0, v_row0, mask=None, block=None, n_masks=0, combine=None):
    b, t, qcols = q.shape
    qw = (len(heads) // 2) * LANES
    n_steps = qcols // qw
    tq = ATTN_TILE
    assert mode != 'gate' or MOBA_BLOCK == tq
    q_spec = pl.BlockSpec((1, tq, qw), lambda bi, s, i: (bi, i, s))
    in_specs = [pl.BlockSpec(memory_space=pltpu.SMEM), q_spec,
                pl.BlockSpec((1, LANES, t), lambda bi, s, i: (bi, k_row0 + s, 0)),
                pl.BlockSpec((1, LANES, t), lambda bi, s, i: (bi, v_row0 + s, 0))]
    args = [slopes, q, kv_t, kv_t]
    scratch = [pltpu.VMEM((LANES, t), jnp.bfloat16), pltpu.VMEM((LANES, t), jnp.bfloat16)]
    if mode == 'mask':
        in_specs.append(pl.BlockSpec((1, tq, n_masks * LANES), lambda bi, s, i: (bi, i, s)))
        args.append(mask)
    if mode == 'window':
        in_specs += [q_spec, q_spec, pl.BlockSpec((1, tq, LANES), lambda bi, s, i: (bi, i, s))]
        args += list(combine)
    else:
        expand = (np.arange(LANES)[:, None] == (np.arange(t)[None, :] // block)).astype(np.float32)
        in_specs.append(pl.BlockSpec((LANES, t), lambda bi, s, i: (0, 0)))
        args.append(jnp.asarray(expand, jnp.bfloat16))
    if mode == 'gate':
        scratch.append(pltpu.VMEM((LANES, LANES), jnp.float32))
    return pl.pallas_call(
        functools.partial(_attn_kernel, heads, mode),
        grid=(b, n_steps, t // tq),
        in_specs=in_specs,
        out_specs=q_spec,
        out_shape=jax.ShapeDtypeStruct((b, t, qcols), jnp.float32),
        scratch_shapes=scratch,
        compiler_params=pltpu.CompilerParams(dimension_semantics=("arbitrary",) * 3),
        name="attention_" + mode,
    )(*args)


def _cmp_chunk_kernel(x_ref, wk_ref, wv_ref, abk_ref, abv_ref):
    @pl.when(pl.program_id(1) == 0)
    def _():
        abk_ref[...] = jnp.zeros_like(abk_ref)
        abv_ref[...] = jnp.zeros_like(abv_ref)

    x = x_ref[...]
    abk_ref[...] += jnp.dot(x[:, :NSA_KV_WIDTH].astype(jnp.bfloat16), wk_ref[0], preferred_element_type=jnp.float32)
    abv_ref[...] += jnp.dot(x[:, NSA_KV_WIDTH:].astype(jnp.bfloat16), wv_ref[0], preferred_element_type=jnp.float32)


def _cmp_chunks(rows, wk_bd, wv_bd):
    n = rows.shape[0]
    tm = next(c for c in (1024, 512, 256, 128) if n % c == 0)
    width = 2 * NSA_KV_WIDTH
    out = jax.ShapeDtypeStruct((n, width), jnp.float32)
    return pl.pallas_call(
        _cmp_chunk_kernel,
        grid=(n // tm, CMP_STRIDE),
        in_specs=[pl.BlockSpec((tm, width), lambda i, r: (i, r)),
                  pl.BlockSpec((1, NSA_KV_WIDTH, width), lambda i, r: (r, 0, 0)),
                  pl.BlockSpec((1, NSA_KV_WIDTH, width), lambda i, r: (r, 0, 0))],
        out_specs=[pl.BlockSpec((tm, width), lambda i, r: (i, 0)), pl.BlockSpec((tm, width), lambda i, r: (i, 0))],
        out_shape=[out, out],
        compiler_params=pltpu.CompilerParams(dimension_semantics=("arbitrary", "arbitrary")),
        name="cmp_chunks",
    )(rows, wk_bd, wv_bd)


def _cmp_weights(w1, w2, pe):
    w1r = w1.reshape(2, CMP_STRIDE, HEAD_DIM, CMP_HIDDEN)
    eye = jnp.eye(NSA_KV_HEADS, dtype=w1.dtype)
    w1_bd = jnp.einsum('gG,ardh->rgdaGh', eye, w1r).reshape(CMP_STRIDE, NSA_KV_WIDTH, 2 * NSA_KV_WIDTH)
    w2_bd = jnp.kron(eye, w2)
    bias = jnp.dot(pe.reshape(1, CMP_LEN * HEAD_DIM), w1, precision=lax.Precision.HIGHEST)
    return w1_bd.astype(jnp.bfloat16), w2_bd.astype(jnp.bfloat16), jnp.tile(bias, (1, NSA_KV_HEADS))


def _cmp_second_layer(ab, peb_ref, w2_ref):
    kvw = NSA_KV_WIDTH
    h = jax.nn.gelu(ab[:, :kvw] + pltpu.roll(ab[:, kvw:], N_CMP_CHUNKS - 1, 0) + peb_ref[...])
    return jnp.dot(h.astype(jnp.bfloat16), w2_ref[...], preferred_element_type=jnp.float32)


def _cmp_keys(ab, peb_ref, w2_ref, gk_ref, bd_ref):
    kc = _cmp_second_layer(ab, peb_ref, w2_ref)
    ms = jnp.dot(kc * kc, bd_ref[...], precision=lax.Precision.HIGHEST, preferred_element_type=jnp.float32)
    return kc * lax.rsqrt(ms + RMS_EPS) * gk_ref[...]


def _cmp_select_kernel(slopes_ref, q_ref, abk_ref, abv_ref, pebk_ref, pebv_ref, w2k_ref, w2v_ref, gk_ref, bd_ref,
                       ov_ref, o_ref, mask_ref, kcb_ref, vcb_ref):
    i = pl.program_id(1)
    tq = ATTN_TILE

    @pl.when(i == 0)
    def _():
        kcb_ref[...] = _cmp_keys(abk_ref[0], pebk_ref, w2k_ref, gk_ref, bd_ref).astype(jnp.bfloat16)
        vcb_ref[...] = _cmp_second_layer(abv_ref[0], pebv_ref, w2v_ref).astype(jnp.bfloat16)

    lane = lax.broadcasted_iota(jnp.int32, (tq, LANES), 1)
    t_pos = i * tq + lax.broadcasted_iota(jnp.int32, (tq, LANES), 0)
    dist_i = t_pos - (CMP_STRIDE * lane + CMP_LEN - 1)
    keep = (dist_i >= 0) & (lane < N_CMP_CHUNKS - 1)
    dist = dist_i.astype(jnp.float32)
    own = jnp.right_shift(t_pos, SEL_BLOCK.bit_length() - 1)
    outs = []
    for g in range(NSA_KV_HEADS):
        kchunk, khalf = divmod(g, 2)
        kc_g = kcb_ref[:, kchunk * LANES:(kchunk + 1) * LANES]
        vc_g = vcb_ref[:, kchunk * LANES:(kchunk + 1) * LANES]
        in_half = (lane >= HEAD_DIM) if khalf else (lane < HEAD_DIM)
        psum = None
        for r in range(NSA_GROUP):
            h = g * NSA_GROUP + r
            qc = q_ref[0, :, (h // 2) * LANES:(h // 2 + 1) * LANES]
            if h % 2 != khalf:
                qc = pltpu.roll(qc, HEAD_DIM, 1)
            qb = (jnp.where(in_half, qc, 0.0) * SCALE).astype(jnp.bfloat16)
            s = jnp.where(keep, _nt(qb, kc_g) - slopes_ref[h] * dist, NEG)
            m = jnp.max(s, axis=1, keepdims=True)
            e = jnp.where(keep, jnp.exp(s - m), 0.0)
            l = jnp.sum(e, axis=1, keepdims=True)
            p = e / jnp.where(l > 0.0, l, 1.0)
            o = jnp.dot(p.astype(jnp.bfloat16), vc_g, preferred_element_type=jnp.float32)
            if h % 2 != khalf:
                o = pltpu.roll(o, HEAD_DIM, 1)
            outs.append(o)
            psum = p if psum is None else psum + p
        imp = jnp.dot(psum, ov_ref[...], precision=lax.Precision.HIGHEST, preferred_element_type=jnp.float32)
        score = jnp.where(lane == own, OWN_SCORE, jnp.where(lane < own, imp, NEG))
        mask_ref[0, :, g * LANES:(g + 1) * LANES] = _top_lanes(score, SEL_TOPK)
    for c in range(NSA_HEADS // 2):
        o_ref[0, :, c * LANES:(c + 1) * LANES] = jnp.where(lane < HEAD_DIM, outs[2 * c], outs[2 * c + 1])


def _cmp_consts(cw):
    c_start = np.arange(LANES) * CMP_STRIDE
    j_start = np.arange(LANES) * SEL_BLOCK
    ov = ((c_start[:, None] < j_start[None, :] + SEL_BLOCK) & (c_start[:, None] + CMP_LEN > j_start[None, :])
          & (np.arange(LANES)[:, None] < N_CMP_CHUNKS - 1)).astype(np.float32)
    bd = np.kron(np.eye(NSA_KV_HEADS, dtype=np.float32), np.full((HEAD_DIM, HEAD_DIM), 1.0 / HEAD_DIM, np.float32))
    return [cw['pebk'], cw['pebv'], cw['w2k'], cw['w2v'], cw['gk'], jnp.asarray(bd), jnp.asarray(ov)]


def _cmp_select(q, abk, abv, slopes, cw):
    b, t, _ = q.shape
    tq = ATTN_TILE
    kvw = NSA_KV_WIDTH
    consts = _cmp_consts(cw)
    return pl.pallas_call(
        _cmp_select_kernel,
        grid=(b, t // tq),
        in_specs=[pl.BlockSpec(memory_space=pltpu.SMEM),
                  pl.BlockSpec((1, tq, NSA_WIDTH), lambda bi, i: (bi, i, 0)),
                  pl.BlockSpec((1, N_CMP_CHUNKS, 2 * kvw), lambda bi, i: (bi, 0, 0)),
                  pl.BlockSpec((1, N_CMP_CHUNKS, 2 * kvw), lambda bi, i: (bi, 0, 0))]
                 + [pl.BlockSpec(c.shape, lambda bi, i: (0, 0)) for c in consts],
        out_specs=[pl.BlockSpec((1, tq, NSA_WIDTH), lambda bi, i: (bi, i, 0)),
                   pl.BlockSpec((1, tq, NSA_KV_HEADS * LANES), lambda bi, i: (bi, i, 0))],
        out_shape=[jax.ShapeDtypeStruct((b, t, NSA_WIDTH), jnp.float32),
                   jax.ShapeDtypeStruct((b, t, NSA_KV_HEADS * LANES), jnp.float32)],
        scratch_shapes=[pltpu.VMEM((N_CMP_CHUNKS, kvw), jnp.bfloat16), pltpu.VMEM((N_CMP_CHUNKS, kvw), jnp.bfloat16)],
        compiler_params=pltpu.CompilerParams(dimension_semantics=("arbitrary", "arbitrary")),
        name="cmp_select",
    )(slopes, q, abk, abv, *consts)


def _pages_chunks_kernel(x_ref, o_ref, nat_ref):
    n_col = nat_ref.shape[0] // PAGE_SIZE
    width = n_col * LANES
    for j in range(x_ref.shape[0]):
        for c in range(n_col):
            nat_ref[c * PAGE_SIZE:(c + 1) * PAGE_SIZE, :] = x_ref[j, c * LANES:(c + 1) * LANES, :].T
        for r in range(CMP_STRIDE):
            for c in range(n_col):
                o_ref[j * CHUNKS_PER_PAGE:(j + 1) * CHUNKS_PER_PAGE, r * width + c * LANES:r * width + (c + 1) * LANES] = (
                    nat_ref[pl.ds(c * PAGE_SIZE + r, CHUNKS_PER_PAGE, stride=CMP_STRIDE), :])


def _pages_chunks(cache_t):
    p = cache_t.shape[0]
    per = 8
    width = 2 * NSA_KV_WIDTH
    return pl.pallas_call(
        _pages_chunks_kernel,
        grid=(p // per,),
        in_specs=[pl.BlockSpec((per, width, PAGE_SIZE), lambda i: (i, 0, 0))],
        out_specs=pl.BlockSpec((per * CHUNKS_PER_PAGE, CMP_STRIDE * width), lambda i: (i, 0)),
        out_shape=jax.ShapeDtypeStruct((p * CHUNKS_PER_PAGE, CMP_STRIDE * width), jnp.float32),
        scratch_shapes=[pltpu.VMEM((width // LANES * PAGE_SIZE, LANES), jnp.float32)],
        compiler_params=pltpu.CompilerParams(dimension_semantics=("arbitrary",)),
        name="pages_chunks",
    )(cache_t)


def _one_query_softmax(s, keep, s_own):
    s = jnp.where(keep, s, NEG)
    m = jnp.maximum(jnp.max(s, axis=1, keepdims=True), s_own)
    e = jnp.where(keep, jnp.exp(s - m), 0.0)
    e_own = jnp.exp(s_own - m)
    inv = 1.0 / (jnp.sum(e, axis=1, keepdims=True) + e_own)
    return e * inv, e_own * inv


def _own_score(qb, k_row):
    return jnp.sum(qb.astype(jnp.float32) * k_row.astype(jnp.bfloat16).astype(jnp.float32), axis=1, keepdims=True)


def _nsa_sample_kernel(pt_ref, qb_ref, *refs):
    pps = PAGES_PER_STEP
    page_refs, abk_refs, abv_refs = refs[:pps], refs[pps:2 * pps], refs[2 * pps:3 * pps]
    (win_ref, new_ref, gates_ref, slopes_ref, pebk_ref, pebv_ref, w2k_ref, w2v_ref, gk_ref, bd_ref, ov_ref,
     grp_ref, eslc_ref, hmask_ref, o_ref, s_all, v_all, abk_all, abv_all) = refs[3 * pps:]
    p = pl.program_id(1)
    kvw = NSA_KV_WIDTH
    qb = qb_ref[0].astype(jnp.bfloat16)
    for j in range(pps):
        lane0 = pl.multiple_of((p * pps + j) * PAGE_SIZE, PAGE_SIZE)
        page = page_refs[j][0]
        s_all[:, pl.ds(lane0, PAGE_SIZE)] = jnp.dot(qb, page[:kvw].astype(jnp.bfloat16),
                                                    preferred_element_type=jnp.float32)
        v_all[:, pl.ds(lane0, PAGE_SIZE)] = page[kvw:].astype(jnp.bfloat16)
        c0 = pl.multiple_of((p * pps + j) * CHUNKS_PER_PAGE, CHUNKS_PER_PAGE)
        abk_all[pl.ds(c0, CHUNKS_PER_PAGE), :] = abk_refs[j][0]
        abv_all[pl.ds(c0, CHUNKS_PER_PAGE), :] = abv_refs[j][0]

    @pl.when(p == N_PAGES // pps - 1)
    def _():
        slopes = slopes_ref[...]
        new = new_ref[0]
        lane = lax.broadcasted_iota(jnp.int32, (NSA_HEADS, LANES), 1)

        def fold(full):
            m = full * hmask_ref[...]
            return m[0:4] + m[4:8] + m[8:12] + m[12:16]

        kc = _cmp_keys(abk_all[...], pebk_ref, w2k_ref, gk_ref, bd_ref)
        vc = _cmp_second_layer(abv_all[...], pebv_ref, w2v_ref)
        dist_c = (PAST_LEN - (CMP_LEN - 1) - CMP_STRIDE * lane).astype(jnp.float32)
        keep_c = lane < N_CMP_CHUNKS - 1
        s = jnp.where(keep_c, _nt(qb, kc.astype(jnp.bfloat16)) - slopes * dist_c, NEG)
        e = jnp.where(keep_c, jnp.exp(s - jnp.max(s, axis=1, keepdims=True)), 0.0)
        p_cmp = e / jnp.sum(e, axis=1, keepdims=True)
        o_cmp = jnp.dot(p_cmp.astype(jnp.bfloat16), vc.astype(jnp.bfloat16), preferred_element_type=jnp.float32)

        imp = jnp.dot(p_cmp, ov_ref[...], precision=lax.Precision.HIGHEST, preferred_element_type=jnp.float32)
        imp = jnp.dot(grp_ref[...], imp, precision=lax.Precision.HIGHEST, preferred_element_type=jnp.float32)
        sel = _top_lanes(jnp.where(lane < PAST_LEN // SEL_BLOCK, imp, NEG), SEL_TOPK - 1)
        picked = jnp.dot(sel.astype(jnp.bfloat16), eslc_ref[...], preferred_element_type=jnp.float32) > 0.5
        kpos = lax.broadcasted_iota(jnp.int32, (NSA_HEADS, PAST_LEN), 1)
        s = s_all[...] - slopes * (PAST_LEN - kpos).astype(jnp.float32)
        prob, p_own = _one_query_softmax(s, picked, _own_score(qb, new[:, 0:kvw]))
        o_slc = _nt(prob.astype(jnp.bfloat16), v_all[...]) + p_own * new[:, kvw:2 * kvw]

        win = win_ref[0]
        wpos = lax.broadcasted_iota(jnp.int32, (NSA_HEADS, WINDOW), 1)
        s = (jnp.dot(qb, win[:kvw].astype(jnp.bfloat16), preferred_element_type=jnp.float32)
             - slopes * (WINDOW - wpos).astype(jnp.float32))
        prob, p_own = _one_query_softmax(s, wpos >= 0, _own_score(qb, new[:, 2 * kvw:3 * kvw]))
        o_win = _nt(prob.astype(jnp.bfloat16), win[kvw:].astype(jnp.bfloat16)) + p_own * new[:, 3 * kvw:4 * kvw]

        g = gates_ref[0]
        o_ref[0] = g[0] * fold(o_cmp) + g[1] * fold(o_slc) + g[2] * fold(o_win)


def _moba_sample_kernel(pt_ref, qb_ref, *refs):
    pps = PAGES_PER_STEP
    page_refs = refs[:pps]
    new_ref, slopes_ref, emoba_ref, hmask_ref, o_ref, s_all, v_all, ksum_ref = refs[pps:]
    p = pl.program_id(1)
    w = MOBA_WIDTH
    qf = qb_ref[0]
    qb = qf.astype(jnp.bfloat16)

    @pl.when(p == 0)
    def _():
        ksum_ref[...] = jnp.zeros_like(ksum_ref)

    blk = lax.broadcasted_iota(jnp.int32, ksum_ref.shape, 1)
    for j in range(pps):
        page_no = p * pps + j
        lane0 = pl.multiple_of(page_no * PAGE_SIZE, PAGE_SIZE)
        page = page_refs[j][0]
        k_t = page[:w]
        s_all[:, pl.ds(lane0, PAGE_SIZE)] = jnp.dot(qb, k_t.astype(jnp.bfloat16), preferred_element_type=jnp.float32)
        v_all[:, pl.ds(lane0, PAGE_SIZE)] = page[w:].astype(jnp.bfloat16)
        ksum_ref[...] += jnp.where(blk == page_no // (MOBA_BLOCK // PAGE_SIZE),
                                   jnp.sum(k_t, axis=1, keepdims=True), 0.0)

    @pl.when(p == N_PAGES // pps - 1)
    def _():
        slopes = slopes_ref[...]
        new = new_ref[0]
        lane = lax.broadcasted_iota(jnp.int32, (MOBA_HEADS, LANES), 1)
        gate = jnp.dot(qf, ksum_ref[...], precision=lax.Precision.HIGHEST, preferred_element_type=jnp.float32)
        sel = _top_lanes(jnp.where(lane < PAST_LEN // MOBA_BLOCK, gate, NEG), MOBA_TOPK)
        picked = jnp.dot(sel.astype(jnp.bfloat16), emoba_ref[...], preferred_element_type=jnp.float32) > 0.5
        kpos = lax.broadcasted_iota(jnp.int32, (MOBA_HEADS, PAST_LEN), 1)
        s = s_all[...] - slopes * (PAST_LEN - kpos).astype(jnp.float32)
        prob, p_own = _one_query_softmax(s, picked, _own_score(qb, new[:, :w]))
        full = _nt(prob.astype(jnp.bfloat16), v_all[...]) + p_own * new[:, w:]
        o_ref[0] = jnp.sum(full * hmask_ref[...], axis=0, keepdims=True)


def _head_rows(q, lane_head):
    n = int(max(lane_head)) + 1
    onehot = jnp.asarray(np.eye(n, dtype=np.float32)[np.asarray(lane_head)])
    return jnp.einsum('bhd,hn->bhnd', q, onehot).reshape(q.shape[0], q.shape[1], n * HEAD_DIM)


def _block_expand_t(block):
    e = (np.arange(LANES)[:, None] == (np.arange(PAST_LEN)[None, :] // block)).astype(np.float32)
    return jnp.asarray(e, jnp.bfloat16)


def _const_spec(shape):
    nd = len(shape)
    return pl.BlockSpec(shape, lambda b, p, pt: (0,) * nd)


def _page_index(j, row_block):
    return lambda b, p, pt: (pt[b, p * PAGES_PER_STEP + j], row_block, 0)


def _nsa_sample(page_table, q, cache_t, abk, abv, win_t, new, gates, slopes, cw):
    nseq = q.shape[0]
    kvw = NSA_KV_WIDTH
    pps = PAGES_PER_STEP
    group_of_head = [h // NSA_GROUP for h in range(NSA_HEADS)]
    qb = _head_rows(q * SCALE, group_of_head)
    hmask = _head_rows(jnp.ones((1, NSA_HEADS, HEAD_DIM), jnp.float32), group_of_head)[0]
    hh = np.arange(NSA_HEADS)
    grp = ((hh[:, None] // NSA_GROUP) == (hh[None, :] // NSA_GROUP)).astype(np.float32)
    consts = [slopes.reshape(NSA_HEADS, 1)] + _cmp_consts(cw) + [jnp.asarray(grp), _block_expand_t(SEL_BLOCK), hmask]
    grid_spec = pltpu.PrefetchScalarGridSpec(
        num_scalar_prefetch=1,
        grid=(nseq, N_PAGES // pps),
        in_specs=[pl.BlockSpec((1, NSA_HEADS, kvw), lambda b, p, pt: (b, 0, 0))]
                 + [pl.BlockSpec((1, 2 * kvw, PAGE_SIZE), _page_index(j, 1)) for j in range(pps)]
                 + [pl.BlockSpec((1, CHUNKS_PER_PAGE, 2 * kvw), _page_index(j, 0)) for j in range(pps)] * 2
                 + [pl.BlockSpec((1, 2 * kvw, WINDOW), lambda b, p, pt: (b, 0, 0)),
                    pl.BlockSpec((1, 1, 4 * kvw), lambda b, p, pt: (b, 0, 0)),
                    pl.BlockSpec((1, 3, NSA_GROUP, kvw), lambda b, p, pt: (b, 0, 0, 0))]
                 + [_const_spec(c.shape) for c in consts],
        out_specs=pl.BlockSpec((1, NSA_GROUP, kvw), lambda b, p, pt: (b, 0, 0)),
        scratch_shapes=[pltpu.VMEM((NSA_HEADS, PAST_LEN), jnp.float32),
                        pltpu.VMEM((kvw, PAST_LEN), jnp.bfloat16),
                        pltpu.VMEM((N_CMP_CHUNKS, 2 * kvw), jnp.float32),
                        pltpu.VMEM((N_CMP_CHUNKS, 2 * kvw), jnp.float32)])
    return pl.pallas_call(
        _nsa_sample_kernel,
        grid_spec=grid_spec,
        out_shape=jax.ShapeDtypeStruct((nseq, NSA_GROUP, kvw), jnp.float32),
        compiler_params=pltpu.CompilerParams(dimension_semantics=("arbitrary", "arbitrary")),
        name="nsa_sample",
    )(page_table, qb, *([cache_t] * pps + [abk] * pps + [abv] * pps), win_t, new, gates, *consts)


def _moba_sample(page_table, q, cache_t, new, slopes):
    nseq = q.shape[0]
    w = MOBA_WIDTH
    pps = PAGES_PER_STEP
    own_head = list(range(MOBA_HEADS))
    qb = _head_rows(q * SCALE, own_head)
    hmask = _head_rows(jnp.ones((1, MOBA_HEADS, HEAD_DIM), jnp.float32), own_head)[0]
    consts = [slopes.reshape(MOBA_HEADS, 1), _block_expand_t(MOBA_BLOCK), hmask]
    grid_spec = pltpu.PrefetchScalarGridSpec(
        num_scalar_prefetch=1,
        grid=(nseq, N_PAGES // pps),
        in_specs=[pl.BlockSpec((1, MOBA_HEADS, w), lambda b, p, pt: (b, 0, 0))]
                 + [pl.BlockSpec((1, 2 * w, PAGE_SIZE), _page_index(j, 0)) for j in range(pps)]
                 + [pl.BlockSpec((1, 1, 2 * w), lambda b, p, pt: (b, 0, 0))]
                 + [_const_spec(c.shape) for c in consts],
        out_specs=pl.BlockSpec((1, 1, w), lambda b, p, pt: (b, 0, 0)),
        scratch_shapes=[pltpu.VMEM((MOBA_HEADS, PAST_LEN), jnp.float32),
                        pltpu.VMEM((w, PAST_LEN), jnp.bfloat16),
                        pltpu.VMEM((w, LANES), jnp.float32)])
    return pl.pallas_call(
        _moba_sample_kernel,
        grid_spec=grid_spec,
        out_shape=jax.ShapeDtypeStruct((nseq, 1, w), jnp.float32),
        compiler_params=pltpu.CompilerParams(dimension_semantics=("arbitrary", "arbitrary")),
        name="moba_sample",
    )(page_table, qb, *([cache_t] * pps), new, *consts)


def _rank_top(work, k):
    r_total, width = work.shape
    rowid = lax.broadcasted_iota(jnp.int32, (r_total, width), 0)
    kid = lax.broadcasted_iota(jnp.int32, (k, width), 0)

    def body(r, carry):
        w, vals, rank = carry
        mx = jnp.max(w, axis=0, keepdims=True)
        first = jnp.min(jnp.where(w == mx, rowid, r_total), axis=0, keepdims=True)
        hit = rowid == first
        return jnp.where(hit, F32_MIN, w), jnp.where(kid == r, mx, vals), jnp.where(hit, r.astype(jnp.float32), rank)

    init = (work, jnp.zeros((k, width), jnp.float32), jnp.full((r_total, width), float(k), jnp.float32))
    _, vals, rank = lax.fori_loop(0, k, body, init)
    return vals, rank


def _peer_route_kernel(x_ref, g_ref, wq_ref, subk_ref, xt_ref, c0_ref, e0_ref, r1_ref, e1_ref):
    x = x_ref[...]
    xn = x * lax.rsqrt(jnp.mean(x * x, axis=-1, keepdims=True) + RMS_EPS) * g_ref[...]
    xb = xn.astype(jnp.bfloat16)
    xt_ref[...] = xn.T.astype(jnp.bfloat16)
    t = x.shape[0]
    q = jnp.dot(xb, wq_ref[...], preferred_element_type=jnp.float32)
    half = PEER_QDIM // 2
    k = PEER_TOPK
    tables = [_nt(q[:, j * half:(j + 1) * half], subk_ref[j], precision=lax.Precision.HIGHEST).T
              for j in range(2 * PEER_HEADS)]
    scores = jnp.concatenate(tables, axis=1)
    vals, rank = _rank_top(scores, k)
    cands = []
    for p in range(PEER_HEADS):
        av = vals[:, (2 * p) * t:(2 * p + 1) * t]
        bv = vals[:, (2 * p + 1) * t:(2 * p + 2) * t]
        cands.append(jnp.concatenate([av[a:a + 1, :] + bv for a in range(k)], axis=0))
    best, crank = _rank_top(jnp.concatenate(cands, axis=1), k)
    count = jnp.sum((crank < k).astype(jnp.float32).reshape(k, k, PEER_HEADS * t), axis=1)
    zsum = jnp.sum(jnp.exp(best - best[0:1, :]), axis=0, keepdims=True)
    for p in range(PEER_HEADS):
        s0, s1 = tables[2 * p], tables[2 * p + 1]
        rank0 = rank[:, (2 * p) * t:(2 * p + 1) * t]
        rank1 = rank[:, (2 * p + 1) * t:(2 * p + 2) * t]
        cnt = count[:, p * t:(p + 1) * t]
        c0 = jnp.zeros_like(s0)
        for a in range(k):
            c0 = jnp.where(rank0 == float(a), cnt[a:a + 1, :], c0)
        c0_ref[p] = c0
        e0_ref[p] = jnp.where(rank0 < k, jnp.exp(s0 - vals[0:1, (2 * p) * t:(2 * p + 1) * t]), 0.0) / zsum[:, p * t:(p + 1) * t]
        r1_ref[p] = rank1
        e1_ref[p] = jnp.where(rank1 < k, jnp.exp(s1 - vals[0:1, (2 * p + 1) * t:(2 * p + 2) * t]), 0.0)


def _peer_expert_kernel(x_ref, xt_ref, c0_ref, e0_ref, r1_ref, e1_ref, u_ref, vt_ref, o_ref, acc_ref):
    c = pl.program_id(1)

    @pl.when(c == 0)
    def _():
        acc_ref[...] = jnp.zeros_like(acc_ref)

    sub = 2 * PEER_KEYS
    total = None
    for j in range(PEER_CHUNK // sub):
        ga = jax.nn.gelu(jnp.dot(u_ref[j * sub:(j + 1) * sub, :], xt_ref[...],
                                 preferred_element_type=jnp.float32))
        hs = []
        for k in range(sub // PEER_KEYS):
            i1 = c * (PEER_CHUNK // PEER_KEYS) + j * (sub // PEER_KEYS) + k
            wt = None
            for p in range(PEER_HEADS):
                row_c = c0_ref[p, pl.ds(i1, 1), :]
                row_e = e0_ref[p, pl.ds(i1, 1), :]
                term = jnp.where(r1_ref[p] < row_c, row_e * e1_ref[p], 0.0)
                wt = term if wt is None else wt + term
            hs.append((wt * ga[k * PEER_KEYS:(k + 1) * PEER_KEYS]).astype(jnp.bfloat16))
        part = jnp.dot(vt_ref[:, j * sub:(j + 1) * sub], jnp.concatenate(hs, axis=0),
                       preferred_element_type=jnp.float32)
        total = part if total is None else total + part
    acc_ref[...] += total

    @pl.when(c == pl.num_programs(1) - 1)
    def _():
        o_ref[...] = x_ref[...] + acc_ref[...].T


def _peer_residual(x, g_ffn, wq_b, subk, u_b, vt_b):
    n, d = x.shape
    t1 = 256 if n % 256 == 0 else n
    hp = PEER_HEADS
    tab = jax.ShapeDtypeStruct((hp, PEER_KEYS, n), jnp.float32)
    tab_spec = pl.BlockSpec((hp, PEER_KEYS, t1), lambda i: (0, 0, i))
    xt, c0, e0, r1, e1 = pl.pallas_call(
        _peer_route_kernel,
        grid=(n // t1,),
        in_specs=[pl.BlockSpec((t1, d), lambda i: (i, 0)),
                  pl.BlockSpec((1, d), lambda i: (0, 0)),
                  pl.BlockSpec((d, hp * PEER_QDIM), lambda i: (0, 0)),
                  pl.BlockSpec((2 * hp, PEER_KEYS, PEER_QDIM // 2), lambda i: (0, 0, 0))],
        out_specs=[pl.BlockSpec((d, t1), lambda i: (0, i)), tab_spec, tab_spec, tab_spec, tab_spec],
        out_shape=[jax.ShapeDtypeStruct((d, n), jnp.bfloat16), tab, tab, tab, tab],
        compiler_params=pltpu.CompilerParams(dimension_semantics=("arbitrary",), vmem_limit_bytes=VMEM_LIMIT),
        name="peer_route",
    )(x, g_ffn.reshape(1, d), wq_b, subk.reshape(2 * hp, PEER_KEYS, PEER_QDIM // 2))
    t2 = min(ROW_TILE, n)
    tab_spec2 = pl.BlockSpec((hp, PEER_KEYS, t2), lambda i, c: (0, 0, i))
    return pl.pallas_call(
        _peer_expert_kernel,
        grid=(n // t2, PEER_EXPERTS // PEER_CHUNK),
        in_specs=[pl.BlockSpec((t2, d), lambda i, c: (i, 0)),
                  pl.BlockSpec((d, t2), lambda i, c: (0, i)),
                  tab_spec2, tab_spec2, tab_spec2, tab_spec2,
                  pl.BlockSpec((PEER_CHUNK, d), lambda i, c: (c, 0)),
                  pl.BlockSpec((d, PEER_CHUNK), lambda i, c: (0, c))],
        out_specs=pl.BlockSpec((t2, d), lambda i, c: (i, 0)),
        out_shape=jax.ShapeDtypeStruct((n, d), jnp.float32),
        scratch_shapes=[pltpu.VMEM((d, t2), jnp.float32)],
        compiler_params=pltpu.CompilerParams(dimension_semantics=("arbitrary", "arbitrary"),
                                             vmem_limit_bytes=VMEM_LIMIT),
        name="peer_experts",
    )(x, xt, c0, e0, r1, e1, u_b, vt_b)


def _alibi_slopes():
    n = NSA_HEADS + MOBA_HEADS
    s = jnp.exp2(-8.0 * jnp.arange(1, n + 1, dtype=jnp.float32) / n)
    return s[0::2], s[1::2]


def _in_proj_weights(w_in, gains):
    w_t = jnp.transpose(w_in).astype(jnp.bfloat16)
    kvw = NSA_KV_WIDTH
    edges = np.cumsum([0, NSA_WIDTH] + [kvw] * 6 + [N_GATE] + [MOBA_WIDTH] * 3)
    q_n, kc, vc, ks, vs, kw, vw, gt, q_m, k_m, v_m = [w_t[a:b] for a, b in zip(edges[:-1], edges[1:])]

    def gain(key, n_heads):
        return jnp.tile(gains[key].reshape(HEAD_DIM), n_heads)

    def raw(n):
        return jnp.ones((n,), jnp.float32)

    def pack(parts, gain_parts, flags):
        flag = jnp.concatenate([jnp.full((p.shape[0],), f, jnp.float32) for p, f in zip(parts, flags)])
        return jnp.concatenate(parts, axis=0), jnp.concatenate(gain_parts).reshape(-1, 1), flag.reshape(-1, 1)

    per_step = N_GATE // 2
    pad = jnp.zeros((LANES - per_step, D_MODEL), jnp.bfloat16)
    gt_rows = [gt[:per_step], pad, gt[per_step:], pad]
    return {
        'q_n': pack([q_n], [gain('g_q_nsa', NSA_HEADS)], [1.0]),
        'nsa': pack([kc, vc, ks, vs], [raw(kvw), raw(kvw), gain('g_k_slc', NSA_KV_HEADS), raw(kvw)], [0., 0., 1., 0.]),
        'win': pack([kw, vw], [gain('g_k_win', NSA_KV_HEADS), raw(kvw)], [1., 0.]),
        'cmp': pack([kc, vc], [raw(kvw), raw(kvw)], [0., 0.]),
        'gates': pack(gt_rows, [raw(2 * LANES)], [0., 0., 0., 0.]),
        'q_m': pack([q_m], [gain('g_q_moba', MOBA_HEADS)], [1.0]),
        'moba': pack([k_m, v_m], [gain('g_k_moba', MOBA_HEADS), raw(MOBA_WIDTH)], [1., 0.]),
    }


def _in_proj(x, g_mix, pw, batch):
    h_t = _norm_transpose(x, g_mix)
    out = {}
    for key in ('nsa', 'win', 'moba'):
        out[key] = _proj_t(h_t, *pw[key], batch=batch, rows=512)
    for key in ('q_n', 'q_m', 'cmp'):
        out[key] = _proj_t(h_t, *pw[key], batch=batch, rows=512, natural=True)
    out['gates'] = _proj_t(h_t, *pw['gates'], batch=batch, rows=2 * LANES, natural=True, act='sigmoid')
    return out


def _rows_leaf(rows_t, kinds, heads):
    b, _, t = rows_t.shape
    return jnp.transpose(rows_t.reshape(b, kinds, heads, HEAD_DIM, t), (0, 4, 1, 2, 3))[None]


def kernel(x_prompt, x_sample, cache_nsa, cache_moba, state_win, page_table, p_prompt, p_sample,
           g_mix, w_in, g_q_nsa, g_k_cmp, g_k_slc, g_k_win, g_q_moba, g_k_moba,
           cmp_pe_k, cmp_w1_k, cmp_w2_k, cmp_pe_v, cmp_w1_v, cmp_w2_v,
           g_out_nsa, g_out_moba, w_out, g_ffn, w_peer_q, peer_sub_keys, peer_u, peer_v,
           g_ple, w_ple_gate, w_ple_proj, g_ple_post):
    assert w_in.shape[0] == 1, "single-layer trunk"
    slopes_nsa, slopes_moba = _alibi_slopes()
    kvw = NSA_KV_WIDTH
    bf = jnp.bfloat16
    pw = _in_proj_weights(w_in[0], {'g_q_nsa': g_q_nsa[0], 'g_k_slc': g_k_slc[0], 'g_k_win': g_k_win[0],
                                    'g_q_moba': g_q_moba[0], 'g_k_moba': g_k_moba[0]})
    cw = {}
    w1k_bd, cw['w2k'], cw['pebk'] = _cmp_weights(cmp_w1_k[0], cmp_w2_k[0], cmp_pe_k[0])
    w1v_bd, cw['w2v'], cw['pebv'] = _cmp_weights(cmp_w1_v[0], cmp_w2_v[0], cmp_pe_v[0])
    cw['gk'] = jnp.tile(g_k_cmp[0].reshape(1, HEAD_DIM), (1, NSA_KV_HEADS))
    w_out_b, wq_b, u_b = w_out[0].astype(bf), w_peer_q[0].astype(bf), peer_u[0].astype(bf)
    vt_b = jnp.transpose(peer_v[0]).astype(bf)
    wg_b, wp_b = w_ple_gate[0].astype(bf), w_ple_proj[0].astype(bf)

    def tail(x, o_nsa, o_moba, ple):
        x = _out_proj(o_nsa, o_moba, x, g_out_nsa[0], g_out_moba[0], w_out_b)
        x = _peer_residual(x, g_ffn[0], wq_b, peer_sub_keys[0], u_b, vt_b)
        return _ple(x, ple, g_ple[0], wg_b, wp_b, g_ple_post[0])

    b, t, d = x_prompt.shape
    xp = x_prompt.reshape(b * t, d)
    pr = _in_proj(xp, g_mix[0], pw, b)
    q_n = pr['q_n'].reshape(b, t, NSA_WIDTH)
    abk, abv = _cmp_chunks(pr['cmp'].reshape(b * t // CMP_STRIDE, CMP_STRIDE * 2 * kvw), w1k_bd, w1v_bd)
    o_cmp, mask = _cmp_select(q_n, abk.reshape(b, t // CMP_STRIDE, 2 * kvw), abv.reshape(b, t // CMP_STRIDE, 2 * kvw),
                              slopes_nsa, cw)
    o_slc = _attention(q_n, pr['nsa'], slopes_nsa, mode='mask', heads=NSA_STEP_HEADS, k_row0=4, v_row0=6,
                       mask=mask, block=SEL_BLOCK, n_masks=2)
    o_nsa = _attention(q_n, pr['win'], slopes_nsa, mode='window', heads=NSA_STEP_HEADS, k_row0=0, v_row0=2,
                       combine=(o_cmp, o_slc, pr['gates'].reshape(b, t, 2 * LANES)))
    o_moba = _attention(pr['q_m'].reshape(b, t, MOBA_WIDTH), pr['moba'], slopes_moba, mode='gate',
                        heads=MOBA_STEP_HEADS, k_row0=0, v_row0=MOBA_WIDTH // LANES, block=MOBA_BLOCK)
    y_prompt = tail(xp, o_nsa.reshape(b * t, NSA_WIDTH), o_moba.reshape(b * t, MOBA_WIDTH),
                    p_prompt[0].reshape(b * t, PLE_DIM)).reshape(b, t, d)

    ns = x_sample.shape[0]
    xs = x_sample.reshape(ns, d)
    sr = _in_proj(xs, g_mix[0], pw, 1)
    n_phys = cache_nsa.shape[1]
    cache_n_t = jnp.transpose(cache_nsa[0], (0, 2, 3, 4, 1)).reshape(n_phys, 4 * kvw, PAGE_SIZE)
    cache_m_t = jnp.transpose(cache_moba[0], (0, 2, 3, 4, 1)).reshape(n_phys, 2 * MOBA_WIDTH, PAGE_SIZE)
    win_t = jnp.transpose(state_win[0], (0, 2, 3, 4, 1)).reshape(ns, 2 * kvw, WINDOW)
    abk, abv = _cmp_chunks(_pages_chunks(cache_n_t), w1k_bd, w1v_bd)
    nsa_new = jnp.transpose(sr['nsa'][0])
    win_new = jnp.transpose(sr['win'][0])
    moba_new = jnp.transpose(sr['moba'][0])
    new_n = jnp.concatenate([nsa_new[:, 2 * kvw:], win_new], axis=1)[:, None, :]
    gs = sr['gates']
    gs = jnp.concatenate([gs[:, :N_GATE // 2], gs[:, LANES:LANES + N_GATE // 2]], axis=1)
    g3 = jnp.transpose(gs.reshape(ns, NSA_KV_HEADS, NSA_GROUP, 3), (0, 3, 2, 1))[..., None]
    g3 = jnp.broadcast_to(g3, (ns, 3, NSA_GROUP, NSA_KV_HEADS, HEAD_DIM)).reshape(ns, 3, NSA_GROUP, kvw)
    o_nsa_s = _nsa_sample(page_table, sr['q_n'].reshape(ns, NSA_HEADS, HEAD_DIM), cache_n_t,
                          abk.reshape(n_phys, CHUNKS_PER_PAGE, 2 * kvw), abv.reshape(n_phys, CHUNKS_PER_PAGE, 2 * kvw),
                          win_t, new_n, g3, slopes_nsa, cw)
    o_nsa_s = jnp.transpose(o_nsa_s.reshape(ns, NSA_GROUP, NSA_KV_HEADS, HEAD_DIM), (0, 2, 1, 3)).reshape(ns, NSA_WIDTH)
    o_moba_s = _moba_sample(page_table, sr['q_m'].reshape(ns, MOBA_HEADS, HEAD_DIM), cache_m_t,
                            moba_new[:, None, :], slopes_moba).reshape(ns, MOBA_WIDTH)
    y_sample = tail(xs, o_nsa_s, o_moba_s, p_sample[0].reshape(ns, PLE_DIM)).reshape(ns, 1, d)

    win_prompt = _rows_leaf(pr['win'][:, :, t - WINDOW:], 2, NSA_KV_HEADS)
    nsa_rows_s = nsa_new.reshape(1, ns, 1, 4, NSA_KV_HEADS, HEAD_DIM)
    moba_rows_s = moba_new.reshape(1, ns, 1, 2, MOBA_HEADS, HEAD_DIM)
    win_s = jnp.concatenate([state_win[0][:, 1:], win_new.reshape(ns, 1, 2, NSA_KV_HEADS, HEAD_DIM)], axis=1)[None]
    return (y_prompt, y_sample, _rows_leaf(pr['nsa'], 4, NSA_KV_HEADS), _rows_leaf(pr['moba'], 2, MOBA_HEADS),
            win_prompt, nsa_rows_s, moba_rows_s, win_s)
```

```python
import functools

import jax
import jax.numpy as jnp
from jax import lax
import numpy as np
from jax.experimental import pallas as pl
from jax.experimental.pallas import tpu as pltpu

D_MODEL = 2048
PAST_LEN = 2048
PAGE_SIZE = 128
HEAD_DIM = 64
NSA_HEADS = D_MODEL // (2 * HEAD_DIM)
NSA_KV_HEADS = max(1, NSA_HEADS // 4)
NSA_GROUP = NSA_HEADS // NSA_KV_HEADS
MOBA_HEADS = D_MODEL // (2 * HEAD_DIM)
NSA_WIDTH = NSA_HEADS * HEAD_DIM
NSA_KV_WIDTH = NSA_KV_HEADS * HEAD_DIM
MOBA_WIDTH = MOBA_HEADS * HEAD_DIM
N_GATE = 3 * NSA_HEADS
CMP_LEN = 32
CMP_STRIDE = 16
CMP_HIDDEN = 64
SEL_BLOCK = 64
SEL_TOPK = 8
WINDOW = 512
MOBA_BLOCK = 256
MOBA_TOPK = 3
PEER_KEYS = 128
PEER_EXPERTS = PEER_KEYS * PEER_KEYS
PEER_HEADS = 8
PEER_TOPK = 16
PEER_QDIM = 256
PLE_DIM = 256
RMS_EPS = 1e-6
NEG = -1e30
OWN_SCORE = 1e9
SCALE = HEAD_DIM ** -0.5

LANES = 128
ATTN_TILE = 256
ROW_TILE = 512
PEER_CHUNK = 1024
VMEM_LIMIT = 56 * 1024 * 1024
N_PAGES = PAST_LEN // PAGE_SIZE
PAGES_PER_STEP = 4
N_CMP_CHUNKS = PAST_LEN // CMP_STRIDE
CHUNKS_PER_PAGE = PAGE_SIZE // CMP_STRIDE
F32_MIN = float(np.finfo(np.float32).min)
PEER_CAND_START = tuple(int(v) for v in np.cumsum([0] + [PEER_TOPK // (a + 1) for a in range(PEER_TOPK)]))

MOBA_STEP_HEADS = ((0, 0, 0), (HEAD_DIM, 1, 1))
NSA_STEP_HEADS = tuple((g * NSA_GROUP * HEAD_DIM + r * HEAD_DIM, g, g) for g in range(2) for r in range(NSA_GROUP))


def _nt(a, b, **kw):
    return lax.dot_general(a, b, (((1,), (1,)), ((), ())), preferred_element_type=jnp.float32, **kw)


def _top_lanes(score, k):
    lane = lax.broadcasted_iota(jnp.int32, score.shape, 1)
    sel = jnp.zeros(score.shape, jnp.float32)
    for _ in range(k):
        m = jnp.max(score, axis=1, keepdims=True)
        first = jnp.min(jnp.where(score == m, lane, LANES), axis=1, keepdims=True)
        hit = lane == first
        sel = jnp.where(hit & (m > 0.5 * NEG), 1.0, sel)
        score = jnp.where(hit, F32_MIN, score)
    return sel


def _extract_top(work, k):
    r_total, t = work.shape
    rowid = lax.broadcasted_iota(jnp.int32, (r_total, t), 0)
    kid = lax.broadcasted_iota(jnp.int32, (k, t), 0)

    def body(r, carry):
        w, vals = carry
        mx = jnp.max(w, axis=0, keepdims=True)
        first = jnp.min(jnp.where(w == mx, rowid, r_total), axis=0, keepdims=True)
        w = jnp.where(rowid == first, F32_MIN, w)
        vals = jnp.where(kid == r, mx, vals)
        return w, vals

    w, vals = lax.fori_loop(0, k, body, (work, jnp.zeros((k, t), jnp.float32)))
    return vals, w


def _norm_t_kernel(x_ref, g_ref, o_ref):
    x = x_ref[...]
    xn = x * lax.rsqrt(jnp.mean(x * x, axis=-1, keepdims=True) + RMS_EPS) * g_ref[...]
    o_ref[...] = xn.T.astype(jnp.bfloat16)


def _norm_transpose(x, g):
    n, d = x.shape
    tm = min(ROW_TILE, n)
    return pl.pallas_call(
        _norm_t_kernel,
        grid=(n // tm,),
        in_specs=[pl.BlockSpec((tm, d), lambda i: (i, 0)), pl.BlockSpec((1, d), lambda i: (0, 0))],
        out_specs=pl.BlockSpec((d, tm), lambda i: (0, i)),
        out_shape=jax.ShapeDtypeStruct((d, n), jnp.bfloat16),
        compiler_params=pltpu.CompilerParams(dimension_semantics=("arbitrary",)),
        name="norm_transpose",
    )(x, g.reshape(1, d))


def _proj_t_kernel(act, natural, w_ref, ht_ref, gain_ref, flag_ref, o_ref):
    y = jnp.dot(w_ref[...], ht_ref[...], preferred_element_type=jnp.float32)
    r, tm = y.shape
    y3 = y.reshape(r // HEAD_DIM, HEAD_DIM, tm)
    ms = jnp.mean(y3 * y3, axis=1, keepdims=True)
    yn = (y3 * lax.rsqrt(ms + RMS_EPS)).reshape(r, tm) * gain_ref[...]
    y = jnp.where(flag_ref[...] > 0.5, yn, y)
    if act == 'sigmoid':
        y = jax.nn.sigmoid(y)
    if natural:
        o_ref[...] = y.T
    else:
        o_ref[0] = y


def _proj_t(ht, w_t, gain, flag, *, batch, rows, act=None, natural=False):
    c, d = w_t.shape
    n = ht.shape[1]
    t = n // batch
    tm = min(ROW_TILE, t)
    nt = t // tm
    if natural:
        out_shape = jax.ShapeDtypeStruct((n, c), jnp.float32)
        out_spec = pl.BlockSpec((tm, rows), lambda b, i, j: (b * nt + i, j))
    else:
        out_shape = jax.ShapeDtypeStruct((batch, c, t), jnp.float32)
        out_spec = pl.BlockSpec((1, rows, tm), lambda b, i, j: (b, j, i))
    return pl.pallas_call(
        functools.partial(_proj_t_kernel, act, natural),
        grid=(batch, nt, c // rows),
        in_specs=[pl.BlockSpec((rows, d), lambda b, i, j: (j, 0)),
                  pl.BlockSpec((d, tm), lambda b, i, j: (0, b * nt + i)),
                  pl.BlockSpec((rows, 1), lambda b, i, j: (j, 0)),
                  pl.BlockSpec((rows, 1), lambda b, i, j: (j, 0))],
        out_specs=out_spec,
        out_shape=out_shape,
        compiler_params=pltpu.CompilerParams(dimension_semantics=("arbitrary",) * 3),
        name="proj_t",
    )(w_t, ht, gain, flag)


def _out_proj_kernel(on_ref, om_ref, x_ref, gn_ref, gm_ref, w_ref, o_ref, mix_ref):
    @pl.when(pl.program_id(1) == 0)
    def _():
        for src, g_ref, lo in ((on_ref, gn_ref, 0), (om_ref, gm_ref, NSA_WIDTH)):
            a = src[...]
            an = a * lax.rsqrt(jnp.mean(a * a, axis=-1, keepdims=True) + RMS_EPS) * g_ref[...]
            mix_ref[:, lo:lo + a.shape[1]] = an.astype(jnp.bfloat16)

    o_ref[...] = x_ref[...] + jnp.dot(mix_ref[...], w_ref[...], preferred_element_type=jnp.float32)


def _out_proj(o_nsa, o_moba, x, g_nsa, g_moba, w_b):
    n, d = x.shape
    tm = min(ROW_TILE, n)
    tn = 512
    return pl.pallas_call(
        _out_proj_kernel,
        grid=(n // tm, d // tn),
        in_specs=[pl.BlockSpec((tm, NSA_WIDTH), lambda i, j: (i, 0)),
                  pl.BlockSpec((tm, MOBA_WIDTH), lambda i, j: (i, 0)),
                  pl.BlockSpec((tm, tn), lambda i, j: (i, j)),
                  pl.BlockSpec((1, NSA_WIDTH), lambda i, j: (0, 0)),
                  pl.BlockSpec((1, MOBA_WIDTH), lambda i, j: (0, 0)),
                  pl.BlockSpec((NSA_WIDTH + MOBA_WIDTH, tn), lambda i, j: (0, j))],
        out_specs=pl.BlockSpec((tm, tn), lambda i, j: (i, j)),
        out_shape=jax.ShapeDtypeStruct((n, d), jnp.float32),
        scratch_shapes=[pltpu.VMEM((tm, NSA_WIDTH + MOBA_WIDTH), jnp.bfloat16)],
        compiler_params=pltpu.CompilerParams(dimension_semantics=("arbitrary", "arbitrary")),
        name="out_proj",
    )(o_nsa, o_moba, x, g_nsa.reshape(1, -1), g_moba.reshape(1, -1), w_b)


def _ple_kernel(tn, x_ref, p_ref, g_ref, wg_ref, wp_ref, gp_ref, o_ref, xn_ref, pn_ref):
    j = pl.program_id(1)

    @pl.when(j == 0)
    def _():
        x = x_ref[...]
        xn_ref[...] = (x * lax.rsqrt(jnp.mean(x * x, axis=-1, keepdims=True) + RMS_EPS) * g_ref[...]
                       ).astype(jnp.bfloat16)
        pr = jnp.dot(p_ref[...].astype(jnp.bfloat16), wp_ref[...], preferred_element_type=jnp.float32)
        pn_ref[...] = pr * lax.rsqrt(jnp.mean(pr * pr, axis=-1, keepdims=True) + RMS_EPS) * gp_ref[...]

    col = pl.multiple_of(j * tn, tn)
    gate = jax.nn.sigmoid(jnp.dot(xn_ref[...], wg_ref[...], preferred_element_type=jnp.float32))
    o_ref[...] = x_ref[:, pl.ds(col, tn)] + gate * pn_ref[:, pl.ds(col, tn)]


def _ple(x, ple, g_ple, wg_b, wp_b, g_post):
    n, d = x.shape
    tm = min(ROW_TILE, n)
    tn = 512
    return pl.pallas_call(
        functools.partial(_ple_kernel, tn),
        grid=(n // tm, d // tn),
        in_specs=[pl.BlockSpec((tm, d), lambda i, j: (i, 0)),
                  pl.BlockSpec((tm, PLE_DIM), lambda i, j: (i, 0)),
                  pl.BlockSpec((1, d), lambda i, j: (0, 0)),
                  pl.BlockSpec((d, tn), lambda i, j: (0, j)),
                  pl.BlockSpec((PLE_DIM, d), lambda i, j: (0, 0)),
                  pl.BlockSpec((1, d), lambda i, j: (0, 0))],
        out_specs=pl.BlockSpec((tm, tn), lambda i, j: (i, j)),
        out_shape=jax.ShapeDtypeStruct((n, d), jnp.float32),
        scratch_shapes=[pltpu.VMEM((tm, d), jnp.bfloat16), pltpu.VMEM((tm, d), jnp.float32)],
        compiler_params=pltpu.CompilerParams(dimension_semantics=("arbitrary", "arbitrary")),
        name="ple_gate",
    )(x, ple, g_ple.reshape(1, d), wg_b, wp_b, g_post.reshape(1, d))


def _attn_kernel(heads, mode, *refs):
    if mode == 'mask':
        slopes_ref, q_ref, k_ref, v_ref, mask_ref, expand_ref, o_ref, kb_ref, vb_ref = refs
    elif mode == 'gate':
        slopes_ref, q_ref, k_ref, v_ref, expand_ref, o_ref, kb_ref, vb_ref, means_ref = refs
    else:
        slopes_ref, q_ref, k_ref, v_ref, ocmp_ref, oslc_ref, gates_ref, o_ref, kb_ref, vb_ref = refs
    s_idx = pl.program_id(1)
    i = pl.program_id(2)
    tq = ATTN_TILE
    n_heads = len(heads)
    t_total = k_ref.shape[2]
    lane = lax.broadcasted_iota(jnp.int32, (tq, LANES), 1)

    @pl.when(i == 0)
    def _():
        kb_ref[...] = k_ref[0].astype(jnp.bfloat16)
        vb_ref[...] = v_ref[0].astype(jnp.bfloat16)
        if mode == 'gate':
            sq = lax.broadcasted_iota(jnp.int32, (LANES, LANES), 1)
            means = jnp.zeros((LANES, LANES), jnp.float32)
            for n in range(t_total // MOBA_BLOCK):
                col = jnp.mean(k_ref[0, :, n * MOBA_BLOCK:(n + 1) * MOBA_BLOCK], axis=1, keepdims=True)
                means = jnp.where(sq == n, col, means)
            means_ref[...] = means.T

    sel_b = {}
    if mode == 'mask':
        for m in range(mask_ref.shape[2] // LANES):
            sel_b[m] = mask_ref[0, :, m * LANES:(m + 1) * LANES].astype(jnp.bfloat16)

    row = lax.broadcasted_iota(jnp.int32, (tq, tq), 0)
    col = lax.broadcasted_iota(jnp.int32, (tq, tq), 1)
    rc = (row - col).astype(jnp.float32)
    diag0 = pl.multiple_of(i * tq, tq)

    qbs, slopes = [], []
    for h, (q_off, kv_half, m_idx) in enumerate(heads):
        slopes.append(slopes_ref[s_idx * n_heads + h])
        chunk = q_off // LANES
        qc = q_ref[0, :, chunk * LANES:(chunk + 1) * LANES]
        if (q_off % LANES) // HEAD_DIM != kv_half:
            qc = pltpu.roll(qc, HEAD_DIM, 1)
        in_half = (lane >= HEAD_DIM) if kv_half else (lane < HEAD_DIM)
        qh = jnp.where(in_half, qc, 0.0)
        qbs.append((qh * SCALE).astype(jnp.bfloat16))
        if mode == 'gate':
            n_blk = t_total // MOBA_BLOCK
            gate = _nt(means_ref[...], qh, precision=lax.Precision.HIGHEST)[0:n_blk]
            blk = lax.broadcasted_iota(jnp.int32, (n_blk, tq), 0)
            gate = jnp.where(blk < i, gate, NEG)
            _, taken = _extract_top(gate, MOBA_TOPK)
            sel = ((taken == F32_MIN) & (gate > 0.5 * NEG)) | (blk == i)
            sel = jnp.concatenate([sel.astype(jnp.float32), jnp.zeros((LANES - n_blk, tq), jnp.float32)], axis=0)
            sel_b[m_idx] = sel.T.astype(jnp.bfloat16)

    def tile_step(start, dist, tile_keep, states):
        k_tile = kb_ref[:, pl.ds(start, tq)]
        v_tile = vb_ref[:, pl.ds(start, tq)]
        picked = {m: jnp.dot(sb, expand_ref[:, pl.ds(start, tq)], preferred_element_type=jnp.float32) > 0.5
                  for m, sb in sel_b.items()}
        new_states = []
        for h, (_, _, m_idx) in enumerate(heads):
            keep = picked[m_idx] if mode != 'window' else tile_keep
            if mode != 'window' and tile_keep is not None:
                keep = keep & tile_keep
            s = jnp.dot(qbs[h], k_tile, preferred_element_type=jnp.float32) - slopes[h] * dist
            s = jnp.where(keep, s, NEG)
            m_tile = jnp.max(s, axis=1, keepdims=True)
            if states is None:
                p = jnp.exp(s - m_tile)
                new_states += [m_tile, jnp.sum(p, axis=1, keepdims=True), _nt(p.astype(jnp.bfloat16), v_tile)]
            else:
                m_prev, l_prev, acc = states[3 * h:3 * h + 3]
                m_new = jnp.maximum(m_prev, m_tile)
                alpha = jnp.exp(m_prev - m_new)
                p = jnp.exp(s - m_new)
                new_states += [m_new, alpha * l_prev + jnp.sum(p, axis=1, keepdims=True),
                               alpha * acc + _nt(p.astype(jnp.bfloat16), v_tile)]
        return tuple(new_states)

    states = tile_step(diag0, rc, rc >= 0.0, None)

    def body(n, states):
        start = pl.multiple_of(n * tq, tq)
        dist = rc + ((i - n) * tq).astype(jnp.float32)
        tile_keep = None
        if mode == 'window':
            tile_keep = rc <= jnp.where(n == i - WINDOW // tq, 0.0, float(tq))
        return tile_step(start, dist, tile_keep, states)

    first_tile = jnp.maximum(i - WINDOW // tq, 0) if mode == 'window' else 0
    states = lax.fori_loop(first_tile, i, body, states)
    outs = []
    for h, (q_off, kv_half, _) in enumerate(heads):
        o = states[3 * h + 2] / states[3 * h + 1]
        if (q_off % LANES) // HEAD_DIM != kv_half:
            o = pltpu.roll(o, HEAD_DIM, 1)
        outs.append(o)

    for c in range(n_heads // 2):
        o = jnp.where(lane < HEAD_DIM, outs[2 * c], outs[2 * c + 1])
        if mode == 'window':
            g = gates_ref[0]

            def gate_of(branch):
                lo = g[:, 3 * (2 * c) + branch:3 * (2 * c) + branch + 1]
                hi = g[:, 3 * (2 * c + 1) + branch:3 * (2 * c + 1) + branch + 1]
                return jnp.where(lane < HEAD_DIM, lo, hi)

            sl = slice(c * LANES, (c + 1) * LANES)
            o = gate_of(0) * ocmp_ref[0, :, sl] + gate_of(1) * oslc_ref[0, :, sl] + gate_of(2) * o
        o_ref[0, :, c * LANES:(c + 1) * LANES] = o


def _attention(q, kv_t, slopes, *, mode, heads, k_row0, v_row0, mask=None, block=None, n_masks=0, combine=None):
    b, t, qcols = q.shape
    qw = (len(heads) // 2) * LANES
    n_steps = qcols // qw
    tq = ATTN_TILE
    assert mode != 'gate' or MOBA_BLOCK == tq
    q_spec = pl.BlockSpec((1, tq, qw), lambda bi, s, i: (bi, i, s))
    in_specs = [pl.BlockSpec(memory_space=pltpu.SMEM), q_spec,
                pl.BlockSpec((1, LANES, t), lambda bi, s, i: (bi, k_row0 + s, 0)),
                pl.BlockSpec((1, LANES, t), lambda bi, s, i: (bi, v_row0 + s, 0))]
    args = [slopes, q, kv_t, kv_t]
    scratch = [pltpu.VMEM((LANES, t), jnp.bfloat16), pltpu.VMEM((LANES, t), jnp.bfloat16)]
    if mode == 'mask':
        in_specs.append(pl.BlockSpec((1, tq, n_masks * LANES), lambda bi, s, i: (bi, i, s)))
        args.append(mask)
    if mode == 'window':
        in_specs += [q_spec, q_spec, pl.BlockSpec((1, tq, LANES), lambda bi, s, i: (bi, i, s))]
        args += list(combine)
    else:
        expand = (np.arange(LANES)[:, None] == (np.arange(t)[None, :] // block)).astype(np.float32)
        in_specs.append(pl.BlockSpec((LANES, t), lambda bi, s, i: (0, 0)))
        args.append(jnp.asarray(expand, jnp.bfloat16))
    if mode == 'gate':
        scratch.append(pltpu.VMEM((LANES, LANES), jnp.float32))
    return pl.pallas_call(
        functools.partial(_attn_kernel, heads, mode),
        grid=(b, n_steps, t // tq),
        in_specs=in_specs,
        out_specs=q_spec,
        out_shape=jax.ShapeDtypeStruct((b, t, qcols), jnp.float32),
        scratch_shapes=scratch,
        compiler_params=pltpu.CompilerParams(dimension_semantics=("arbitrary",) * 3),
        name="attention_" + mode,
    )(*args)


def _cmp_chunk_kernel(x_ref, wk_ref, wv_ref, abk_ref, abv_ref):
    @pl.when(pl.program_id(1) == 0)
    def _():
        abk_ref[...] = jnp.zeros_like(abk_ref)
        abv_ref[...] = jnp.zeros_like(abv_ref)

    x = x_ref[...]
    abk_ref[...] += jnp.dot(x[:, :NSA_KV_WIDTH].astype(jnp.bfloat16), wk_ref[0], preferred_element_type=jnp.float32)
    abv_ref[...] += jnp.dot(x[:, NSA_KV_WIDTH:].astype(jnp.bfloat16), wv_ref[0], preferred_element_type=jnp.float32)


def _cmp_chunks(rows, wk_bd, wv_bd):
    n = rows.shape[0]
    tm = next(c for c in (1024, 512, 256, 128) if n % c == 0)
    width = 2 * NSA_KV_WIDTH
    out = jax.ShapeDtypeStruct((n, width), jnp.float32)
    return pl.pallas_call(
        _cmp_chunk_kernel,
        grid=(n // tm, CMP_STRIDE),
        in_specs=[pl.BlockSpec((tm, width), lambda i, r: (i, r)),
                  pl.BlockSpec((1, NSA_KV_WIDTH, width), lambda i, r: (r, 0, 0)),
                  pl.BlockSpec((1, NSA_KV_WIDTH, width), lambda i, r: (r, 0, 0))],
        out_specs=[pl.BlockSpec((tm, width), lambda i, r: (i, 0)), pl.BlockSpec((tm, width), lambda i, r: (i, 0))],
        out_shape=[out, out],
        compiler_params=pltpu.CompilerParams(dimension_semantics=("arbitrary", "arbitrary")),
        name="cmp_chunks",
    )(rows, wk_bd, wv_bd)


def _cmp_weights(w1, w2, pe):
    w1r = w1.reshape(2, CMP_STRIDE, HEAD_DIM, CMP_HIDDEN)
    eye = jnp.eye(NSA_KV_HEADS, dtype=w1.dtype)
    w1_bd = jnp.einsum('gG,ardh->rgdaGh', eye, w1r).reshape(CMP_STRIDE, NSA_KV_WIDTH, 2 * NSA_KV_WIDTH)
    w2_bd = jnp.kron(eye, w2)
    bias = jnp.dot(pe.reshape(1, CMP_LEN * HEAD_DIM), w1, precision=lax.Precision.HIGHEST)
    return w1_bd.astype(jnp.bfloat16), w2_bd.astype(jnp.bfloat16), jnp.tile(bias, (1, NSA_KV_HEADS))


def _cmp_second_layer(ab, peb_ref, w2_ref):
    kvw = NSA_KV_WIDTH
    h = jax.nn.gelu(ab[:, :kvw] + pltpu.roll(ab[:, kvw:], N_CMP_CHUNKS - 1, 0) + peb_ref[...])
    return jnp.dot(h.astype(jnp.bfloat16), w2_ref[...], preferred_element_type=jnp.float32)


def _cmp_keys(ab, peb_ref, w2_ref, gk_ref, bd_ref):
    kc = _cmp_second_layer(ab, peb_ref, w2_ref)
    ms = jnp.dot(kc * kc, bd_ref[...], precision=lax.Precision.HIGHEST, preferred_element_type=jnp.float32)
    return kc * lax.rsqrt(ms + RMS_EPS) * gk_ref[...]


def _cmp_select_kernel(slopes_ref, q_ref, abk_ref, abv_ref, pebk_ref, pebv_ref, w2k_ref, w2v_ref, gk_ref, bd_ref,
                       ov_ref, o_ref, mask_ref, kcb_ref, vcb_ref):
    i = pl.program_id(1)
    tq = ATTN_TILE

    @pl.when(i == 0)
    def _():
        kcb_ref[...] = _cmp_keys(abk_ref[0], pebk_ref, w2k_ref, gk_ref, bd_ref).astype(jnp.bfloat16)
        vcb_ref[...] = _cmp_second_layer(abv_ref[0], pebv_ref, w2v_ref).astype(jnp.bfloat16)

    lane = lax.broadcasted_iota(jnp.int32, (tq, LANES), 1)
    t_pos = i * tq + lax.broadcasted_iota(jnp.int32, (tq, LANES), 0)
    dist_i = t_pos - (CMP_STRIDE * lane + CMP_LEN - 1)
    keep = (dist_i >= 0) & (lane < N_CMP_CHUNKS - 1)
    dist = dist_i.astype(jnp.float32)
    own = jnp.right_shift(t_pos, SEL_BLOCK.bit_length() - 1)
    outs = []
    for g in range(NSA_KV_HEADS):
        kchunk, khalf = divmod(g, 2)
        kc_g = kcb_ref[:, kchunk * LANES:(kchunk + 1) * LANES]
        vc_g = vcb_ref[:, kchunk * LANES:(kchunk + 1) * LANES]
        in_half = (lane >= HEAD_DIM) if khalf else (lane < HEAD_DIM)
        psum = None
        for r in range(NSA_GROUP):
            h = g * NSA_GROUP + r
            qc = q_ref[0, :, (h // 2) * LANES:(h // 2 + 1) * LANES]
            if h % 2 != khalf:
                qc = pltpu.roll(qc, HEAD_DIM, 1)
            qb = (jnp.where(in_half, qc, 0.0) * SCALE).astype(jnp.bfloat16)
            s = jnp.where(keep, _nt(qb, kc_g) - slopes_ref[h] * dist, NEG)
            m = jnp.max(s, axis=1, keepdims=True)
            e = jnp.where(keep, jnp.exp(s - m), 0.0)
            l = jnp.sum(e, axis=1, keepdims=True)
            p = e / jnp.where(l > 0.0, l, 1.0)
            o = jnp.dot(p.astype(jnp.bfloat16), vc_g, preferred_element_type=jnp.float32)
            if h % 2 != khalf:
                o = pltpu.roll(o, HEAD_DIM, 1)
            outs.append(o)
            psum = p if psum is None else psum + p
        imp = jnp.dot(psum, ov_ref[...], precision=lax.Precision.HIGHEST, preferred_element_type=jnp.float32)
        score = jnp.where(lane == own, OWN_SCORE, jnp.where(lane < own, imp, NEG))
        mask_ref[0, :, g * LANES:(g + 1) * LANES] = _top_lanes(score, SEL_TOPK)
    for c in range(NSA_HEADS // 2):
        o_ref[0, :, c * LANES:(c + 1) * LANES] = jnp.where(lane < HEAD_DIM, outs[2 * c], outs[2 * c + 1])


def _cmp_consts(cw):
    c_start = np.arange(LANES) * CMP_STRIDE
    j_start = np.arange(LANES) * SEL_BLOCK
    ov = ((c_start[:, None] < j_start[None, :] + SEL_BLOCK) & (c_start[:, None] + CMP_LEN > j_start[None, :])
          & (np.arange(LANES)[:, None] < N_CMP_CHUNKS - 1)).astype(np.float32)
    bd = np.kron(np.eye(NSA_KV_HEADS, dtype=np.float32), np.full((HEAD_DIM, HEAD_DIM), 1.0 / HEAD_DIM, np.float32))
    return [cw['pebk'], cw['pebv'], cw['w2k'], cw['w2v'], cw['gk'], jnp.asarray(bd), jnp.asarray(ov)]


def _cmp_select(q, abk, abv, slopes, cw):
    b, t, _ = q.shape
    tq = ATTN_TILE
    kvw = NSA_KV_WIDTH
    consts = _cmp_consts(cw)
    return pl.pallas_call(
        _cmp_select_kernel,
        grid=(b, t // tq),
        in_specs=[pl.BlockSpec(memory_space=pltpu.SMEM),
                  pl.BlockSpec((1, tq, NSA_WIDTH), lambda bi, i: (bi, i, 0)),
                  pl.BlockSpec((1, N_CMP_CHUNKS, 2 * kvw), lambda bi, i: (bi, 0, 0)),
                  pl.BlockSpec((1, N_CMP_CHUNKS, 2 * kvw), lambda bi, i: (bi, 0, 0))]
                 + [pl.BlockSpec(c.shape, lambda bi, i: (0, 0)) for c in consts],
        out_specs=[pl.BlockSpec((1, tq, NSA_WIDTH), lambda bi, i: (bi, i, 0)),
                   pl.BlockSpec((1, tq, NSA_KV_HEADS * LANES), lambda bi, i: (bi, i, 0))],
        out_shape=[jax.ShapeDtypeStruct((b, t, NSA_WIDTH), jnp.float32),
                   jax.ShapeDtypeStruct((b, t, NSA_KV_HEADS * LANES), jnp.float32)],
        scratch_shapes=[pltpu.VMEM((N_CMP_CHUNKS, kvw), jnp.bfloat16), pltpu.VMEM((N_CMP_CHUNKS, kvw), jnp.bfloat16)],
        compiler_params=pltpu.CompilerParams(dimension_semantics=("arbitrary", "arbitrary")),
        name="cmp_select",
    )(slopes, q, abk, abv, *consts)


def _pages_chunks_kernel(x_ref, o_ref, nat_ref):
    n_col = nat_ref.shape[0] // PAGE_SIZE
    width = n_col * LANES
    for j in range(x_ref.shape[0]):
        for c in range(n_col):
            nat_ref[c * PAGE_SIZE:(c + 1) * PAGE_SIZE, :] = x_ref[j, c * LANES:(c + 1) * LANES, :].T
        for r in range(CMP_STRIDE):
            for c in range(n_col):
                o_ref[j * CHUNKS_PER_PAGE:(j + 1) * CHUNKS_PER_PAGE, r * width + c * LANES:r * width + (c + 1) * LANES] = (
                    nat_ref[pl.ds(c * PAGE_SIZE + r, CHUNKS_PER_PAGE, stride=CMP_STRIDE), :])


def _pages_chunks(cache_t):
    p = cache_t.shape[0]
    per = 8
    width = 2 * NSA_KV_WIDTH
    return pl.pallas_call(
        _pages_chunks_kernel,
        grid=(p // per,),
        in_specs=[pl.BlockSpec((per, width, PAGE_SIZE), lambda i: (i, 0, 0))],
        out_specs=pl.BlockSpec((per * CHUNKS_PER_PAGE, CMP_STRIDE * width), lambda i: (i, 0)),
        out_shape=jax.ShapeDtypeStruct((p * CHUNKS_PER_PAGE, CMP_STRIDE * width), jnp.float32),
        scratch_shapes=[pltpu.VMEM((width // LANES * PAGE_SIZE, LANES), jnp.float32)],
        compiler_params=pltpu.CompilerParams(dimension_semantics=("arbitrary",)),
        name="pages_chunks",
    )(cache_t)


def _one_query_softmax(s, keep, s_own):
    s = jnp.where(keep, s, NEG)
    m = jnp.maximum(jnp.max(s, axis=1, keepdims=True), s_own)
    e = jnp.where(keep, jnp.exp(s - m), 0.0)
    e_own = jnp.exp(s_own - m)
    inv = 1.0 / (jnp.sum(e, axis=1, keepdims=True) + e_own)
    return e * inv, e_own * inv


def _own_score(qb, k_row):
    return jnp.sum(qb.astype(jnp.float32) * k_row.astype(jnp.bfloat16).astype(jnp.float32), axis=1, keepdims=True)


def _nsa_sample_kernel(pt_ref, qb_ref, *refs):
    pps = PAGES_PER_STEP
    page_refs, abk_refs, abv_refs = refs[:pps], refs[pps:2 * pps], refs[2 * pps:3 * pps]
    (win_ref, new_ref, gates_ref, slopes_ref, pebk_ref, pebv_ref, w2k_ref, w2v_ref, gk_ref, bd_ref, ov_ref,
     grp_ref, eslc_ref, hmask_ref, o_ref, s_all, v_all, abk_all, abv_all) = refs[3 * pps:]
    p = pl.program_id(1)
    kvw = NSA_KV_WIDTH
    qb = qb_ref[0].astype(jnp.bfloat16)
    for j in range(pps):
        lane0 = pl.multiple_of((p * pps + j) * PAGE_SIZE, PAGE_SIZE)
        page = page_refs[j][0]
        s_all[:, pl.ds(lane0, PAGE_SIZE)] = jnp.dot(qb, page[:kvw].astype(jnp.bfloat16),
                                                    preferred_element_type=jnp.float32)
        v_all[:, pl.ds(lane0, PAGE_SIZE)] = page[kvw:].astype(jnp.bfloat16)
        c0 = pl.multiple_of((p * pps + j) * CHUNKS_PER_PAGE, CHUNKS_PER_PAGE)
        abk_all[pl.ds(c0, CHUNKS_PER_PAGE), :] = abk_refs[j][0]
        abv_all[pl.ds(c0, CHUNKS_PER_PAGE), :] = abv_refs[j][0]

    @pl.when(p == N_PAGES // pps - 1)
    def _():
        slopes = slopes_ref[...]
        new = new_ref[0]
        lane = lax.broadcasted_iota(jnp.int32, (NSA_HEADS, LANES), 1)

        def fold(full):
            m = full * hmask_ref[...]
            return m[0:4] + m[4:8] + m[8:12] + m[12:16]

        kc = _cmp_keys(abk_all[...], pebk_ref, w2k_ref, gk_ref, bd_ref)
        vc = _cmp_second_layer(abv_all[...], pebv_ref, w2v_ref)
        dist_c = (PAST_LEN - (CMP_LEN - 1) - CMP_STRIDE * lane).astype(jnp.float32)
        keep_c = lane < N_CMP_CHUNKS - 1
        s = jnp.where(keep_c, _nt(qb, kc.astype(jnp.bfloat16)) - slopes * dist_c, NEG)
        e = jnp.where(keep_c, jnp.exp(s - jnp.max(s, axis=1, keepdims=True)), 0.0)
        p_cmp = e / jnp.sum(e, axis=1, keepdims=True)
        o_cmp = jnp.dot(p_cmp.astype(jnp.bfloat16), vc.astype(jnp.bfloat16), preferred_element_type=jnp.float32)

        imp = jnp.dot(p_cmp, ov_ref[...], precision=lax.Precision.HIGHEST, preferred_element_type=jnp.float32)
        imp = jnp.dot(grp_ref[...], imp, precision=lax.Precision.HIGHEST, preferred_element_type=jnp.float32)
        sel = _top_lanes(jnp.where(lane < PAST_LEN // SEL_BLOCK, imp, NEG), SEL_TOPK - 1)
        picked = jnp.dot(sel.astype(jnp.bfloat16), eslc_ref[...], preferred_element_type=jnp.float32) > 0.5
        kpos = lax.broadcasted_iota(jnp.int32, (NSA_HEADS, PAST_LEN), 1)
        s = s_all[...] - slopes * (PAST_LEN - kpos).astype(jnp.float32)
        prob, p_own = _one_query_softmax(s, picked, _own_score(qb, new[:, 0:kvw]))
        o_slc = _nt(prob.astype(jnp.bfloat16), v_all[...]) + p_own * new[:, kvw:2 * kvw]

        win = win_ref[0]
        wpos = lax.broadcasted_iota(jnp.int32, (NSA_HEADS, WINDOW), 1)
        s = (jnp.dot(qb, win[:kvw].astype(jnp.bfloat16), preferred_element_type=jnp.float32)
             - slopes * (WINDOW - wpos).astype(jnp.float32))
        prob, p_own = _one_query_softmax(s, wpos >= 0, _own_score(qb, new[:, 2 * kvw:3 * kvw]))
        o_win = _nt(prob.astype(jnp.bfloat16), win[kvw:].astype(jnp.bfloat16)) + p_own * new[:, 3 * kvw:4 * kvw]

        g = gates_ref[0]
        o_ref[0] = g[0] * fold(o_cmp) + g[1] * fold(o_slc) + g[2] * fold(o_win)


def _moba_sample_kernel(pt_ref, qb_ref, *refs):
    pps = PAGES_PER_STEP
    page_refs = refs[:pps]
    new_ref, slopes_ref, emoba_ref, hmask_ref, o_ref, s_all, v_all, ksum_ref = refs[pps:]
    p = pl.program_id(1)
    w = MOBA_WIDTH
    qf = qb_ref[0]
    qb = qf.astype(jnp.bfloat16)

    @pl.when(p == 0)
    def _():
        ksum_ref[...] = jnp.zeros_like(ksum_ref)

    blk = lax.broadcasted_iota(jnp.int32, ksum_ref.shape, 1)
    for j in range(pps):
        page_no = p * pps + j
        lane0 = pl.multiple_of(page_no * PAGE_SIZE, PAGE_SIZE)
        page = page_refs[j][0]
        k_t = page[:w]
        s_all[:, pl.ds(lane0, PAGE_SIZE)] = jnp.dot(qb, k_t.astype(jnp.bfloat16), preferred_element_type=jnp.float32)
        v_all[:, pl.ds(lane0, PAGE_SIZE)] = page[w:].astype(jnp.bfloat16)
        ksum_ref[...] += jnp.where(blk == page_no // (MOBA_BLOCK // PAGE_SIZE),
                                   jnp.sum(k_t, axis=1, keepdims=True), 0.0)

    @pl.when(p == N_PAGES // pps - 1)
    def _():
        slopes = slopes_ref[...]
        new = new_ref[0]
        lane = lax.broadcasted_iota(jnp.int32, (MOBA_HEADS, LANES), 1)
        gate = jnp.dot(qf, ksum_ref[...], precision=lax.Precision.HIGHEST, preferred_element_type=jnp.float32)
        sel = _top_lanes(jnp.where(lane < PAST_LEN // MOBA_BLOCK, gate, NEG), MOBA_TOPK)
        picked = jnp.dot(sel.astype(jnp.bfloat16), emoba_ref[...], preferred_element_type=jnp.float32) > 0.5
        kpos = lax.broadcasted_iota(jnp.int32, (MOBA_HEADS, PAST_LEN), 1)
        s = s_all[...] - slopes * (PAST_LEN - kpos).astype(jnp.float32)
        prob, p_own = _one_query_softmax(s, picked, _own_score(qb, new[:, :w]))
        full = _nt(prob.astype(jnp.bfloat16), v_all[...]) + p_own * new[:, w:]
        o_ref[0] = jnp.sum(full * hmask_ref[...], axis=0, keepdims=True)


def _head_rows(q, lane_head):
    n = int(max(lane_head)) + 1
    onehot = jnp.asarray(np.eye(n, dtype=np.float32)[np.asarray(lane_head)])
    return jnp.einsum('bhd,hn->bhnd', q, onehot).reshape(q.shape[0], q.shape[1], n * HEAD_DIM)


def _block_expand_t(block):
    e = (np.arange(LANES)[:, None] == (np.arange(PAST_LEN)[None, :] // block)).astype(np.float32)
    return jnp.asarray(e, jnp.bfloat16)


def _const_spec(shape):
    nd = len(shape)
    return pl.BlockSpec(shape, lambda b, p, pt: (0,) * nd)


def _page_index(j, row_block):
    return lambda b, p, pt: (pt[b, p * PAGES_PER_STEP + j], row_block, 0)


def _nsa_sample(page_table, q, cache_t, abk, abv, win_t, new, gates, slopes, cw):
    nseq = q.shape[0]
    kvw = NSA_KV_WIDTH
    pps = PAGES_PER_STEP
    group_of_head = [h // NSA_GROUP for h in range(NSA_HEADS)]
    qb = _head_rows(q * SCALE, group_of_head)
    hmask = _head_rows(jnp.ones((1, NSA_HEADS, HEAD_DIM), jnp.float32), group_of_head)[0]
    hh = np.arange(NSA_HEADS)
    grp = ((hh[:, None] // NSA_GROUP) == (hh[None, :] // NSA_GROUP)).astype(np.float32)
    consts = [slopes.reshape(NSA_HEADS, 1)] + _cmp_consts(cw) + [jnp.asarray(grp), _block_expand_t(SEL_BLOCK), hmask]
    grid_spec = pltpu.PrefetchScalarGridSpec(
        num_scalar_prefetch=1,
        grid=(nseq, N_PAGES // pps),
        in_specs=[pl.BlockSpec((1, NSA_HEADS, kvw), lambda b, p, pt: (b, 0, 0))]
                 + [pl.BlockSpec((1, 2 * kvw, PAGE_SIZE), _page_index(j, 1)) for j in range(pps)]
                 + [pl.BlockSpec((1, CHUNKS_PER_PAGE, 2 * kvw), _page_index(j, 0)) for j in range(pps)] * 2
                 + [pl.BlockSpec((1, 2 * kvw, WINDOW), lambda b, p, pt: (b, 0, 0)),
                    pl.BlockSpec((1, 1, 4 * kvw), lambda b, p, pt: (b, 0, 0)),
                    pl.BlockSpec((1, 3, NSA_GROUP, kvw), lambda b, p, pt: (b, 0, 0, 0))]
                 + [_const_spec(c.shape) for c in consts],
        out_specs=pl.BlockSpec((1, NSA_GROUP, kvw), lambda b, p, pt: (b, 0, 0)),
        scratch_shapes=[pltpu.VMEM((NSA_HEADS, PAST_LEN), jnp.float32),
                        pltpu.VMEM((kvw, PAST_LEN), jnp.bfloat16),
                        pltpu.VMEM((N_CMP_CHUNKS, 2 * kvw), jnp.float32),
                        pltpu.VMEM((N_CMP_CHUNKS, 2 * kvw), jnp.float32)])
    return pl.pallas_call(
        _nsa_sample_kernel,
        grid_spec=grid_spec,
        out_shape=jax.ShapeDtypeStruct((nseq, NSA_GROUP, kvw), jnp.float32),
        compiler_params=pltpu.CompilerParams(dimension_semantics=("arbitrary", "arbitrary")),
        name="nsa_sample",
    )(page_table, qb, *([cache_t] * pps + [abk] * pps + [abv] * pps), win_t, new, gates, *consts)


def _moba_sample(page_table, q, cache_t, new, slopes):
    nseq = q.shape[0]
    w = MOBA_WIDTH
    pps = PAGES_PER_STEP
    own_head = list(range(MOBA_HEADS))
    qb = _head_rows(q * SCALE, own_head)
    hmask = _head_rows(jnp.ones((1, MOBA_HEADS, HEAD_DIM), jnp.float32), own_head)[0]
    consts = [slopes.reshape(MOBA_HEADS, 1), _block_expand_t(MOBA_BLOCK), hmask]
    grid_spec = pltpu.PrefetchScalarGridSpec(
        num_scalar_prefetch=1,
        grid=(nseq, N_PAGES // pps),
        in_specs=[pl.BlockSpec((1, MOBA_HEADS, w), lambda b, p, pt: (b, 0, 0))]
                 + [pl.BlockSpec((1, 2 * w, PAGE_SIZE), _page_index(j, 0)) for j in range(pps)]
                 + [pl.BlockSpec((1, 1, 2 * w), lambda b, p, pt: (b, 0, 0))]
                 + [_const_spec(c.shape) for c in consts],
        out_specs=pl.BlockSpec((1, 1, w), lambda b, p, pt: (b, 0, 0)),
        scratch_shapes=[pltpu.VMEM((MOBA_HEADS, PAST_LEN), jnp.float32),
                        pltpu.VMEM((w, PAST_LEN), jnp.bfloat16),
                        pltpu.VMEM((w, LANES), jnp.float32)])
    return pl.pallas_call(
        _moba_sample_kernel,
        grid_spec=grid_spec,
        out_shape=jax.ShapeDtypeStruct((nseq, 1, w), jnp.float32),
        compiler_params=pltpu.CompilerParams(dimension_semantics=("arbitrary", "arbitrary")),
        name="moba_sample",
    )(page_table, qb, *([cache_t] * pps), new, *consts)


def _pick_top(s_ref, vals_ref, rows_ref, k):
    r_total, width = s_ref.shape
    cw = min(2 * LANES, width)
    kid = lax.broadcasted_iota(jnp.int32, (k, cw), 0)
    rowid = lax.broadcasted_iota(jnp.int32, (r_total, cw), 0)

    def one_pass(ci, _):
        col = pl.multiple_of(ci * cw, cw)

        def body(r, carry):
            w, vals, rows = carry
            mx = jnp.max(w, axis=0, keepdims=True)
            first = jnp.min(jnp.where(w == mx, rowid, r_total), axis=0, keepdims=True)
            w = jnp.where(rowid == first, F32_MIN, w)
            return w, jnp.where(kid == r, mx, vals), jnp.where(kid == r, first, rows)

        init = (s_ref[:, pl.ds(col, cw)], jnp.zeros((k, cw), jnp.float32), jnp.zeros((k, cw), jnp.int32))
        _, vals, rows = lax.fori_loop(0, k, body, init)
        vals_ref[:, pl.ds(col, cw)] = vals
        rows_ref[:, pl.ds(col, cw)] = rows
        return 0

    lax.fori_loop(0, width // cw, one_pass, 0)
    return vals_ref[...], rows_ref[...]


def _peer_route_kernel(x_ref, g_ref, wq_ref, subk_ref, xt_ref, c0_ref, e0_ref, r1_ref, e1_ref, sc_ref, cand_ref,
                       vals_ref, rows_ref, best_ref, crows_ref):
    x = x_ref[...]
    xn = x * lax.rsqrt(jnp.mean(x * x, axis=-1, keepdims=True) + RMS_EPS) * g_ref[...]
    xb = xn.astype(jnp.bfloat16)
    xt_ref[...] = xn.T.astype(jnp.bfloat16)
    t = x.shape[0]
    q = jnp.dot(xb, wq_ref[...], preferred_element_type=jnp.float32)
    half = PEER_QDIM // 2
    k = PEER_TOPK
    for j in range(2 * PEER_HEADS):
        sc_ref[:, j * t:(j + 1) * t] = _nt(q[:, j * half:(j + 1) * half], subk_ref[j],
                                           precision=lax.Precision.HIGHEST).T
    vals, rows = _pick_top(sc_ref, vals_ref, rows_ref, k)
    for p in range(PEER_HEADS):
        av = vals[:, (2 * p) * t:(2 * p + 1) * t]
        bv = vals[:, (2 * p + 1) * t:(2 * p + 2) * t]
        for a in range(k):
            lo, n_b = PEER_CAND_START[a], PEER_CAND_START[a + 1] - PEER_CAND_START[a]
            cand_ref[lo:lo + n_b, p * t:(p + 1) * t] = av[a:a + 1, :] + bv[0:n_b]
        n_real = PEER_CAND_START[k]
        cand_ref[n_real:, p * t:(p + 1) * t] = jnp.full((cand_ref.shape[0] - n_real, t), F32_MIN, jnp.float32)
    best, crows = _pick_top(cand_ref, best_ref, crows_ref, k)
    zsum = jnp.sum(jnp.exp(best - best[0:1, :]), axis=0, keepdims=True)
    win_a = jnp.zeros(crows.shape, jnp.int32)
    for a in range(1, k):
        win_a = win_a + (crows >= PEER_CAND_START[a]).astype(jnp.int32)
    keyid = lax.broadcasted_iota(jnp.int32, (PEER_KEYS, t), 0)
    for p in range(PEER_HEADS):
        lo0, lo1 = (2 * p) * t, (2 * p + 1) * t
        s0, s1 = sc_ref[:, lo0:lo0 + t], sc_ref[:, lo1:lo1 + t]
        c0 = jnp.zeros((PEER_KEYS, t), jnp.float32)
        r1 = jnp.full((PEER_KEYS, t), float(k), jnp.float32)
        for a in range(k):
            cnt_a = jnp.sum((win_a[:, p * t:(p + 1) * t] == a).astype(jnp.float32), axis=0, keepdims=True)
            c0 = jnp.where(keyid == rows[a:a + 1, lo0:lo0 + t], cnt_a, c0)
            r1 = jnp.where(keyid == rows[a:a + 1, lo1:lo1 + t], float(a), r1)
        c0_ref[p] = c0
        e0_ref[p] = jnp.where(c0 > 0.0, jnp.exp(s0 - vals[0:1, lo0:lo0 + t]), 0.0) / zsum[:, p * t:(p + 1) * t]
        r1_ref[p] = r1
        e1_ref[p] = jnp.where(r1 < k, jnp.exp(s1 - vals[0:1, lo1:lo1 + t]), 0.0)


def _peer_expert_kernel(x_ref, xt_ref, c0_ref, e0_ref, r1_ref, e1_ref, u_ref, vt_ref, o_ref, acc_ref, h_ref):
    c = pl.program_id(1)
    per = PEER_CHUNK // PEER_KEYS

    @pl.when(c == 0)
    def _():
        acc_ref[...] = jnp.zeros_like(acc_ref)

    ga = jax.nn.gelu(jnp.dot(u_ref[...], xt_ref[...], preferred_element_type=jnp.float32))
    for k in range(per):
        i1 = c * per + k
        wt = None
        for p in range(PEER_HEADS):
            row_c = c0_ref[p, pl.ds(i1, 1), :]
            row_e = e0_ref[p, pl.ds(i1, 1), :]
            term = jnp.where(r1_ref[p] < row_c, row_e * e1_ref[p], 0.0)
            wt = term if wt is None else wt + term
        rows = slice(k * PEER_KEYS, (k + 1) * PEER_KEYS)
        h_ref[rows, :] = (wt * ga[rows]).astype(jnp.bfloat16)
    acc_ref[...] += jnp.dot(vt_ref[...], h_ref[...], preferred_element_type=jnp.float32)

    @pl.when(c == pl.num_programs(1) - 1)
    def _():
        o_ref[...] = x_ref[...] + acc_ref[...].T


def _peer_residual(x, g_ffn, wq_b, subk, u_b, vt_b):
    n, d = x.shape
    t1 = 256 if n % 256 == 0 else n
    hp = PEER_HEADS
    tab = jax.ShapeDtypeStruct((hp, PEER_KEYS, n), jnp.float32)
    tab_spec = pl.BlockSpec((hp, PEER_KEYS, t1), lambda i: (0, 0, i))
    xt, c0, e0, r1, e1 = pl.pallas_call(
        _peer_route_kernel,
        grid=(n // t1,),
        in_specs=[pl.BlockSpec((t1, d), lambda i: (i, 0)),
                  pl.BlockSpec((1, d), lambda i: (0, 0)),
                  pl.BlockSpec((d, hp * PEER_QDIM), lambda i: (0, 0)),
                  pl.BlockSpec((2 * hp, PEER_KEYS, PEER_QDIM // 2), lambda i: (0, 0, 0))],
        out_specs=[pl.BlockSpec((d, t1), lambda i: (0, i)), tab_spec, tab_spec, tab_spec, tab_spec],
        out_shape=[jax.ShapeDtypeStruct((d, n), jnp.bfloat16), tab, tab, tab, tab],
        scratch_shapes=[pltpu.VMEM((PEER_KEYS, 2 * hp * t1), jnp.float32),
                        pltpu.VMEM((-(-PEER_CAND_START[PEER_TOPK] // 8) * 8, hp * t1), jnp.float32),
                        pltpu.VMEM((PEER_TOPK, 2 * hp * t1), jnp.float32),
                        pltpu.VMEM((PEER_TOPK, 2 * hp * t1), jnp.int32),
                        pltpu.VMEM((PEER_TOPK, hp * t1), jnp.float32),
                        pltpu.VMEM((PEER_TOPK, hp * t1), jnp.int32)],
        compiler_params=pltpu.CompilerParams(dimension_semantics=("arbitrary",), vmem_limit_bytes=VMEM_LIMIT),
        name="peer_route",
    )(x, g_ffn.reshape(1, d), wq_b, subk.reshape(2 * hp, PEER_KEYS, PEER_QDIM // 2))
    t2 = min(ROW_TILE, n)
    once = pl.Buffered(1)
    tab_spec2 = pl.BlockSpec((hp, PEER_KEYS, t2), lambda i, c: (0, 0, i), pipeline_mode=once)
    return pl.pallas_call(
        _peer_expert_kernel,
        grid=(n // t2, PEER_EXPERTS // PEER_CHUNK),
        in_specs=[pl.BlockSpec((t2, d), lambda i, c: (i, 0), pipeline_mode=once),
                  pl.BlockSpec((d, t2), lambda i, c: (0, i), pipeline_mode=once),
                  tab_spec2, tab_spec2, tab_spec2, tab_spec2,
                  pl.BlockSpec((PEER_CHUNK, d), lambda i, c: (c, 0)),
                  pl.BlockSpec((d, PEER_CHUNK), lambda i, c: (0, c))],
        out_specs=pl.BlockSpec((t2, d), lambda i, c: (i, 0)),
        out_shape=jax.ShapeDtypeStruct((n, d), jnp.float32),
        scratch_shapes=[pltpu.VMEM((d, t2), jnp.float32), pltpu.VMEM((PEER_CHUNK, t2), jnp.bfloat16)],
        compiler_params=pltpu.CompilerParams(dimension_semantics=("arbitrary", "arbitrary"),
                                             vmem_limit_bytes=VMEM_LIMIT),
        name="peer_experts",
    )(x, xt, c0, e0, r1, e1, u_b, vt_b)


def _alibi_slopes():
    n = NSA_HEADS + MOBA_HEADS
    s = jnp.exp2(-8.0 * jnp.arange(1, n + 1, dtype=jnp.float32) / n)
    return s[0::2], s[1::2]


def _in_proj_weights(w_in, gains):
    w_t = jnp.transpose(w_in).astype(jnp.bfloat16)
    kvw = NSA_KV_WIDTH
    edges = np.cumsum([0, NSA_WIDTH] + [kvw] * 6 + [N_GATE] + [MOBA_WIDTH] * 3)
    q_n, kc, vc, ks, vs, kw, vw, gt, q_m, k_m, v_m = [w_t[a:b] for a, b in zip(edges[:-1], edges[1:])]

    def gain(key, n_heads):
        return jnp.tile(gains[key].reshape(HEAD_DIM), n_heads)

    def raw(n):
        return jnp.ones((n,), jnp.float32)

    def pack(parts, gain_parts, flags):
        flag = jnp.concatenate([jnp.full((p.shape[0],), f, jnp.float32) for p, f in zip(parts, flags)])
        return jnp.concatenate(parts, axis=0), jnp.concatenate(gain_parts).reshape(-1, 1), flag.reshape(-1, 1)

    per_step = N_GATE // 2
    pad = jnp.zeros((LANES - per_step, D_MODEL), jnp.bfloat16)
    gt_rows = [gt[:per_step], pad, gt[per_step:], pad]
    return {
        'q_n': pack([q_n], [gain('g_q_nsa', NSA_HEADS)], [1.0]),
        'nsa': pack([kc, vc, ks, vs], [raw(kvw), raw(kvw), gain('g_k_slc', NSA_KV_HEADS), raw(kvw)], [0., 0., 1., 0.]),
        'win': pack([kw, vw], [gain('g_k_win', NSA_KV_HEADS), raw(kvw)], [1., 0.]),
        'cmp': pack([kc, vc], [raw(kvw), raw(kvw)], [0., 0.]),
        'gates': pack(gt_rows, [raw(2 * LANES)], [0., 0., 0., 0.]),
        'q_m': pack([q_m], [gain('g_q_moba', MOBA_HEADS)], [1.0]),
        'moba': pack([k_m, v_m], [gain('g_k_moba', MOBA_HEADS), raw(MOBA_WIDTH)], [1., 0.]),
    }


def _in_proj(x, g_mix, pw, batch):
    h_t = _norm_transpose(x, g_mix)
    out = {}
    for key in ('nsa', 'win', 'moba'):
        out[key] = _proj_t(h_t, *pw[key], batch=batch, rows=512)
    for key in ('q_n', 'q_m', 'cmp'):
        out[key] = _proj_t(h_t, *pw[key], batch=batch, rows=512, natural=True)
    out['gates'] = _proj_t(h_t, *pw['gates'], batch=batch, rows=2 * LANES, natural=True, act='sigmoid')
    return out


def _rows_leaf(rows_t, kinds, heads):
    b, _, t = rows_t.shape
    return jnp.transpose(rows_t.reshape(b, kinds, heads, HEAD_DIM, t), (0, 4, 1, 2, 3))[None]


def kernel(x_prompt, x_sample, cache_nsa, cache_moba, state_win, page_table, p_prompt, p_sample,
           g_mix, w_in, g_q_nsa, g_k_cmp, g_k_slc, g_k_win, g_q_moba, g_k_moba,
           cmp_pe_k, cmp_w1_k, cmp_w2_k, cmp_pe_v, cmp_w1_v, cmp_w2_v,
           g_out_nsa, g_out_moba, w_out, g_ffn, w_peer_q, peer_sub_keys, peer_u, peer_v,
           g_ple, w_ple_gate, w_ple_proj, g_ple_post):
    assert w_in.shape[0] == 1, "single-layer trunk"
    slopes_nsa, slopes_moba = _alibi_slopes()
    kvw = NSA_KV_WIDTH
    bf = jnp.bfloat16
    pw = _in_proj_weights(w_in[0], {'g_q_nsa': g_q_nsa[0], 'g_k_slc': g_k_slc[0], 'g_k_win': g_k_win[0],
                                    'g_q_moba': g_q_moba[0], 'g_k_moba': g_k_moba[0]})
    cw = {}
    w1k_bd, cw['w2k'], cw['pebk'] = _cmp_weights(cmp_w1_k[0], cmp_w2_k[0], cmp_pe_k[0])
    w1v_bd, cw['w2v'], cw['pebv'] = _cmp_weights(cmp_w1_v[0], cmp_w2_v[0], cmp_pe_v[0])
    cw['gk'] = jnp.tile(g_k_cmp[0].reshape(1, HEAD_DIM), (1, NSA_KV_HEADS))
    w_out_b, wq_b, u_b = w_out[0].astype(bf), w_peer_q[0].astype(bf), peer_u[0].astype(bf)
    vt_b = jnp.transpose(peer_v[0]).astype(bf)
    wg_b, wp_b = w_ple_gate[0].astype(bf), w_ple_proj[0].astype(bf)

    def tail(x, o_nsa, o_moba, ple):
        x = _out_proj(o_nsa, o_moba, x, g_out_nsa[0], g_out_moba[0], w_out_b)
        x = _peer_residual(x, g_ffn[0], wq_b, peer_sub_keys[0], u_b, vt_b)
        return _ple(x, ple, g_ple[0], wg_b, wp_b, g_ple_post[0])

    b, t, d = x_prompt.shape
    xp = x_prompt.reshape(b * t, d)
    pr = _in_proj(xp, g_mix[0], pw, b)
    q_n = pr['q_n'].reshape(b, t, NSA_WIDTH)
    abk, abv = _cmp_chunks(pr['cmp'].reshape(b * t // CMP_STRIDE, CMP_STRIDE * 2 * kvw), w1k_bd, w1v_bd)
    o_cmp, mask = _cmp_select(q_n, abk.reshape(b, t // CMP_STRIDE, 2 * kvw), abv.reshape(b, t // CMP_STRIDE, 2 * kvw),
                              slopes_nsa, cw)
    o_slc = _attention(q_n, pr['nsa'], slopes_nsa, mode='mask', heads=NSA_STEP_HEADS, k_row0=4, v_row0=6,
                       mask=mask, block=SEL_BLOCK, n_masks=2)
    o_nsa = _attention(q_n, pr['win'], slopes_nsa, mode='window', heads=NSA_STEP_HEADS, k_row0=0, v_row0=2,
                       combine=(o_cmp, o_slc, pr['gates'].reshape(b, t, 2 * LANES)))
    o_moba = _attention(pr['q_m'].reshape(b, t, MOBA_WIDTH), pr['moba'], slopes_moba, mode='gate',
                        heads=MOBA_STEP_HEADS, k_row0=0, v_row0=MOBA_WIDTH // LANES, block=MOBA_BLOCK)
    y_prompt = tail(xp, o_nsa.reshape(b * t, NSA_WIDTH), o_moba.reshape(b * t, MOBA_WIDTH),
                    p_prompt[0].reshape(b * t, PLE_DIM)).reshape(b, t, d)

    ns = x_sample.shape[0]
    xs = x_sample.reshape(ns, d)
    sr = _in_proj(xs, g_mix[0], pw, 1)
    n_phys = cache_nsa.shape[1]
    cache_n_t = jnp.transpose(cache_nsa[0], (0, 2, 3, 4, 1)).reshape(n_phys, 4 * kvw, PAGE_SIZE)
    cache_m_t = jnp.transpose(cache_moba[0], (0, 2, 3, 4, 1)).reshape(n_phys, 2 * MOBA_WIDTH, PAGE_SIZE)
    win_t = jnp.transpose(state_win[0], (0, 2, 3, 4, 1)).reshape(ns, 2 * kvw, WINDOW)
    abk, abv = _cmp_chunks(_pages_chunks(cache_n_t), w1k_bd, w1v_bd)
    nsa_new = jnp.transpose(sr['nsa'][0])
    win_new = jnp.transpose(sr['win'][0])
    moba_new = jnp.transpose(sr['moba'][0])
    new_n = jnp.concatenate([nsa_new[:, 2 * kvw:], win_new], axis=1)[:, None, :]
    gs = sr['gates']
    gs = jnp.concatenate([gs[:, :N_GATE // 2], gs[:, LANES:LANES + N_GATE // 2]], axis=1)
    g3 = jnp.transpose(gs.reshape(ns, NSA_KV_HEADS, NSA_GROUP, 3), (0, 3, 2, 1))[..., None]
    g3 = jnp.broadcast_to(g3, (ns, 3, NSA_GROUP, NSA_KV_HEADS, HEAD_DIM)).reshape(ns, 3, NSA_GROUP, kvw)
    o_nsa_s = _nsa_sample(page_table, sr['q_n'].reshape(ns, NSA_HEADS, HEAD_DIM), cache_n_t,
                          abk.reshape(n_phys, CHUNKS_PER_PAGE, 2 * kvw), abv.reshape(n_phys, CHUNKS_PER_PAGE, 2 * kvw),
                          win_t, new_n, g3, slopes_nsa, cw)
    o_nsa_s = jnp.transpose(o_nsa_s.reshape(ns, NSA_GROUP, NSA_KV_HEADS, HEAD_DIM), (0, 2, 1, 3)).reshape(ns, NSA_WIDTH)
    o_moba_s = _moba_sample(page_table, sr['q_m'].reshape(ns, MOBA_HEADS, HEAD_DIM), cache_m_t,
                            moba_new[:, None, :], slopes_moba).reshape(ns, MOBA_WIDTH)
    y_sample = tail(xs, o_nsa_s, o_moba_s, p_sample[0].reshape(ns, PLE_DIM)).reshape(ns, 1, d)

    win_prompt = _rows_leaf(pr['win'][:, :, t - WINDOW:], 2, NSA_KV_HEADS)
    nsa_rows_s = nsa_new.reshape(1, ns, 1, 4, NSA_KV_HEADS, HEAD_DIM)
    moba_rows_s = moba_new.reshape(1, ns, 1, 2, MOBA_HEADS, HEAD_DIM)
    win_s = jnp.concatenate([state_win[0][:, 1:], win_new.reshape(ns, 1, 2, NSA_KV_HEADS, HEAD_DIM)], axis=1)[None]
    return (y_prompt, y_sample, _rows_leaf(pr['nsa'], 4, NSA_KV_HEADS), _rows_leaf(pr['moba'], 2, MOBA_HEADS),
            win_prompt, nsa_rows_s, moba_rows_s, win_s)
```

```python
import functools

import jax
import jax.numpy as jnp
from jax import lax
import numpy as np
from jax.experimental import pallas as pl
from jax.experimental.pallas import tpu as pltpu

D_MODEL = 2048
PAST_LEN = 2048
PAGE_SIZE = 128
HEAD_DIM = 64
NSA_HEADS = D_MODEL // (2 * HEAD_DIM)
NSA_KV_HEADS = max(1, NSA_HEADS // 4)
NSA_GROUP = NSA_HEADS // NSA_KV_HEADS
MOBA_HEADS = D_MODEL // (2 * HEAD_DIM)
NSA_WIDTH = NSA_HEADS * HEAD_DIM
NSA_KV_WIDTH = NSA_KV_HEADS * HEAD_DIM
MOBA_WIDTH = MOBA_HEADS * HEAD_DIM
N_GATE = 3 * NSA_HEADS
CMP_LEN = 32
CMP_STRIDE = 16
CMP_HIDDEN = 64
SEL_BLOCK = 64
SEL_TOPK = 8
WINDOW = 512
MOBA_BLOCK = 256
MOBA_TOPK = 3
PEER_KEYS = 128
PEER_EXPERTS = PEER_KEYS * PEER_KEYS
PEER_HEADS = 8
PEER_TOPK = 16
PEER_QDIM = 256
PLE_DIM = 256
RMS_EPS = 1e-6
NEG = -1e30
OWN_SCORE = 1e9
SCALE = HEAD_DIM ** -0.5

LANES = 128
ATTN_TILE = 256
ROW_TILE = 512
PEER_CHUNK = 1024
VMEM_LIMIT = 56 * 1024 * 1024
N_PAGES = PAST_LEN // PAGE_SIZE
PAGES_PER_STEP = 8
N_CMP_CHUNKS = PAST_LEN // CMP_STRIDE
CHUNKS_PER_PAGE = PAGE_SIZE // CMP_STRIDE
F32_MIN = float(np.finfo(np.float32).min)
PEER_CAND_START = tuple(int(v) for v in np.cumsum([0] + [PEER_TOPK // (a + 1) for a in range(PEER_TOPK)]))

MOBA_STEP_HEADS = ((0, 0, 0), (HEAD_DIM, 1, 1))
NSA_STEP_HEADS = tuple((g * NSA_GROUP * HEAD_DIM + r * HEAD_DIM, g, g) for g in range(2) for r in range(NSA_GROUP))


def _nt(a, b, **kw):
    return lax.dot_general(a, b, (((1,), (1,)), ((), ())), preferred_element_type=jnp.float32, **kw)


def _top_lanes(score, k):
    lane = lax.broadcasted_iota(jnp.int32, score.shape, 1)
    sel = jnp.zeros(score.shape, jnp.float32)
    for _ in range(k):
        m = jnp.max(score, axis=1, keepdims=True)
        first = jnp.min(jnp.where(score == m, lane, LANES), axis=1, keepdims=True)
        hit = lane == first
        sel = jnp.where(hit & (m > 0.5 * NEG), 1.0, sel)
        score = jnp.where(hit, F32_MIN, score)
    return sel


def _extract_top(work, k):
    r_total, t = work.shape
    rowid = lax.broadcasted_iota(jnp.int32, (r_total, t), 0)
    kid = lax.broadcasted_iota(jnp.int32, (k, t), 0)

    def body(r, carry):
        w, vals = carry
        mx = jnp.max(w, axis=0, keepdims=True)
        first = jnp.min(jnp.where(w == mx, rowid, r_total), axis=0, keepdims=True)
        w = jnp.where(rowid == first, F32_MIN, w)
        vals = jnp.where(kid == r, mx, vals)
        return w, vals

    w, vals = lax.fori_loop(0, k, body, (work, jnp.zeros((k, t), jnp.float32)))
    return vals, w


def _norm_t_kernel(x_ref, g_ref, o_ref):
    x = x_ref[...]
    xn = x * lax.rsqrt(jnp.mean(x * x, axis=-1, keepdims=True) + RMS_EPS) * g_ref[...]
    o_ref[...] = xn.T.astype(jnp.bfloat16)


def _norm_transpose(x, g):
    n, d = x.shape
    tm = min(ROW_TILE, n)
    return pl.pallas_call(
        _norm_t_kernel,
        grid=(n // tm,),
        in_specs=[pl.BlockSpec((tm, d), lambda i: (i, 0)), pl.BlockSpec((1, d), lambda i: (0, 0))],
        out_specs=pl.BlockSpec((d, tm), lambda i: (0, i)),
        out_shape=jax.ShapeDtypeStruct((d, n), jnp.bfloat16),
        compiler_params=pltpu.CompilerParams(dimension_semantics=("arbitrary",)),
        name="norm_transpose",
    )(x, g.reshape(1, d))


def _proj_t_kernel(act, natural, w_ref, ht_ref, gain_ref, flag_ref, o_ref):
    y = jnp.dot(w_ref[...], ht_ref[...], preferred_element_type=jnp.float32)
    r, tm = y.shape
    y3 = y.reshape(r // HEAD_DIM, HEAD_DIM, tm)
    ms = jnp.mean(y3 * y3, axis=1, keepdims=True)
    yn = (y3 * lax.rsqrt(ms + RMS_EPS)).reshape(r, tm) * gain_ref[...]
    y = jnp.where(flag_ref[...] > 0.5, yn, y)
    if act == 'sigmoid':
        y = jax.nn.sigmoid(y)
    if natural:
        o_ref[...] = y.T
    else:
        o_ref[0] = y


def _proj_t(ht, w_t, gain, flag, *, batch, rows, act=None, natural=False):
    c, d = w_t.shape
    n = ht.shape[1]
    t = n // batch
    tm = min(ROW_TILE, t)
    nt = t // tm
    if natural:
        out_shape = jax.ShapeDtypeStruct((n, c), jnp.float32)
        out_spec = pl.BlockSpec((tm, rows), lambda b, i, j: (b * nt + i, j))
    else:
        out_shape = jax.ShapeDtypeStruct((batch, c, t), jnp.float32)
        out_spec = pl.BlockSpec((1, rows, tm), lambda b, i, j: (b, j, i))
    return pl.pallas_call(
        functools.partial(_proj_t_kernel, act, natural),
        grid=(batch, nt, c // rows),
        in_specs=[pl.BlockSpec((rows, d), lambda b, i, j: (j, 0)),
                  pl.BlockSpec((d, tm), lambda b, i, j: (0, b * nt + i)),
                  pl.BlockSpec((rows, 1), lambda b, i, j: (j, 0)),
                  pl.BlockSpec((rows, 1), lambda b, i, j: (j, 0))],
        out_specs=out_spec,
        out_shape=out_shape,
        compiler_params=pltpu.CompilerParams(dimension_semantics=("arbitrary",) * 3),
        name="proj_t",
    )(w_t, ht, gain, flag)


def _out_proj_kernel(on_ref, om_ref, x_ref, gn_ref, gm_ref, w_ref, o_ref, mix_ref):
    @pl.when(pl.program_id(1) == 0)
    def _():
        for src, g_ref, lo in ((on_ref, gn_ref, 0), (om_ref, gm_ref, NSA_WIDTH)):
            a = src[...]
            an = a * lax.rsqrt(jnp.mean(a * a, axis=-1, keepdims=True) + RMS_EPS) * g_ref[...]
            mix_ref[:, lo:lo + a.shape[1]] = an.astype(jnp.bfloat16)

    o_ref[...] = x_ref[...] + jnp.dot(mix_ref[...], w_ref[...], preferred_element_type=jnp.float32)


def _out_proj(o_nsa, o_moba, x, g_nsa, g_moba, w_b):
    n, d = x.shape
    tm = min(ROW_TILE, n)
    tn = 512
    return pl.pallas_call(
        _out_proj_kernel,
        grid=(n // tm, d // tn),
        in_specs=[pl.BlockSpec((tm, NSA_WIDTH), lambda i, j: (i, 0)),
                  pl.BlockSpec((tm, MOBA_WIDTH), lambda i, j: (i, 0)),
                  pl.BlockSpec((tm, tn), lambda i, j: (i, j)),
                  pl.BlockSpec((1, NSA_WIDTH), lambda i, j: (0, 0)),
                  pl.BlockSpec((1, MOBA_WIDTH), lambda i, j: (0, 0)),
                  pl.BlockSpec((NSA_WIDTH + MOBA_WIDTH, tn), lambda i, j: (0, j))],
        out_specs=pl.BlockSpec((tm, tn), lambda i, j: (i, j)),
        out_shape=jax.ShapeDtypeStruct((n, d), jnp.float32),
        scratch_shapes=[pltpu.VMEM((tm, NSA_WIDTH + MOBA_WIDTH), jnp.bfloat16)],
        compiler_params=pltpu.CompilerParams(dimension_semantics=("arbitrary", "arbitrary")),
        name="out_proj",
    )(o_nsa, o_moba, x, g_nsa.reshape(1, -1), g_moba.reshape(1, -1), w_b)


def _ple_kernel(tn, x_ref, p_ref, g_ref, wg_ref, wp_ref, gp_ref, o_ref, xn_ref, pn_ref):
    j = pl.program_id(1)

    @pl.when(j == 0)
    def _():
        x = x_ref[...]
        xn_ref[...] = (x * lax.rsqrt(jnp.mean(x * x, axis=-1, keepdims=True) + RMS_EPS) * g_ref[...]
                       ).astype(jnp.bfloat16)
        pr = jnp.dot(p_ref[...].astype(jnp.bfloat16), wp_ref[...], preferred_element_type=jnp.float32)
        pn_ref[...] = pr * lax.rsqrt(jnp.mean(pr * pr, axis=-1, keepdims=True) + RMS_EPS) * gp_ref[...]

    col = pl.multiple_of(j * tn, tn)
    gate = jax.nn.sigmoid(jnp.dot(xn_ref[...], wg_ref[...], preferred_element_type=jnp.float32))
    o_ref[...] = x_ref[:, pl.ds(col, tn)] + gate * pn_ref[:, pl.ds(col, tn)]


def _ple(x, ple, g_ple, wg_b, wp_b, g_post):
    n, d = x.shape
    tm = min(ROW_TILE, n)
    tn = 512
    return pl.pallas_call(
        functools.partial(_ple_kernel, tn),
        grid=(n // tm, d // tn),
        in_specs=[pl.BlockSpec((tm, d), lambda i, j: (i, 0)),
                  pl.BlockSpec((tm, PLE_DIM), lambda i, j: (i, 0)),
                  pl.BlockSpec((1, d), lambda i, j: (0, 0)),
                  pl.BlockSpec((d, tn), lambda i, j: (0, j)),
                  pl.BlockSpec((PLE_DIM, d), lambda i, j: (0, 0)),
                  pl.BlockSpec((1, d), lambda i, j: (0, 0))],
        out_specs=pl.BlockSpec((tm, tn), lambda i, j: (i, j)),
        out_shape=jax.ShapeDtypeStruct((n, d), jnp.float32),
        scratch_shapes=[pltpu.VMEM((tm, d), jnp.bfloat16), pltpu.VMEM((tm, d), jnp.float32)],
        compiler_params=pltpu.CompilerParams(dimension_semantics=("arbitrary", "arbitrary")),
        name="ple_gate",
    )(x, ple, g_ple.reshape(1, d), wg_b, wp_b, g_post.reshape(1, d))


def _attn_kernel(heads, mode, *refs):
    if mode == 'mask':
        slopes_ref, q_ref, k_ref, v_ref, mask_ref, expand_ref, o_ref, kb_ref, vb_ref = refs
    elif mode == 'gate':
        slopes_ref, q_ref, k_ref, v_ref, expand_ref, o_ref, kb_ref, vb_ref, means_ref = refs
    else:
        slopes_ref, q_ref, k_ref, v_ref, ocmp_ref, oslc_ref, gates_ref, o_ref, kb_ref, vb_ref = refs
    s_idx = pl.program_id(1)
    i = pl.program_id(2)
    tq = ATTN_TILE
    n_heads = len(heads)
    t_total = k_ref.shape[2]
    lane = lax.broadcasted_iota(jnp.int32, (tq, LANES), 1)

    @pl.when(i == 0)
    def _():
        kb_ref[...] = k_ref[0].astype(jnp.bfloat16)
        vb_ref[...] = v_ref[0].astype(jnp.bfloat16)
        if mode == 'gate':
            sq = lax.broadcasted_iota(jnp.int32, (LANES, LANES), 1)
            means = jnp.zeros((LANES, LANES), jnp.float32)
            for n in range(t_total // MOBA_BLOCK):
                col = jnp.mean(k_ref[0, :, n * MOBA_BLOCK:(n + 1) * MOBA_BLOCK], axis=1, keepdims=True)
                means = jnp.where(sq == n, col, means)
            means_ref[...] = means.T

    sel_b = {}
    if mode == 'mask':
        for m in range(mask_ref.shape[2] // LANES):
            sel_b[m] = mask_ref[0, :, m * LANES:(m + 1) * LANES].astype(jnp.bfloat16)

    row = lax.broadcasted_iota(jnp.int32, (tq, tq), 0)
    col = lax.broadcasted_iota(jnp.int32, (tq, tq), 1)
    rc = (row - col).astype(jnp.float32)
    diag0 = pl.multiple_of(i * tq, tq)

    qbs, slopes = [], []
    for h, (q_off, kv_half, m_idx) in enumerate(heads):
        slopes.append(slopes_ref[s_idx * n_heads + h])
        chunk = q_off // LANES
        qc = q_ref[0, :, chunk * LANES:(chunk + 1) * LANES]
        if (q_off % LANES) // HEAD_DIM != kv_half:
            qc = pltpu.roll(qc, HEAD_DIM, 1)
        in_half = (lane >= HEAD_DIM) if kv_half else (lane < HEAD_DIM)
        qh = jnp.where(in_half, qc, 0.0)
        qbs.append((qh * SCALE).astype(jnp.bfloat16))
        if mode == 'gate':
            n_blk = t_total // MOBA_BLOCK
            gate = _nt(means_ref[...], qh, precision=lax.Precision.HIGHEST)[0:n_blk]
            blk = lax.broadcasted_iota(jnp.int32, (n_blk, tq), 0)
            gate = jnp.where(blk < i, gate, NEG)
            _, taken = _extract_top(gate, MOBA_TOPK)
            sel = ((taken == F32_MIN) & (gate > 0.5 * NEG)) | (blk == i)
            sel = jnp.concatenate([sel.astype(jnp.float32), jnp.zeros((LANES - n_blk, tq), jnp.float32)], axis=0)
            sel_b[m_idx] = sel.T.astype(jnp.bfloat16)

    def tile_step(start, dist, tile_keep, states):
        k_tile = kb_ref[:, pl.ds(start, tq)]
        v_tile = vb_ref[:, pl.ds(start, tq)]
        picked = {m: jnp.dot(sb, expand_ref[:, pl.ds(start, tq)], preferred_element_type=jnp.float32) > 0.5
                  for m, sb in sel_b.items()}
        new_states = []
        for h, (_, _, m_idx) in enumerate(heads):
            keep = picked[m_idx] if mode != 'window' else tile_keep
            if mode != 'window' and tile_keep is not None:
                keep = keep & tile_keep
            s = jnp.dot(qbs[h], k_tile, preferred_element_type=jnp.float32) - slopes[h] * dist
            s = jnp.where(keep, s, NEG)
            m_tile = jnp.max(s, axis=1, keepdims=True)
            if states is None:
                p = jnp.exp(s - m_tile)
                new_states += [m_tile, jnp.sum(p, axis=1, keepdims=True), _nt(p.astype(jnp.bfloat16), v_tile)]
            else:
                m_prev, l_prev, acc = states[3 * h:3 * h + 3]
                m_new = jnp.maximum(m_prev, m_tile)
                alpha = jnp.exp(m_prev - m_new)
                p = jnp.exp(s - m_new)
                new_states += [m_new, alpha * l_prev + jnp.sum(p, axis=1, keepdims=True),
                               alpha * acc + _nt(p.astype(jnp.bfloat16), v_tile)]
        return tuple(new_states)

    states = tile_step(diag0, rc, rc >= 0.0, None)

    def body(n, states):
        start = pl.multiple_of(n * tq, tq)
        dist = rc + ((i - n) * tq).astype(jnp.float32)
        tile_keep = None
        if mode == 'window':
            tile_keep = rc <= jnp.where(n == i - WINDOW // tq, 0.0, float(tq))
        return tile_step(start, dist, tile_keep, states)

    first_tile = jnp.maximum(i - WINDOW // tq, 0) if mode == 'window' else 0
    states = lax.fori_loop(first_tile, i, body, states)
    outs = []
    for h, (q_off, kv_half, _) in enumerate(heads):
        o = states[3 * h + 2] / states[3 * h + 1]
        if (q_off % LANES) // HEAD_DIM != kv_half:
            o = pltpu.roll(o, HEAD_DIM, 1)
        outs.append(o)

    for c in range(n_heads // 2):
        o = jnp.where(lane < HEAD_DIM, outs[2 * c], outs[2 * c + 1])
        if mode == 'window':
            g = gates_ref[0]

            def gate_of(branch):
                lo = g[:, 3 * (2 * c) + branch:3 * (2 * c) + branch + 1]
                hi = g[:, 3 * (2 * c + 1) + branch:3 * (2 * c + 1) + branch + 1]
                return jnp.where(lane < HEAD_DIM, lo, hi)

            sl = slice(c * LANES, (c + 1) * LANES)
            o = gate_of(0) * ocmp_ref[0, :, sl] + gate_of(1) * oslc_ref[0, :, sl] + gate_of(2) * o
        o_ref[0, :, c * LANES:(c + 1) * LANES] = o


def _attention(q, kv_t, slopes, *, mode, heads, k_row0, v_row0, mask=None, block=None, n_masks=0, combine=None):
    b, t, qcols = q.shape
    qw = (len(heads) // 2) * LANES
    n_steps = qcols // qw
    tq = ATTN_TILE
    assert mode != 'gate' or MOBA_BLOCK == tq
    q_spec = pl.BlockSpec((1, tq, qw), lambda bi, s, i: (bi, i, s))
    in_specs = [pl.BlockSpec(memory_space=pltpu.SMEM), q_spec,
                pl.BlockSpec((1, LANES, t), lambda bi, s, i: (bi, k_row0 + s, 0)),
                pl.BlockSpec((1, LANES, t), lambda bi, s, i: (bi, v_row0 + s, 0))]
    args = [slopes, q, kv_t, kv_t]
    scratch = [pltpu.VMEM((LANES, t), jnp.bfloat16), pltpu.VMEM((LANES, t), jnp.bfloat16)]
    if mode == 'mask':
        in_specs.append(pl.BlockSpec((1, tq, n_masks * LANES), lambda bi, s, i: (bi, i, s)))
        args.append(mask)
    if mode == 'window':
        in_specs += [q_spec, q_spec, pl.BlockSpec((1, tq, LANES), lambda bi, s, i: (bi, i, s))]
        args += list(combine)
    else:
        expand = (np.arange(LANES)[:, None] == (np.arange(t)[None, :] // block)).astype(np.float32)
        in_specs.append(pl.BlockSpec((LANES, t), lambda bi, s, i: (0, 0)))
        args.append(jnp.asarray(expand, jnp.bfloat16))
    if mode == 'gate':
        scratch.append(pltpu.VMEM((LANES, LANES), jnp.float32))
    return pl.pallas_call(
        functools.partial(_attn_kernel, heads, mode),
        grid=(b, n_steps, t // tq),
        in_specs=in_specs,
        out_specs=q_spec,
        out_shape=jax.ShapeDtypeStruct((b, t, qcols), jnp.float32),
        scratch_shapes=scratch,
        compiler_params=pltpu.CompilerParams(dimension_semantics=("arbitrary",) * 3),
        name="attention_" + mode,
    )(*args)


def _cmp_chunk_kernel(x_ref, wk_ref, wv_ref, abk_ref, abv_ref):
    @pl.when(pl.program_id(1) == 0)
    def _():
        abk_ref[...] = jnp.zeros_like(abk_ref)
        abv_ref[...] = jnp.zeros_like(abv_ref)

    x = x_ref[...]
    abk_ref[...] += jnp.dot(x[:, :NSA_KV_WIDTH].astype(jnp.bfloat16), wk_ref[0], preferred_element_type=jnp.float32)
    abv_ref[...] += jnp.dot(x[:, NSA_KV_WIDTH:].astype(jnp.bfloat16), wv_ref[0], preferred_element_type=jnp.float32)


def _cmp_chunks(rows, wk_bd, wv_bd):
    n = rows.shape[0]
    tm = next(c for c in (1024, 512, 256, 128) if n % c == 0)
    width = 2 * NSA_KV_WIDTH
    out = jax.ShapeDtypeStruct((n, width), jnp.float32)
    return pl.pallas_call(
        _cmp_chunk_kernel,
        grid=(n // tm, CMP_STRIDE),
        in_specs=[pl.BlockSpec((tm, width), lambda i, r: (i, r)),
                  pl.BlockSpec((1, NSA_KV_WIDTH, width), lambda i, r: (r, 0, 0)),
                  pl.BlockSpec((1, NSA_KV_WIDTH, width), lambda i, r: (r, 0, 0))],
        out_specs=[pl.BlockSpec((tm, width), lambda i, r: (i, 0)), pl.BlockSpec((tm, width), lambda i, r: (i, 0))],
        out_shape=[out, out],
        compiler_params=pltpu.CompilerParams(dimension_semantics=("arbitrary", "arbitrary")),
        name="cmp_chunks",
    )(rows, wk_bd, wv_bd)


def _cmp_weights(w1, w2, pe):
    w1r = w1.reshape(2, CMP_STRIDE, HEAD_DIM, CMP_HIDDEN)
    eye = jnp.eye(NSA_KV_HEADS, dtype=w1.dtype)
    w1_bd = jnp.einsum('gG,ardh->rgdaGh', eye, w1r).reshape(CMP_STRIDE, NSA_KV_WIDTH, 2 * NSA_KV_WIDTH)
    w2_bd = jnp.kron(eye, w2)
    bias = jnp.dot(pe.reshape(1, CMP_LEN * HEAD_DIM), w1, precision=lax.Precision.HIGHEST)
    return w1_bd.astype(jnp.bfloat16), w2_bd.astype(jnp.bfloat16), jnp.tile(bias, (1, NSA_KV_HEADS))


def _cmp_second_layer(ab, peb_ref, w2_ref):
    kvw = NSA_KV_WIDTH
    h = jax.nn.gelu(ab[:, :kvw] + pltpu.roll(ab[:, kvw:], N_CMP_CHUNKS - 1, 0) + peb_ref[...])
    return jnp.dot(h.astype(jnp.bfloat16), w2_ref[...], preferred_element_type=jnp.float32)


def _cmp_keys(ab, peb_ref, w2_ref, gk_ref, bd_ref):
    kc = _cmp_second_layer(ab, peb_ref, w2_ref)
    ms = jnp.dot(kc * kc, bd_ref[...], precision=lax.Precision.HIGHEST, preferred_element_type=jnp.float32)
    return kc * lax.rsqrt(ms + RMS_EPS) * gk_ref[...]


def _cmp_select_kernel(slopes_ref, q_ref, abk_ref, abv_ref, pebk_ref, pebv_ref, w2k_ref, w2v_ref, gk_ref, bd_ref,
                       ov_ref, o_ref, mask_ref, kcb_ref, vcb_ref):
    i = pl.program_id(1)
    tq = ATTN_TILE

    @pl.when(i == 0)
    def _():
        kcb_ref[...] = _cmp_keys(abk_ref[0], pebk_ref, w2k_ref, gk_ref, bd_ref).astype(jnp.bfloat16)
        vcb_ref[...] = _cmp_second_layer(abv_ref[0], pebv_ref, w2v_ref).astype(jnp.bfloat16)

    lane = lax.broadcasted_iota(jnp.int32, (tq, LANES), 1)
    t_pos = i * tq + lax.broadcasted_iota(jnp.int32, (tq, LANES), 0)
    dist_i = t_pos - (CMP_STRIDE * lane + CMP_LEN - 1)
    keep = (dist_i >= 0) & (lane < N_CMP_CHUNKS - 1)
    dist = dist_i.astype(jnp.float32)
    own = jnp.right_shift(t_pos, SEL_BLOCK.bit_length() - 1)
    outs = []
    for g in range(NSA_KV_HEADS):
        kchunk, khalf = divmod(g, 2)
        kc_g = kcb_ref[:, kchunk * LANES:(kchunk + 1) * LANES]
        vc_g = vcb_ref[:, kchunk * LANES:(kchunk + 1) * LANES]
        in_half = (lane >= HEAD_DIM) if khalf else (lane < HEAD_DIM)
        psum = None
        for r in range(NSA_GROUP):
            h = g * NSA_GROUP + r
            qc = q_ref[0, :, (h // 2) * LANES:(h // 2 + 1) * LANES]
            if h % 2 != khalf:
                qc = pltpu.roll(qc, HEAD_DIM, 1)
            qb = (jnp.where(in_half, qc, 0.0) * SCALE).astype(jnp.bfloat16)
            s = jnp.where(keep, _nt(qb, kc_g) - slopes_ref[h] * dist, NEG)
            m = jnp.max(s, axis=1, keepdims=True)
            e = jnp.where(keep, jnp.exp(s - m), 0.0)
            l = jnp.sum(e, axis=1, keepdims=True)
            p = e / jnp.where(l > 0.0, l, 1.0)
            o = jnp.dot(p.astype(jnp.bfloat16), vc_g, preferred_element_type=jnp.float32)
            if h % 2 != khalf:
                o = pltpu.roll(o, HEAD_DIM, 1)
            outs.append(o)
            psum = p if psum is None else psum + p
        imp = jnp.dot(psum, ov_ref[...], precision=lax.Precision.HIGHEST, preferred_element_type=jnp.float32)
        score = jnp.where(lane == own, OWN_SCORE, jnp.where(lane < own, imp, NEG))
        mask_ref[0, :, g * LANES:(g + 1) * LANES] = _top_lanes(score, SEL_TOPK)
    for c in range(NSA_HEADS // 2):
        o_ref[0, :, c * LANES:(c + 1) * LANES] = jnp.where(lane < HEAD_DIM, outs[2 * c], outs[2 * c + 1])


def _cmp_consts(cw):
    c_start = np.arange(LANES) * CMP_STRIDE
    j_start = np.arange(LANES) * SEL_BLOCK
    ov = ((c_start[:, None] < j_start[None, :] + SEL_BLOCK) & (c_start[:, None] + CMP_LEN > j_start[None, :])
          & (np.arange(LANES)[:, None] < N_CMP_CHUNKS - 1)).astype(np.float32)
    bd = np.kron(np.eye(NSA_KV_HEADS, dtype=np.float32), np.full((HEAD_DIM, HEAD_DIM), 1.0 / HEAD_DIM, np.float32))
    return [cw['pebk'], cw['pebv'], cw['w2k'], cw['w2v'], cw['gk'], jnp.asarray(bd), jnp.asarray(ov)]


def _cmp_select(q, abk, abv, slopes, cw):
    b, t, _ = q.shape
    tq = ATTN_TILE
    kvw = NSA_KV_WIDTH
    consts = _cmp_consts(cw)
    return pl.pallas_call(
        _cmp_select_kernel,
        grid=(b, t // tq),
        in_specs=[pl.BlockSpec(memory_space=pltpu.SMEM),
                  pl.BlockSpec((1, tq, NSA_WIDTH), lambda bi, i: (bi, i, 0)),
                  pl.BlockSpec((1, N_CMP_CHUNKS, 2 * kvw), lambda bi, i: (bi, 0, 0)),
                  pl.BlockSpec((1, N_CMP_CHUNKS, 2 * kvw), lambda bi, i: (bi, 0, 0))]
                 + [pl.BlockSpec(c.shape, lambda bi, i: (0, 0)) for c in consts],
        out_specs=[pl.BlockSpec((1, tq, NSA_WIDTH), lambda bi, i: (bi, i, 0)),
                   pl.BlockSpec((1, tq, NSA_KV_HEADS * LANES), lambda bi, i: (bi, i, 0))],
        out_shape=[jax.ShapeDtypeStruct((b, t, NSA_WIDTH), jnp.float32),
                   jax.ShapeDtypeStruct((b, t, NSA_KV_HEADS * LANES), jnp.float32)],
        scratch_shapes=[pltpu.VMEM((N_CMP_CHUNKS, kvw), jnp.bfloat16), pltpu.VMEM((N_CMP_CHUNKS, kvw), jnp.bfloat16)],
        compiler_params=pltpu.CompilerParams(dimension_semantics=("arbitrary", "arbitrary")),
        name="cmp_select",
    )(slopes, q, abk, abv, *consts)


def _pages_chunks_kernel(pt_ref, *refs):
    x_refs, (o_ref, nat_ref) = refs[:-2], refs[-2:]
    n_col = nat_ref.shape[0] // PAGE_SIZE
    width = n_col * LANES
    for j, x_ref in enumerate(x_refs):
        for c in range(n_col):
            nat_ref[c * PAGE_SIZE:(c + 1) * PAGE_SIZE, :] = x_ref[0, c * LANES:(c + 1) * LANES, :].T
        for r in range(CMP_STRIDE):
            for c in range(n_col):
                o_ref[j * CHUNKS_PER_PAGE:(j + 1) * CHUNKS_PER_PAGE, r * width + c * LANES:r * width + (c + 1) * LANES] = (
                    nat_ref[pl.ds(c * PAGE_SIZE + r, CHUNKS_PER_PAGE, stride=CMP_STRIDE), :])


def _pages_chunks(page_table, cache_t):
    nseq = page_table.shape[0]
    per = 8
    width = 2 * NSA_KV_WIDTH
    groups = N_PAGES // per

    def page_index(j):
        return lambda b, s, pt: (pt[b, s * per + j], 0, 0)

    grid_spec = pltpu.PrefetchScalarGridSpec(
        num_scalar_prefetch=1,
        grid=(nseq, groups),
        in_specs=[pl.BlockSpec((1, width, PAGE_SIZE), page_index(j)) for j in range(per)],
        out_specs=pl.BlockSpec((per * CHUNKS_PER_PAGE, CMP_STRIDE * width), lambda b, s, pt: (b * groups + s, 0)),
        scratch_shapes=[pltpu.VMEM((width // LANES * PAGE_SIZE, LANES), jnp.float32)])
    return pl.pallas_call(
        _pages_chunks_kernel,
        grid_spec=grid_spec,
        out_shape=jax.ShapeDtypeStruct((nseq * N_CMP_CHUNKS, CMP_STRIDE * width), jnp.float32),
        compiler_params=pltpu.CompilerParams(dimension_semantics=("arbitrary", "arbitrary")),
        name="pages_chunks",
    )(page_table, *([cache_t] * per))


def _one_query_softmax(s, keep, s_own):
    s = jnp.where(keep, s, NEG)
    m = jnp.maximum(jnp.max(s, axis=1, keepdims=True), s_own)
    e = jnp.where(keep, jnp.exp(s - m), 0.0)
    e_own = jnp.exp(s_own - m)
    inv = 1.0 / (jnp.sum(e, axis=1, keepdims=True) + e_own)
    return e * inv, e_own * inv


def _own_score(qb, k_row):
    return jnp.sum(qb.astype(jnp.float32) * k_row.astype(jnp.bfloat16).astype(jnp.float32), axis=1, keepdims=True)


def _nsa_sample_kernel(pt_ref, qb_ref, *refs):
    pps = PAGES_PER_STEP
    page_refs, abk_refs, abv_refs = refs[:pps], refs[pps:2 * pps], refs[2 * pps:3 * pps]
    (win_ref, new_ref, gates_ref, slopes_ref, pebk_ref, pebv_ref, w2k_ref, w2v_ref, gk_ref, bd_ref, ov_ref,
     grp_ref, eslc_ref, hmask_ref, o_ref, s_all, v_all, abk_all, abv_all) = refs[3 * pps:]
    p = pl.program_id(1)
    kvw = NSA_KV_WIDTH
    qb = qb_ref[0].astype(jnp.bfloat16)
    for j in range(pps):
        lane0 = pl.multiple_of((p * pps + j) * PAGE_SIZE, PAGE_SIZE)
        page = page_refs[j][0]
        s_all[:, pl.ds(lane0, PAGE_SIZE)] = jnp.dot(qb, page[:kvw].astype(jnp.bfloat16),
                                                    preferred_element_type=jnp.float32)
        v_all[:, pl.ds(lane0, PAGE_SIZE)] = page[kvw:].astype(jnp.bfloat16)
        c0 = pl.multiple_of((p * pps + j) * CHUNKS_PER_PAGE, CHUNKS_PER_PAGE)
        abk_all[pl.ds(c0, CHUNKS_PER_PAGE), :] = abk_refs[j][0]
        abv_all[pl.ds(c0, CHUNKS_PER_PAGE), :] = abv_refs[j][0]

    @pl.when(p == N_PAGES // pps - 1)
    def _():
        slopes = slopes_ref[...]
        new = new_ref[0]
        lane = lax.broadcasted_iota(jnp.int32, (NSA_HEADS, LANES), 1)

        def fold(full):
            m = full * hmask_ref[...]
            return m[0:4] + m[4:8] + m[8:12] + m[12:16]

        kc = _cmp_keys(abk_all[...], pebk_ref, w2k_ref, gk_ref, bd_ref)
        vc = _cmp_second_layer(abv_all[...], pebv_ref, w2v_ref)
        dist_c = (PAST_LEN - (CMP_LEN - 1) - CMP_STRIDE * lane).astype(jnp.float32)
        keep_c = lane < N_CMP_CHUNKS - 1
        s = jnp.where(keep_c, _nt(qb, kc.astype(jnp.bfloat16)) - slopes * dist_c, NEG)
        e = jnp.where(keep_c, jnp.exp(s - jnp.max(s, axis=1, keepdims=True)), 0.0)
        p_cmp = e / jnp.sum(e, axis=1, keepdims=True)
        o_cmp = jnp.dot(p_cmp.astype(jnp.bfloat16), vc.astype(jnp.bfloat16), preferred_element_type=jnp.float32)

        imp = jnp.dot(p_cmp, ov_ref[...], precision=lax.Precision.HIGHEST, preferred_element_type=jnp.float32)
        imp = jnp.dot(grp_ref[...], imp, precision=lax.Precision.HIGHEST, preferred_element_type=jnp.float32)
        sel = _top_lanes(jnp.where(lane < PAST_LEN // SEL_BLOCK, imp, NEG), SEL_TOPK - 1)
        picked = jnp.dot(sel.astype(jnp.bfloat16), eslc_ref[...], preferred_element_type=jnp.float32) > 0.5
        kpos = lax.broadcasted_iota(jnp.int32, (NSA_HEADS, PAST_LEN), 1)
        s = s_all[...] - slopes * (PAST_LEN - kpos).astype(jnp.float32)
        prob, p_own = _one_query_softmax(s, picked, _own_score(qb, new[:, 0:kvw]))
        o_slc = _nt(prob.astype(jnp.bfloat16), v_all[...]) + p_own * new[:, kvw:2 * kvw]

        win = win_ref[0]
        wpos = lax.broadcasted_iota(jnp.int32, (NSA_HEADS, WINDOW), 1)
        s = (jnp.dot(qb, win[:kvw].astype(jnp.bfloat16), preferred_element_type=jnp.float32)
             - slopes * (WINDOW - wpos).astype(jnp.float32))
        prob, p_own = _one_query_softmax(s, wpos >= 0, _own_score(qb, new[:, 2 * kvw:3 * kvw]))
        o_win = _nt(prob.astype(jnp.bfloat16), win[kvw:].astype(jnp.bfloat16)) + p_own * new[:, 3 * kvw:4 * kvw]

        g = gates_ref[0]
        o_ref[0] = g[0] * fold(o_cmp) + g[1] * fold(o_slc) + g[2] * fold(o_win)


def _moba_sample_kernel(pt_ref, qb_ref, *refs):
    pps = PAGES_PER_STEP
    page_refs = refs[:pps]
    new_ref, slopes_ref, emoba_ref, hmask_ref, o_ref, s_all, v_all, ksum_ref = refs[pps:]
    p = pl.program_id(1)
    w = MOBA_WIDTH
    qf = qb_ref[0]
    qb = qf.astype(jnp.bfloat16)

    per_blk = MOBA_BLOCK // PAGE_SIZE
    k_sum = None
    for j in range(pps):
        page_no = p * pps + j
        lane0 = pl.multiple_of(page_no * PAGE_SIZE, PAGE_SIZE)
        page = page_refs[j][0]
        k_t = page[:w]
        s_all[:, pl.ds(lane0, PAGE_SIZE)] = jnp.dot(qb, k_t.astype(jnp.bfloat16), preferred_element_type=jnp.float32)
        v_all[:, pl.ds(lane0, PAGE_SIZE)] = page[w:].astype(jnp.bfloat16)
        k_sum = k_t if j % per_blk == 0 else k_sum + k_t
        if j % per_blk == per_blk - 1:
            ksum_ref[p * (pps // per_blk) + j // per_blk] = k_sum

    @pl.when(p == N_PAGES // pps - 1)
    def _():
        slopes = slopes_ref[...]
        new = new_ref[0]
        lane = lax.broadcasted_iota(jnp.int32, (MOBA_HEADS, LANES), 1)
        sq = lax.broadcasted_iota(jnp.int32, (w, LANES), 1)
        ksum = jnp.zeros((w, LANES), jnp.float32)
        for n in range(PAST_LEN // MOBA_BLOCK):
            ksum = jnp.where(sq == n, jnp.sum(ksum_ref[n], axis=1, keepdims=True), ksum)
        gate = jnp.dot(qf, ksum, precision=lax.Precision.HIGHEST, preferred_element_type=jnp.float32)
        sel = _top_lanes(jnp.where(lane < PAST_LEN // MOBA_BLOCK, gate, NEG), MOBA_TOPK)
        picked = jnp.dot(sel.astype(jnp.bfloat16), emoba_ref[...], preferred_element_type=jnp.float32) > 0.5
        kpos = lax.broadcasted_iota(jnp.int32, (MOBA_HEADS, PAST_LEN), 1)
        s = s_all[...] - slopes * (PAST_LEN - kpos).astype(jnp.float32)
        prob, p_own = _one_query_softmax(s, picked, _own_score(qb, new[:, :w]))
        full = _nt(prob.astype(jnp.bfloat16), v_all[...]) + p_own * new[:, w:]
        o_ref[0] = jnp.sum(full * hmask_ref[...], axis=0, keepdims=True)


def _head_rows(q, lane_head):
    n = int(max(lane_head)) + 1
    onehot = jnp.asarray(np.eye(n, dtype=np.float32)[np.asarray(lane_head)])
    return jnp.einsum('bhd,hn->bhnd', q, onehot).reshape(q.shape[0], q.shape[1], n * HEAD_DIM)


def _block_expand_t(block):
    e = (np.arange(LANES)[:, None] == (np.arange(PAST_LEN)[None, :] // block)).astype(np.float32)
    return jnp.asarray(e, jnp.bfloat16)


def _const_spec(shape):
    nd = len(shape)
    return pl.BlockSpec(shape, lambda b, p, pt: (0,) * nd)


def _page_index(j, row_block):
    return lambda b, p, pt: (pt[b, p * PAGES_PER_STEP + j], row_block, 0)


def _context_page_index(j):
    return lambda b, p, pt: (b * N_PAGES + p * PAGES_PER_STEP + j, 0, 0)


def _nsa_sample(page_table, q, cache_t, abk, abv, win_t, new, gates, slopes, cw):
    nseq = q.shape[0]
    kvw = NSA_KV_WIDTH
    pps = PAGES_PER_STEP
    group_of_head = [h // NSA_GROUP for h in range(NSA_HEADS)]
    qb = _head_rows(q * SCALE, group_of_head)
    hmask = _head_rows(jnp.ones((1, NSA_HEADS, HEAD_DIM), jnp.float32), group_of_head)[0]
    hh = np.arange(NSA_HEADS)
    grp = ((hh[:, None] // NSA_GROUP) == (hh[None, :] // NSA_GROUP)).astype(np.float32)
    consts = [slopes.reshape(NSA_HEADS, 1)] + _cmp_consts(cw) + [jnp.asarray(grp), _block_expand_t(SEL_BLOCK), hmask]
    grid_spec = pltpu.PrefetchScalarGridSpec(
        num_scalar_prefetch=1,
        grid=(nseq, N_PAGES // pps),
        in_specs=[pl.BlockSpec((1, NSA_HEADS, kvw), lambda b, p, pt: (b, 0, 0))]
                 + [pl.BlockSpec((1, 2 * kvw, PAGE_SIZE), _page_index(j, 1)) for j in range(pps)]
                 + [pl.BlockSpec((1, CHUNKS_PER_PAGE, 2 * kvw), _context_page_index(j)) for j in range(pps)] * 2
                 + [pl.BlockSpec((1, 2 * kvw, WINDOW), lambda b, p, pt: (b, 0, 0)),
                    pl.BlockSpec((1, 1, 4 * kvw), lambda b, p, pt: (b, 0, 0)),
                    pl.BlockSpec((1, 3, NSA_GROUP, kvw), lambda b, p, pt: (b, 0, 0, 0))]
                 + [_const_spec(c.shape) for c in consts],
        out_specs=pl.BlockSpec((1, NSA_GROUP, kvw), lambda b, p, pt: (b, 0, 0)),
        scratch_shapes=[pltpu.VMEM((NSA_HEADS, PAST_LEN), jnp.float32),
                        pltpu.VMEM((kvw, PAST_LEN), jnp.bfloat16),
                        pltpu.VMEM((N_CMP_CHUNKS, 2 * kvw), jnp.float32),
                        pltpu.VMEM((N_CMP_CHUNKS, 2 * kvw), jnp.float32)])
    return pl.pallas_call(
        _nsa_sample_kernel,
        grid_spec=grid_spec,
        out_shape=jax.ShapeDtypeStruct((nseq, NSA_GROUP, kvw), jnp.float32),
        compiler_params=pltpu.CompilerParams(dimension_semantics=("arbitrary", "arbitrary")),
        name="nsa_sample",
    )(page_table, qb, *([cache_t] * pps + [abk] * pps + [abv] * pps), win_t, new, gates, *consts)


def _moba_sample(page_table, q, cache_t, new, slopes):
    nseq = q.shape[0]
    w = MOBA_WIDTH
    pps = PAGES_PER_STEP
    own_head = list(range(MOBA_HEADS))
    qb = _head_rows(q * SCALE, own_head)
    hmask = _head_rows(jnp.ones((1, MOBA_HEADS, HEAD_DIM), jnp.float32), own_head)[0]
    consts = [slopes.reshape(MOBA_HEADS, 1), _block_expand_t(MOBA_BLOCK), hmask]
    grid_spec = pltpu.PrefetchScalarGridSpec(
        num_scalar_prefetch=1,
        grid=(nseq, N_PAGES // pps),
        in_specs=[pl.BlockSpec((1, MOBA_HEADS, w), lambda b, p, pt: (b, 0, 0))]
                 + [pl.BlockSpec((1, 2 * w, PAGE_SIZE), _page_index(j, 0)) for j in range(pps)]
                 + [pl.BlockSpec((1, 1, 2 * w), lambda b, p, pt: (b, 0, 0))]
                 + [_const_spec(c.shape) for c in consts],
        out_specs=pl.BlockSpec((1, 1, w), lambda b, p, pt: (b, 0, 0)),
        scratch_shapes=[pltpu.VMEM((MOBA_HEADS, PAST_LEN), jnp.float32),
                        pltpu.VMEM((w, PAST_LEN), jnp.bfloat16),
                        pltpu.VMEM((PAST_LEN // MOBA_BLOCK, w, LANES), jnp.float32)])
    return pl.pallas_call(
        _moba_sample_kernel,
        grid_spec=grid_spec,
        out_shape=jax.ShapeDtypeStruct((nseq, 1, w), jnp.float32),
        compiler_params=pltpu.CompilerParams(dimension_semantics=("arbitrary", "arbitrary"),
                                             vmem_limit_bytes=VMEM_LIMIT),
        name="moba_sample",
    )(page_table, qb, *([cache_t] * pps), new, *consts)


def _pick_top(s_ref, vals_ref, rows_ref, k):
    r_total, width = s_ref.shape
    cw = min(2 * LANES, width)
    kid = lax.broadcasted_iota(jnp.int32, (k, cw), 0)
    rowid = lax.broadcasted_iota(jnp.int32, (r_total, cw), 0)

    def one_pass(ci, _):
        col = pl.multiple_of(ci * cw, cw)

        def body(r, carry):
            w, vals, rows = carry
            mx = jnp.max(w, axis=0, keepdims=True)
            first = jnp.min(jnp.where(w == mx, rowid, r_total), axis=0, keepdims=True)
            w = jnp.where(rowid == first, F32_MIN, w)
            return w, jnp.where(kid == r, mx, vals), jnp.where(kid == r, first, rows)

        init = (s_ref[:, pl.ds(col, cw)], jnp.zeros((k, cw), jnp.float32), jnp.zeros((k, cw), jnp.int32))
        _, vals, rows = lax.fori_loop(0, k, body, init)
        vals_ref[:, pl.ds(col, cw)] = vals
        rows_ref[:, pl.ds(col, cw)] = rows
        return 0

    lax.fori_loop(0, width // cw, one_pass, 0)
    return vals_ref[...], rows_ref[...]


def _peer_route_kernel(x_ref, g_ref, wq_ref, subk_ref, xt_ref, c0_ref, e0_ref, r1_ref, e1_ref, sc_ref, cand_ref,
                       vals_ref, rows_ref, best_ref, crows_ref):
    x = x_ref[...]
    xn = x * lax.rsqrt(jnp.mean(x * x, axis=-1, keepdims=True) + RMS_EPS) * g_ref[...]
    xb = xn.astype(jnp.bfloat16)
    xt_ref[...] = xn.T.astype(jnp.bfloat16)
    t = x.shape[0]
    q = jnp.dot(xb, wq_ref[...], preferred_element_type=jnp.float32)
    half = PEER_QDIM // 2
    k = PEER_TOPK
    for j in range(2 * PEER_HEADS):
        sc_ref[:, j * t:(j + 1) * t] = _nt(q[:, j * half:(j + 1) * half], subk_ref[j],
                                           precision=lax.Precision.HIGHEST).T
    vals, rows = _pick_top(sc_ref, vals_ref, rows_ref, k)
    for p in range(PEER_HEADS):
        av = vals[:, (2 * p) * t:(2 * p + 1) * t]
        bv = vals[:, (2 * p + 1) * t:(2 * p + 2) * t]
        for a in range(k):
            lo, n_b = PEER_CAND_START[a], PEER_CAND_START[a + 1] - PEER_CAND_START[a]
            cand_ref[lo:lo + n_b, p * t:(p + 1) * t] = av[a:a + 1, :] + bv[0:n_b]
        n_real = PEER_CAND_START[k]
        cand_ref[n_real:, p * t:(p + 1) * t] = jnp.full((cand_ref.shape[0] - n_real, t), F32_MIN, jnp.float32)
    best, crows = _pick_top(cand_ref, best_ref, crows_ref, k)
    zsum = jnp.sum(jnp.exp(best - best[0:1, :]), axis=0, keepdims=True)
    win_a = jnp.zeros(crows.shape, jnp.int32)
    for a in range(1, k):
        win_a = win_a + (crows >= PEER_CAND_START[a]).astype(jnp.int32)
    keyid = lax.broadcasted_iota(jnp.int32, (PEER_KEYS, t), 0)
    for p in range(PEER_HEADS):
        lo0, lo1 = (2 * p) * t, (2 * p + 1) * t
        s0, s1 = sc_ref[:, lo0:lo0 + t], sc_ref[:, lo1:lo1 + t]
        c0 = jnp.zeros((PEER_KEYS, t), jnp.float32)
        r1 = jnp.full((PEER_KEYS, t), float(k), jnp.float32)
        for a in range(k):
            cnt_a = jnp.sum((win_a[:, p * t:(p + 1) * t] == a).astype(jnp.float32), axis=0, keepdims=True)
            c0 = jnp.where(keyid == rows[a:a + 1, lo0:lo0 + t], cnt_a, c0)
            r1 = jnp.where(keyid == rows[a:a + 1, lo1:lo1 + t], float(a), r1)
        c0_ref[p] = c0
        e0_ref[p] = jnp.where(c0 > 0.0, jnp.exp(s0 - vals[0:1, lo0:lo0 + t]), 0.0) / zsum[:, p * t:(p + 1) * t]
        r1_ref[p] = r1
        e1_ref[p] = jnp.where(r1 < k, jnp.exp(s1 - vals[0:1, lo1:lo1 + t]), 0.0)


def _peer_expert_kernel(x_ref, xt_ref, c0_ref, e0_ref, r1_ref, e1_ref, u_ref, vt_ref, o_ref, acc_ref, h_ref):
    c = pl.program_id(1)
    per = PEER_CHUNK // PEER_KEYS

    @pl.when(c == 0)
    def _():
        acc_ref[...] = jnp.zeros_like(acc_ref)

    ga = jax.nn.gelu(jnp.dot(u_ref[...], xt_ref[...], preferred_element_type=jnp.float32))
    for k in range(per):
        i1 = c * per + k
        wt = None
        for p in range(PEER_HEADS):
            row_c = c0_ref[p, pl.ds(i1, 1), :]
            row_e = e0_ref[p, pl.ds(i1, 1), :]
            term = jnp.where(r1_ref[p] < row_c, row_e * e1_ref[p], 0.0)
            wt = term if wt is None else wt + term
        rows = slice(k * PEER_KEYS, (k + 1) * PEER_KEYS)
        h_ref[rows, :] = (wt * ga[rows]).astype(jnp.bfloat16)
    acc_ref[...] += jnp.dot(vt_ref[...], h_ref[...], preferred_element_type=jnp.float32)

    @pl.when(c == pl.num_programs(1) - 1)
    def _():
        o_ref[...] = x_ref[...] + acc_ref[...].T


def _peer_residual(x, g_ffn, wq_b, subk, u_b, vt_b):
    n, d = x.shape
    t1 = 256 if n % 256 == 0 else n
    hp = PEER_HEADS
    tab = jax.ShapeDtypeStruct((hp, PEER_KEYS, n), jnp.float32)
    tab_spec = pl.BlockSpec((hp, PEER_KEYS, t1), lambda i: (0, 0, i))
    xt, c0, e0, r1, e1 = pl.pallas_call(
        _peer_route_kernel,
        grid=(n // t1,),
        in_specs=[pl.BlockSpec((t1, d), lambda i: (i, 0)),
                  pl.BlockSpec((1, d), lambda i: (0, 0)),
                  pl.BlockSpec((d, hp * PEER_QDIM), lambda i: (0, 0)),
                  pl.BlockSpec((2 * hp, PEER_KEYS, PEER_QDIM // 2), lambda i: (0, 0, 0))],
        out_specs=[pl.BlockSpec((d, t1), lambda i: (0, i)), tab_spec, tab_spec, tab_spec, tab_spec],
        out_shape=[jax.ShapeDtypeStruct((d, n), jnp.bfloat16), tab, tab, tab, tab],
        scratch_shapes=[pltpu.VMEM((PEER_KEYS, 2 * hp * t1), jnp.float32),
                        pltpu.VMEM((-(-PEER_CAND_START[PEER_TOPK] // 8) * 8, hp * t1), jnp.float32),
                        pltpu.VMEM((PEER_TOPK, 2 * hp * t1), jnp.float32),
                        pltpu.VMEM((PEER_TOPK, 2 * hp * t1), jnp.int32),
                        pltpu.VMEM((PEER_TOPK, hp * t1), jnp.float32),
                        pltpu.VMEM((PEER_TOPK, hp * t1), jnp.int32)],
        compiler_params=pltpu.CompilerParams(dimension_semantics=("arbitrary",), vmem_limit_bytes=VMEM_LIMIT),
        name="peer_route",
    )(x, g_ffn.reshape(1, d), wq_b, subk.reshape(2 * hp, PEER_KEYS, PEER_QDIM // 2))
    t2 = min(ROW_TILE, n)
    once = pl.Buffered(1)
    tab_spec2 = pl.BlockSpec((hp, PEER_KEYS, t2), lambda i, c: (0, 0, i), pipeline_mode=once)
    return pl.pallas_call(
        _peer_expert_kernel,
        grid=(n // t2, PEER_EXPERTS // PEER_CHUNK),
        in_specs=[pl.BlockSpec((t2, d), lambda i, c: (i, 0), pipeline_mode=once),
                  pl.BlockSpec((d, t2), lambda i, c: (0, i), pipeline_mode=once),
                  tab_spec2, tab_spec2, tab_spec2, tab_spec2,
                  pl.BlockSpec((PEER_CHUNK, d), lambda i, c: (c, 0)),
                  pl.BlockSpec((d, PEER_CHUNK), lambda i, c: (0, c))],
        out_specs=pl.BlockSpec((t2, d), lambda i, c: (i, 0)),
        out_shape=jax.ShapeDtypeStruct((n, d), jnp.float32),
        scratch_shapes=[pltpu.VMEM((d, t2), jnp.float32), pltpu.VMEM((PEER_CHUNK, t2), jnp.bfloat16)],
        compiler_params=pltpu.CompilerParams(dimension_semantics=("arbitrary", "arbitrary"),
                                             vmem_limit_bytes=VMEM_LIMIT),
        name="peer_experts",
    )(x, xt, c0, e0, r1, e1, u_b, vt_b)


def _alibi_slopes():
    n = NSA_HEADS + MOBA_HEADS
    s = jnp.exp2(-8.0 * jnp.arange(1, n + 1, dtype=jnp.float32) / n)
    return s[0::2], s[1::2]


def _in_proj_weights(w_in, gains):
    w_t = jnp.transpose(w_in).astype(jnp.bfloat16)
    kvw = NSA_KV_WIDTH
    edges = np.cumsum([0, NSA_WIDTH] + [kvw] * 6 + [N_GATE] + [MOBA_WIDTH] * 3)
    q_n, kc, vc, ks, vs, kw, vw, gt, q_m, k_m, v_m = [w_t[a:b] for a, b in zip(edges[:-1], edges[1:])]

    def gain(key, n_heads):
        return jnp.tile(gains[key].reshape(HEAD_DIM), n_heads)

    def raw(n):
        return jnp.ones((n,), jnp.float32)

    def pack(parts, gain_parts, flags):
        flag = jnp.concatenate([jnp.full((p.shape[0],), f, jnp.float32) for p, f in zip(parts, flags)])
        return jnp.concatenate(parts, axis=0), jnp.concatenate(gain_parts).reshape(-1, 1), flag.reshape(-1, 1)

    per_step = N_GATE // 2
    pad = jnp.zeros((LANES - per_step, D_MODEL), jnp.bfloat16)
    gt_rows = [gt[:per_step], pad, gt[per_step:], pad]
    return {
        'q_n': pack([q_n], [gain('g_q_nsa', NSA_HEADS)], [1.0]),
        'nsa': pack([kc, vc, ks, vs], [raw(kvw), raw(kvw), gain('g_k_slc', NSA_KV_HEADS), raw(kvw)], [0., 0., 1., 0.]),
        'win': pack([kw, vw], [gain('g_k_win', NSA_KV_HEADS), raw(kvw)], [1., 0.]),
        'cmp': pack([kc, vc], [raw(kvw), raw(kvw)], [0., 0.]),
        'gates': pack(gt_rows, [raw(2 * LANES)], [0., 0., 0., 0.]),
        'q_m': pack([q_m], [gain('g_q_moba', MOBA_HEADS)], [1.0]),
        'moba': pack([k_m, v_m], [gain('g_k_moba', MOBA_HEADS), raw(MOBA_WIDTH)], [1., 0.]),
    }


def _in_proj(x, g_mix, pw, batch):
    h_t = _norm_transpose(x, g_mix)
    out = {}
    for key in ('nsa', 'win', 'moba'):
        out[key] = _proj_t(h_t, *pw[key], batch=batch, rows=512)
    for key in ('q_n', 'q_m', 'cmp'):
        out[key] = _proj_t(h_t, *pw[key], batch=batch, rows=512, natural=True)
    out['gates'] = _proj_t(h_t, *pw['gates'], batch=batch, rows=2 * LANES, natural=True, act='sigmoid')
    return out


def _rows_leaf(rows_t, kinds, heads):
    b, _, t = rows_t.shape
    return jnp.transpose(rows_t.reshape(b, kinds, heads, HEAD_DIM, t), (0, 4, 1, 2, 3))[None]


def kernel(x_prompt, x_sample, cache_nsa, cache_moba, state_win, page_table, p_prompt, p_sample,
           g_mix, w_in, g_q_nsa, g_k_cmp, g_k_slc, g_k_win, g_q_moba, g_k_moba,
           cmp_pe_k, cmp_w1_k, cmp_w2_k, cmp_pe_v, cmp_w1_v, cmp_w2_v,
           g_out_nsa, g_out_moba, w_out, g_ffn, w_peer_q, peer_sub_keys, peer_u, peer_v,
           g_ple, w_ple_gate, w_ple_proj, g_ple_post):
    assert w_in.shape[0] == 1, "single-layer trunk"
    slopes_nsa, slopes_moba = _alibi_slopes()
    kvw = NSA_KV_WIDTH
    bf = jnp.bfloat16
    pw = _in_proj_weights(w_in[0], {'g_q_nsa': g_q_nsa[0], 'g_k_slc': g_k_slc[0], 'g_k_win': g_k_win[0],
                                    'g_q_moba': g_q_moba[0], 'g_k_moba': g_k_moba[0]})
    cw = {}
    w1k_bd, cw['w2k'], cw['pebk'] = _cmp_weights(cmp_w1_k[0], cmp_w2_k[0], cmp_pe_k[0])
    w1v_bd, cw['w2v'], cw['pebv'] = _cmp_weights(cmp_w1_v[0], cmp_w2_v[0], cmp_pe_v[0])
    cw['gk'] = jnp.tile(g_k_cmp[0].reshape(1, HEAD_DIM), (1, NSA_KV_HEADS))
    w_out_b, wq_b, u_b = w_out[0].astype(bf), w_peer_q[0].astype(bf), peer_u[0].astype(bf)
    vt_b = jnp.transpose(peer_v[0]).astype(bf)
    wg_b, wp_b = w_ple_gate[0].astype(bf), w_ple_proj[0].astype(bf)

    def tail(x, o_nsa, o_moba, ple):
        x = _out_proj(o_nsa, o_moba, x, g_out_nsa[0], g_out_moba[0], w_out_b)
        x = _peer_residual(x, g_ffn[0], wq_b, peer_sub_keys[0], u_b, vt_b)
        return _ple(x, ple, g_ple[0], wg_b, wp_b, g_ple_post[0])

    b, t, d = x_prompt.shape
    xp = x_prompt.reshape(b * t, d)
    pr = _in_proj(xp, g_mix[0], pw, b)
    q_n = pr['q_n'].reshape(b, t, NSA_WIDTH)
    abk, abv = _cmp_chunks(pr['cmp'].reshape(b * t // CMP_STRIDE, CMP_STRIDE * 2 * kvw), w1k_bd, w1v_bd)
    o_cmp, mask = _cmp_select(q_n, abk.reshape(b, t // CMP_STRIDE, 2 * kvw), abv.reshape(b, t // CMP_STRIDE, 2 * kvw),
                              slopes_nsa, cw)
    o_slc = _attention(q_n, pr['nsa'], slopes_nsa, mode='mask', heads=NSA_STEP_HEADS, k_row0=4, v_row0=6,
                       mask=mask, block=SEL_BLOCK, n_masks=2)
    o_nsa = _attention(q_n, pr['win'], slopes_nsa, mode='window', heads=NSA_STEP_HEADS, k_row0=0, v_row0=2,
                       combine=(o_cmp, o_slc, pr['gates'].reshape(b, t, 2 * LANES)))
    o_moba = _attention(pr['q_m'].reshape(b, t, MOBA_WIDTH), pr['moba'], slopes_moba, mode='gate',
                        heads=MOBA_STEP_HEADS, k_row0=0, v_row0=MOBA_WIDTH // LANES, block=MOBA_BLOCK)
    y_prompt = tail(xp, o_nsa.reshape(b * t, NSA_WIDTH), o_moba.reshape(b * t, MOBA_WIDTH),
                    p_prompt[0].reshape(b * t, PLE_DIM)).reshape(b, t, d)

    ns = x_sample.shape[0]
    xs = x_sample.reshape(ns, d)
    sr = _in_proj(xs, g_mix[0], pw, 1)
    n_phys = cache_nsa.shape[1]
    cache_n_t = jnp.transpose(cache_nsa[0], (0, 2, 3, 4, 1)).reshape(n_phys, 4 * kvw, PAGE_SIZE)
    cache_m_t = jnp.transpose(cache_moba[0], (0, 2, 3, 4, 1)).reshape(n_phys, 2 * MOBA_WIDTH, PAGE_SIZE)
    win_t = jnp.transpose(state_win[0], (0, 2, 3, 4, 1)).reshape(ns, 2 * kvw, WINDOW)
    abk, abv = _cmp_chunks(_pages_chunks(page_table, cache_n_t), w1k_bd, w1v_bd)
    nsa_new = jnp.transpose(sr['nsa'][0])
    win_new = jnp.transpose(sr['win'][0])
    moba_new = jnp.transpose(sr['moba'][0])
    new_n = jnp.concatenate([nsa_new[:, 2 * kvw:], win_new], axis=1)[:, None, :]
    gs = sr['gates']
    gs = jnp.concatenate([gs[:, :N_GATE // 2], gs[:, LANES:LANES + N_GATE // 2]], axis=1)
    g3 = jnp.transpose(gs.reshape(ns, NSA_KV_HEADS, NSA_GROUP, 3), (0, 3, 2, 1))[..., None]
    g3 = jnp.broadcast_to(g3, (ns, 3, NSA_GROUP, NSA_KV_HEADS, HEAD_DIM)).reshape(ns, 3, NSA_GROUP, kvw)
    o_nsa_s = _nsa_sample(page_table, sr['q_n'].reshape(ns, NSA_HEADS, HEAD_DIM), cache_n_t,
                          abk.reshape(ns * N_PAGES, CHUNKS_PER_PAGE, 2 * kvw),
                          abv.reshape(ns * N_PAGES, CHUNKS_PER_PAGE, 2 * kvw),
                          win_t, new_n, g3, slopes_nsa, cw)
    o_nsa_s = jnp.transpose(o_nsa_s.reshape(ns, NSA_GROUP, NSA_KV_HEADS, HEAD_DIM), (0, 2, 1, 3)).reshape(ns, NSA_WIDTH)
    o_moba_s = _moba_sample(page_table, sr['q_m'].reshape(ns, MOBA_HEADS, HEAD_DIM), cache_m_t,
                            moba_new[:, None, :], slopes_moba).reshape(ns, MOBA_WIDTH)
    y_sample = tail(xs, o_nsa_s, o_moba_s, p_sample[0].reshape(ns, PLE_DIM)).reshape(ns, 1, d)

    win_prompt = _rows_leaf(pr['win'][:, :, t - WINDOW:], 2, NSA_KV_HEADS)
    nsa_rows_s = nsa_new.reshape(1, ns, 1, 4, NSA_KV_HEADS, HEAD_DIM)
    moba_rows_s = moba_new.reshape(1, ns, 1, 2, MOBA_HEADS, HEAD_DIM)
    win_s = jnp.concatenate([state_win[0][:, 1:], win_new.reshape(ns, 1, 2, NSA_KV_HEADS, HEAD_DIM)], axis=1)[None]
    return (y_prompt, y_sample, _rows_leaf(pr['nsa'], 4, NSA_KV_HEADS), _rows_leaf(pr['moba'], 2, MOBA_HEADS),
            win_prompt, nsa_rows_s, moba_rows_s, win_s)
```

```python
import functools

import jax
import jax.numpy as jnp
from jax import lax
import numpy as np
from jax.experimental import pallas as pl
from jax.experimental.pallas import tpu as pltpu

D_MODEL = 2048
PAST_LEN = 2048
PAGE_SIZE = 128
HEAD_DIM = 64
NSA_HEADS = D_MODEL // (2 * HEAD_DIM)
NSA_KV_HEADS = max(1, NSA_HEADS // 4)
NSA_GROUP = NSA_HEADS // NSA_KV_HEADS
MOBA_HEADS = D_MODEL // (2 * HEAD_DIM)
NSA_WIDTH = NSA_HEADS * HEAD_DIM
NSA_KV_WIDTH = NSA_KV_HEADS * HEAD_DIM
MOBA_WIDTH = MOBA_HEADS * HEAD_DIM
N_GATE = 3 * NSA_HEADS
CMP_LEN = 32
CMP_STRIDE = 16
CMP_HIDDEN = 64
SEL_BLOCK = 64
SEL_TOPK = 8
WINDOW = 512
MOBA_BLOCK = 256
MOBA_TOPK = 3
PEER_KEYS = 128
PEER_EXPERTS = PEER_KEYS * PEER_KEYS
PEER_HEADS = 8
PEER_TOPK = 16
PEER_QDIM = 256
PLE_DIM = 256
RMS_EPS = 1e-6
NEG = -1e30
OWN_SCORE = 1e9
SCALE = HEAD_DIM ** -0.5

LANES = 128
ATTN_TILE = 256
ROW_TILE = 512
PEER_CHUNK = 1024
VMEM_LIMIT = 56 * 1024 * 1024
N_PAGES = PAST_LEN // PAGE_SIZE
PAGES_PER_STEP = 8
N_CMP_CHUNKS = PAST_LEN // CMP_STRIDE
CHUNKS_PER_PAGE = PAGE_SIZE // CMP_STRIDE
F32_MIN = float(np.finfo(np.float32).min)
PEER_CAND_START = tuple(int(v) for v in np.cumsum([0] + [PEER_TOPK // (a + 1) for a in range(PEER_TOPK)]))

MOBA_STEP_HEADS = ((0, 0, 0), (HEAD_DIM, 1, 1))
NSA_STEP_HEADS = tuple((g * NSA_GROUP * HEAD_DIM + r * HEAD_DIM, g, g) for g in range(2) for r in range(NSA_GROUP))


def _nt(a, b, **kw):
    return lax.dot_general(a, b, (((1,), (1,)), ((), ())), preferred_element_type=jnp.float32, **kw)


def _top_lanes(score, k):
    lane = lax.broadcasted_iota(jnp.int32, score.shape, 1)
    sel = jnp.zeros(score.shape, jnp.float32)
    for _ in range(k):
        m = jnp.max(score, axis=1, keepdims=True)
        first = jnp.min(jnp.where(score == m, lane, LANES), axis=1, keepdims=True)
        hit = lane == first
        sel = jnp.where(hit & (m > 0.5 * NEG), 1.0, sel)
        score = jnp.where(hit, F32_MIN, score)
    return sel


def _extract_top(work, k):
    r_total, t = work.shape
    rowid = lax.broadcasted_iota(jnp.int32, (r_total, t), 0)
    kid = lax.broadcasted_iota(jnp.int32, (k, t), 0)

    def body(r, carry):
        w, vals = carry
        mx = jnp.max(w, axis=0, keepdims=True)
        first = jnp.min(jnp.where(w == mx, rowid, r_total), axis=0, keepdims=True)
        w = jnp.where(rowid == first, F32_MIN, w)
        vals = jnp.where(kid == r, mx, vals)
        return w, vals

    w, vals = lax.fori_loop(0, k, body, (work, jnp.zeros((k, t), jnp.float32)))
    return vals, w


def _norm_t_kernel(x_ref, g_ref, o_ref):
    x = x_ref[...]
    xn = x * lax.rsqrt(jnp.mean(x * x, axis=-1, keepdims=True) + RMS_EPS) * g_ref[...]
    o_ref[...] = xn.T.astype(jnp.bfloat16)


def _norm_transpose(x, g):
    n, d = x.shape
    tm = min(ROW_TILE, n)
    return pl.pallas_call(
        _norm_t_kernel,
        grid=(n // tm,),
        in_specs=[pl.BlockSpec((tm, d), lambda i: (i, 0)), pl.BlockSpec((1, d), lambda i: (0, 0))],
        out_specs=pl.BlockSpec((d, tm), lambda i: (0, i)),
        out_shape=jax.ShapeDtypeStruct((d, n), jnp.bfloat16),
        compiler_params=pltpu.CompilerParams(dimension_semantics=("arbitrary",)),
        name="norm_transpose",
    )(x, g.reshape(1, d))


def _proj_t_kernel(act, natural, w_ref, ht_ref, gain_ref, flag_ref, o_ref):
    y = jnp.dot(w_ref[...], ht_ref[...], preferred_element_type=jnp.float32)
    r, tm = y.shape
    y3 = y.reshape(r // HEAD_DIM, HEAD_DIM, tm)
    ms = jnp.mean(y3 * y3, axis=1, keepdims=True)
    yn = (y3 * lax.rsqrt(ms + RMS_EPS)).reshape(r, tm) * gain_ref[...]
    y = jnp.where(flag_ref[...] > 0.5, yn, y)
    if act == 'sigmoid':
        y = jax.nn.sigmoid(y)
    if natural:
        o_ref[...] = y.T
    else:
        o_ref[0] = y


def _proj_t(ht, w_t, gain, flag, *, batch, rows, act=None, natural=False):
    c, d = w_t.shape
    n = ht.shape[1]
    t = n // batch
    tm = min(ROW_TILE, t)
    nt = t // tm
    if natural:
        out_shape = jax.ShapeDtypeStruct((n, c), jnp.float32)
        out_spec = pl.BlockSpec((tm, rows), lambda b, i, j: (b * nt + i, j))
    else:
        out_shape = jax.ShapeDtypeStruct((batch, c, t), jnp.float32)
        out_spec = pl.BlockSpec((1, rows, tm), lambda b, i, j: (b, j, i))
    return pl.pallas_call(
        functools.partial(_proj_t_kernel, act, natural),
        grid=(batch, nt, c // rows),
        in_specs=[pl.BlockSpec((rows, d), lambda b, i, j: (j, 0)),
                  pl.BlockSpec((d, tm), lambda b, i, j: (0, b * nt + i)),
                  pl.BlockSpec((rows, 1), lambda b, i, j: (j, 0)),
                  pl.BlockSpec((rows, 1), lambda b, i, j: (j, 0))],
        out_specs=out_spec,
        out_shape=out_shape,
        compiler_params=pltpu.CompilerParams(dimension_semantics=("arbitrary",) * 3),
        name="proj_t",
    )(w_t, ht, gain, flag)


def _out_proj_kernel(on_ref, om_ref, x_ref, gn_ref, gm_ref, w_ref, o_ref, mix_ref):
    @pl.when(pl.program_id(1) == 0)
    def _():
        for src, g_ref, lo in ((on_ref, gn_ref, 0), (om_ref, gm_ref, NSA_WIDTH)):
            a = src[...]
            an = a * lax.rsqrt(jnp.mean(a * a, axis=-1, keepdims=True) + RMS_EPS) * g_ref[...]
            mix_ref[:, lo:lo + a.shape[1]] = an.astype(jnp.bfloat16)

    o_ref[...] = x_ref[...] + jnp.dot(mix_ref[...], w_ref[...], preferred_element_type=jnp.float32)


def _out_proj(o_nsa, o_moba, x, g_nsa, g_moba, w_b):
    n, d = x.shape
    tm = min(ROW_TILE, n)
    tn = 512
    return pl.pallas_call(
        _out_proj_kernel,
        grid=(n // tm, d // tn),
        in_specs=[pl.BlockSpec((tm, NSA_WIDTH), lambda i, j: (i, 0)),
                  pl.BlockSpec((tm, MOBA_WIDTH), lambda i, j: (i, 0)),
                  pl.BlockSpec((tm, tn), lambda i, j: (i, j)),
                  pl.BlockSpec((1, NSA_WIDTH), lambda i, j: (0, 0)),
                  pl.BlockSpec((1, MOBA_WIDTH), lambda i, j: (0, 0)),
                  pl.BlockSpec((NSA_WIDTH + MOBA_WIDTH, tn), lambda i, j: (0, j))],
        out_specs=pl.BlockSpec((tm, tn), lambda i, j: (i, j)),
        out_shape=jax.ShapeDtypeStruct((n, d), jnp.float32),
        scratch_shapes=[pltpu.VMEM((tm, NSA_WIDTH + MOBA_WIDTH), jnp.bfloat16)],
        compiler_params=pltpu.CompilerParams(dimension_semantics=("arbitrary", "arbitrary")),
        name="out_proj",
    )(o_nsa, o_moba, x, g_nsa.reshape(1, -1), g_moba.reshape(1, -1), w_b)


def _ple_kernel(tn, x_ref, p_ref, g_ref, wg_ref, wp_ref, gp_ref, o_ref, xn_ref, pn_ref):
    j = pl.program_id(1)

    @pl.when(j == 0)
    def _():
        x = x_ref[...]
        xn_ref[...] = (x * lax.rsqrt(jnp.mean(x * x, axis=-1, keepdims=True) + RMS_EPS) * g_ref[...]
                       ).astype(jnp.bfloat16)
        pr = jnp.dot(p_ref[...].astype(jnp.bfloat16), wp_ref[...], preferred_element_type=jnp.float32)
        pn_ref[...] = pr * lax.rsqrt(jnp.mean(pr * pr, axis=-1, keepdims=True) + RMS_EPS) * gp_ref[...]

    col = pl.multiple_of(j * tn, tn)
    gate = jax.nn.sigmoid(jnp.dot(xn_ref[...], wg_ref[...], preferred_element_type=jnp.float32))
    o_ref[...] = x_ref[:, pl.ds(col, tn)] + gate * pn_ref[:, pl.ds(col, tn)]


def _ple(x, ple, g_ple, wg_b, wp_b, g_post):
    n, d = x.shape
    tm = min(ROW_TILE, n)
    tn = 512
    return pl.pallas_call(
        functools.partial(_ple_kernel, tn),
        grid=(n // tm, d // tn),
        in_specs=[pl.BlockSpec((tm, d), lambda i, j: (i, 0)),
                  pl.BlockSpec((tm, PLE_DIM), lambda i, j: (i, 0)),
                  pl.BlockSpec((1, d), lambda i, j: (0, 0)),
                  pl.BlockSpec((d, tn), lambda i, j: (0, j)),
                  pl.BlockSpec((PLE_DIM, d), lambda i, j: (0, 0)),
                  pl.BlockSpec((1, d), lambda i, j: (0, 0))],
        out_specs=pl.BlockSpec((tm, tn), lambda i, j: (i, j)),
        out_shape=jax.ShapeDtypeStruct((n, d), jnp.float32),
        scratch_shapes=[pltpu.VMEM((tm, d), jnp.bfloat16), pltpu.VMEM((tm, d), jnp.float32)],
        compiler_params=pltpu.CompilerParams(dimension_semantics=("arbitrary", "arbitrary")),
        name="ple_gate",
    )(x, ple, g_ple.reshape(1, d), wg_b, wp_b, g_post.reshape(1, d))


def _attn_kernel(heads, mode, *refs):
    if mode == 'mask':
        slopes_ref, q_ref, k_ref, v_ref, mask_ref, expand_ref, o_ref, kb_ref, vb_ref = refs
    elif mode == 'gate':
        slopes_ref, q_ref, k_ref, v_ref, expand_ref, o_ref, kb_ref, vb_ref, means_ref = refs
    else:
        slopes_ref, q_ref, k_ref, v_ref, ocmp_ref, oslc_ref, gates_ref, o_ref, kb_ref, vb_ref = refs
    s_idx = pl.program_id(1)
    i = pl.program_id(2)
    tq = ATTN_TILE
    n_heads = len(heads)
    t_total = k_ref.shape[2]
    lane = lax.broadcasted_iota(jnp.int32, (tq, LANES), 1)

    @pl.when(i == 0)
    def _():
        kb_ref[...] = k_ref[0].astype(jnp.bfloat16)
        vb_ref[...] = v_ref[0].astype(jnp.bfloat16)
        if mode == 'gate':
            sq = lax.broadcasted_iota(jnp.int32, (LANES, LANES), 1)
            means = jnp.zeros((LANES, LANES), jnp.float32)
            for n in range(t_total // MOBA_BLOCK):
                col = jnp.mean(k_ref[0, :, n * MOBA_BLOCK:(n + 1) * MOBA_BLOCK], axis=1, keepdims=True)
                means = jnp.where(sq == n, col, means)
            means_ref[...] = means.T

    sel_b = {}
    if mode == 'mask':
        for m in range(mask_ref.shape[2] // LANES):
            sel_b[m] = mask_ref[0, :, m * LANES:(m + 1) * LANES].astype(jnp.bfloat16)

    row = lax.broadcasted_iota(jnp.int32, (tq, tq), 0)
    col = lax.broadcasted_iota(jnp.int32, (tq, tq), 1)
    rc = (row - col).astype(jnp.float32)
    diag0 = pl.multiple_of(i * tq, tq)

    qbs, slopes = [], []
    for h, (q_off, kv_half, m_idx) in enumerate(heads):
        slopes.append(slopes_ref[s_idx * n_heads + h])
        chunk = q_off // LANES
        qc = q_ref[0, :, chunk * LANES:(chunk + 1) * LANES]
        if (q_off % LANES) // HEAD_DIM != kv_half:
            qc = pltpu.roll(qc, HEAD_DIM, 1)
        in_half = (lane >= HEAD_DIM) if kv_half else (lane < HEAD_DIM)
        qh = jnp.where(in_half, qc, 0.0)
        qbs.append((qh * SCALE).astype(jnp.bfloat16))
        if mode == 'gate':
            n_blk = t_total // MOBA_BLOCK
            gate = _nt(means_ref[...], qh, precision=lax.Precision.HIGHEST)[0:n_blk]
            blk = lax.broadcasted_iota(jnp.int32, (n_blk, tq), 0)
            gate = jnp.where(blk < i, gate, NEG)
            _, taken = _extract_top(gate, MOBA_TOPK)
            sel = ((taken == F32_MIN) & (gate > 0.5 * NEG)) | (blk == i)
            sel = jnp.concatenate([sel.astype(jnp.float32), jnp.zeros((LANES - n_blk, tq), jnp.float32)], axis=0)
            sel_b[m_idx] = sel.T.astype(jnp.bfloat16)

    def tile_step(start, dist, tile_keep, states):
        k_tile = kb_ref[:, pl.ds(start, tq)]
        v_tile = vb_ref[:, pl.ds(start, tq)]
        picked = {m: jnp.dot(sb, expand_ref[:, pl.ds(start, tq)], preferred_element_type=jnp.float32) > 0.5
                  for m, sb in sel_b.items()}
        new_states = []
        for h, (_, _, m_idx) in enumerate(heads):
            keep = picked[m_idx] if mode != 'window' else tile_keep
            if mode != 'window' and tile_keep is not None:
                keep = keep & tile_keep
            s = jnp.dot(qbs[h], k_tile, preferred_element_type=jnp.float32) - slopes[h] * dist
            s = jnp.where(keep, s, NEG)
            m_tile = jnp.max(s, axis=1, keepdims=True)
            if states is None:
                p = jnp.exp(s - m_tile)
                new_states += [m_tile, jnp.sum(p, axis=1, keepdims=True), _nt(p.astype(jnp.bfloat16), v_tile)]
            else:
                m_prev, l_prev, acc = states[3 * h:3 * h + 3]
                m_new = jnp.maximum(m_prev, m_tile)
                alpha = jnp.exp(m_prev - m_new)
                p = jnp.exp(s - m_new)
                new_states += [m_new, alpha * l_prev + jnp.sum(p, axis=1, keepdims=True),
                               alpha * acc + _nt(p.astype(jnp.bfloat16), v_tile)]
        return tuple(new_states)

    states = tile_step(diag0, rc, rc >= 0.0, None)

    def body(n, states):
        start = pl.multiple_of(n * tq, tq)
        dist = rc + ((i - n) * tq).astype(jnp.float32)
        tile_keep = None
        if mode == 'window':
            tile_keep = rc <= jnp.where(n == i - WINDOW // tq, 0.0, float(tq))
        return tile_step(start, dist, tile_keep, states)

    first_tile = jnp.maximum(i - WINDOW // tq, 0) if mode == 'window' else 0
    states = lax.fori_loop(first_tile, i, body, states)
    outs = []
    for h, (q_off, kv_half, _) in enumerate(heads):
        o = states[3 * h + 2] / states[3 * h + 1]
        if (q_off % LANES) // HEAD_DIM != kv_half:
            o = pltpu.roll(o, HEAD_DIM, 1)
        outs.append(o)

    for c in range(n_heads // 2):
        o = jnp.where(lane < HEAD_DIM, outs[2 * c], outs[2 * c + 1])
        if mode == 'window':
            g = gates_ref[0]

            def gate_of(branch):
                lo = g[:, 3 * (2 * c) + branch:3 * (2 * c) + branch + 1]
                hi = g[:, 3 * (2 * c + 1) + branch:3 * (2 * c + 1) + branch + 1]
                return jnp.where(lane < HEAD_DIM, lo, hi)

            sl = slice(c * LANES, (c + 1) * LANES)
            o = gate_of(0) * ocmp_ref[0, :, sl] + gate_of(1) * oslc_ref[0, :, sl] + gate_of(2) * o
        o_ref[0, :, c * LANES:(c + 1) * LANES] = o


def _attention(q, kv_t, slopes, *, mode, heads, k_row0, v_row0, mask=None, block=None, n_masks=0, combine=None):
    b, t, qcols = q.shape
    qw = (len(heads) // 2) * LANES
    n_steps = qcols // qw
    tq = ATTN_TILE
    assert mode != 'gate' or MOBA_BLOCK == tq
    q_spec = pl.BlockSpec((1, tq, qw), lambda bi, s, i: (bi, i, s))
    in_specs = [pl.BlockSpec(memory_space=pltpu.SMEM), q_spec,
                pl.BlockSpec((1, LANES, t), lambda bi, s, i: (bi, k_row0 + s, 0)),
                pl.BlockSpec((1, LANES, t), lambda bi, s, i: (bi, v_row0 + s, 0))]
    args = [slopes, q, kv_t, kv_t]
    scratch = [pltpu.VMEM((LANES, t), jnp.bfloat16), pltpu.VMEM((LANES, t), jnp.bfloat16)]
    if mode == 'mask':
        in_specs.append(pl.BlockSpec((1, tq, n_masks * LANES), lambda bi, s, i: (bi, i, s)))
        args.append(mask)
    if mode == 'window':
        in_specs += [q_spec, q_spec, pl.BlockSpec((1, tq, LANES), lambda bi, s, i: (bi, i, s))]
        args += list(combine)
    else:
        expand = (np.arange(LANES)[:, None] == (np.arange(t)[None, :] // block)).astype(np.float32)
        in_specs.append(pl.BlockSpec((LANES, t), lambda bi, s, i: (0, 0)))
        args.append(jnp.asarray(expand, jnp.bfloat16))
    if mode == 'gate':
        scratch.append(pltpu.VMEM((LANES, LANES), jnp.float32))
    return pl.pallas_call(
        functools.partial(_attn_kernel, heads, mode),
        grid=(b, n_steps, t // tq),
        in_specs=in_specs,
        out_specs=q_spec,
        out_shape=jax.ShapeDtypeStruct((b, t, qcols), jnp.float32),
        scratch_shapes=scratch,
        compiler_params=pltpu.CompilerParams(dimension_semantics=("arbitrary",) * 3),
        name="attention_" + mode,
    )(*args)


def _cmp_chunk_kernel(x_ref, wk_ref, wv_ref, abk_ref, abv_ref):
    @pl.when(pl.program_id(1) == 0)
    def _():
        abk_ref[...] = jnp.zeros_like(abk_ref)
        abv_ref[...] = jnp.zeros_like(abv_ref)

    x = x_ref[...]
    abk_ref[...] += jnp.dot(x[:, :NSA_KV_WIDTH].astype(jnp.bfloat16), wk_ref[0], preferred_element_type=jnp.float32)
    abv_ref[...] += jnp.dot(x[:, NSA_KV_WIDTH:].astype(jnp.bfloat16), wv_ref[0], preferred_element_type=jnp.float32)


def _cmp_chunks(rows, wk_bd, wv_bd):
    n = rows.shape[0]
    tm = next(c for c in (1024, 512, 256, 128) if n % c == 0)
    width = 2 * NSA_KV_WIDTH
    out = jax.ShapeDtypeStruct((n, width), jnp.float32)
    return pl.pallas_call(
        _cmp_chunk_kernel,
        grid=(n // tm, CMP_STRIDE),
        in_specs=[pl.BlockSpec((tm, width), lambda i, r: (i, r)),
                  pl.BlockSpec((1, NSA_KV_WIDTH, width), lambda i, r: (r, 0, 0)),
                  pl.BlockSpec((1, NSA_KV_WIDTH, width), lambda i, r: (r, 0, 0))],
        out_specs=[pl.BlockSpec((tm, width), lambda i, r: (i, 0)), pl.BlockSpec((tm, width), lambda i, r: (i, 0))],
        out_shape=[out, out],
        compiler_params=pltpu.CompilerParams(dimension_semantics=("arbitrary", "arbitrary")),
        name="cmp_chunks",
    )(rows, wk_bd, wv_bd)


def _cmp_weights(w1, w2, pe):
    w1r = w1.reshape(2, CMP_STRIDE, HEAD_DIM, CMP_HIDDEN)
    eye = jnp.eye(NSA_KV_HEADS, dtype=w1.dtype)
    w1_bd = jnp.einsum('gG,ardh->rgdaGh', eye, w1r).reshape(CMP_STRIDE, NSA_KV_WIDTH, 2 * NSA_KV_WIDTH)
    w2_bd = jnp.kron(eye, w2)
    bias = jnp.dot(pe.reshape(1, CMP_LEN * HEAD_DIM), w1, precision=lax.Precision.HIGHEST)
    return w1_bd.astype(jnp.bfloat16), w2_bd.astype(jnp.bfloat16), jnp.tile(bias, (1, NSA_KV_HEADS))


def _cmp_second_layer(ab, peb_ref, w2_ref):
    kvw = NSA_KV_WIDTH
    h = jax.nn.gelu(ab[:, :kvw] + pltpu.roll(ab[:, kvw:], N_CMP_CHUNKS - 1, 0) + peb_ref[...])
    return jnp.dot(h.astype(jnp.bfloat16), w2_ref[...], preferred_element_type=jnp.float32)


def _cmp_keys(ab, peb_ref, w2_ref, gk_ref, bd_ref):
    kc = _cmp_second_layer(ab, peb_ref, w2_ref)
    ms = jnp.dot(kc * kc, bd_ref[...], precision=lax.Precision.HIGHEST, preferred_element_type=jnp.float32)
    return kc * lax.rsqrt(ms + RMS_EPS) * gk_ref[...]


def _cmp_select_kernel(slopes_ref, q_ref, abk_ref, abv_ref, pebk_ref, pebv_ref, w2k_ref, w2v_ref, gk_ref, bd_ref,
                       ov_ref, o_ref, mask_ref, kcb_ref, vcb_ref):
    i = pl.program_id(1)
    tq = ATTN_TILE

    @pl.when(i == 0)
    def _():
        kcb_ref[...] = _cmp_keys(abk_ref[0], pebk_ref, w2k_ref, gk_ref, bd_ref).astype(jnp.bfloat16)
        vcb_ref[...] = _cmp_second_layer(abv_ref[0], pebv_ref, w2v_ref).astype(jnp.bfloat16)

    lane = lax.broadcasted_iota(jnp.int32, (tq, LANES), 1)
    t_pos = i * tq + lax.broadcasted_iota(jnp.int32, (tq, LANES), 0)
    dist_i = t_pos - (CMP_STRIDE * lane + CMP_LEN - 1)
    keep = (dist_i >= 0) & (lane < N_CMP_CHUNKS - 1)
    dist = dist_i.astype(jnp.float32)
    own = jnp.right_shift(t_pos, SEL_BLOCK.bit_length() - 1)
    outs = []
    for g in range(NSA_KV_HEADS):
        kchunk, khalf = divmod(g, 2)
        kc_g = kcb_ref[:, kchunk * LANES:(kchunk + 1) * LANES]
        vc_g = vcb_ref[:, kchunk * LANES:(kchunk + 1) * LANES]
        in_half = (lane >= HEAD_DIM) if khalf else (lane < HEAD_DIM)
        psum = None
        for r in range(NSA_GROUP):
            h = g * NSA_GROUP + r
            qc = q_ref[0, :, (h // 2) * LANES:(h // 2 + 1) * LANES]
            if h % 2 != khalf:
                qc = pltpu.roll(qc, HEAD_DIM, 1)
            qb = (jnp.where(in_half, qc, 0.0) * SCALE).astype(jnp.bfloat16)
            s = jnp.where(keep, _nt(qb, kc_g) - slopes_ref[h] * dist, NEG)
            m = jnp.max(s, axis=1, keepdims=True)
            e = jnp.where(keep, jnp.exp(s - m), 0.0)
            l = jnp.sum(e, axis=1, keepdims=True)
            p = e / jnp.where(l > 0.0, l, 1.0)
            o = jnp.dot(p.astype(jnp.bfloat16), vc_g, preferred_element_type=jnp.float32)
            if h % 2 != khalf:
                o = pltpu.roll(o, HEAD_DIM, 1)
            outs.append(o)
            psum = p if psum is None else psum + p
        imp = jnp.dot(psum, ov_ref[...], precision=lax.Precision.HIGHEST, preferred_element_type=jnp.float32)
        score = jnp.where(lane == own, OWN_SCORE, jnp.where(lane < own, imp, NEG))
        mask_ref[0, :, g * LANES:(g + 1) * LANES] = _top_lanes(score, SEL_TOPK)
    for c in range(NSA_HEADS // 2):
        o_ref[0, :, c * LANES:(c + 1) * LANES] = jnp.where(lane < HEAD_DIM, outs[2 * c], outs[2 * c + 1])


def _cmp_consts(cw):
    c_start = np.arange(LANES) * CMP_STRIDE
    j_start = np.arange(LANES) * SEL_BLOCK
    ov = ((c_start[:, None] < j_start[None, :] + SEL_BLOCK) & (c_start[:, None] + CMP_LEN > j_start[None, :])
          & (np.arange(LANES)[:, None] < N_CMP_CHUNKS - 1)).astype(np.float32)
    bd = np.kron(np.eye(NSA_KV_HEADS, dtype=np.float32), np.full((HEAD_DIM, HEAD_DIM), 1.0 / HEAD_DIM, np.float32))
    return [cw['pebk'], cw['pebv'], cw['w2k'], cw['w2v'], cw['gk'], jnp.asarray(bd), jnp.asarray(ov)]


def _cmp_select(q, abk, abv, slopes, cw):
    b, t, _ = q.shape
    tq = ATTN_TILE
    kvw = NSA_KV_WIDTH
    consts = _cmp_consts(cw)
    return pl.pallas_call(
        _cmp_select_kernel,
        grid=(b, t // tq),
        in_specs=[pl.BlockSpec(memory_space=pltpu.SMEM),
                  pl.BlockSpec((1, tq, NSA_WIDTH), lambda bi, i: (bi, i, 0)),
                  pl.BlockSpec((1, N_CMP_CHUNKS, 2 * kvw), lambda bi, i: (bi, 0, 0)),
                  pl.BlockSpec((1, N_CMP_CHUNKS, 2 * kvw), lambda bi, i: (bi, 0, 0))]
                 + [pl.BlockSpec(c.shape, lambda bi, i: (0, 0)) for c in consts],
        out_specs=[pl.BlockSpec((1, tq, NSA_WIDTH), lambda bi, i: (bi, i, 0)),
                   pl.BlockSpec((1, tq, NSA_KV_HEADS * LANES), lambda bi, i: (bi, i, 0))],
        out_shape=[jax.ShapeDtypeStruct((b, t, NSA_WIDTH), jnp.float32),
                   jax.ShapeDtypeStruct((b, t, NSA_KV_HEADS * LANES), jnp.float32)],
        scratch_shapes=[pltpu.VMEM((N_CMP_CHUNKS, kvw), jnp.bfloat16), pltpu.VMEM((N_CMP_CHUNKS, kvw), jnp.bfloat16)],
        compiler_params=pltpu.CompilerParams(dimension_semantics=("arbitrary", "arbitrary")),
        name="cmp_select",
    )(slopes, q, abk, abv, *consts)


def _pages_chunks_kernel(pt_ref, *refs):
    x_refs, (o_ref, nat_ref) = refs[:-2], refs[-2:]
    n_col = nat_ref.shape[0] // PAGE_SIZE
    width = n_col * LANES
    for j, x_ref in enumerate(x_refs):
        for c in range(n_col):
            nat_ref[c * PAGE_SIZE:(c + 1) * PAGE_SIZE, :] = x_ref[0, c * LANES:(c + 1) * LANES, :].T
        for r in range(CMP_STRIDE):
            for c in range(n_col):
                o_ref[j * CHUNKS_PER_PAGE:(j + 1) * CHUNKS_PER_PAGE, r * width + c * LANES:r * width + (c + 1) * LANES] = (
                    nat_ref[pl.ds(c * PAGE_SIZE + r, CHUNKS_PER_PAGE, stride=CMP_STRIDE), :])


def _pages_chunks(page_table, cache_t):
    nseq = page_table.shape[0]
    per = 8
    width = 2 * NSA_KV_WIDTH
    groups = N_PAGES // per

    def page_index(j):
        return lambda b, s, pt: (pt[b, s * per + j], 0, 0)

    grid_spec = pltpu.PrefetchScalarGridSpec(
        num_scalar_prefetch=1,
        grid=(nseq, groups),
        in_specs=[pl.BlockSpec((1, width, PAGE_SIZE), page_index(j)) for j in range(per)],
        out_specs=pl.BlockSpec((per * CHUNKS_PER_PAGE, CMP_STRIDE * width), lambda b, s, pt: (b * groups + s, 0)),
        scratch_shapes=[pltpu.VMEM((width // LANES * PAGE_SIZE, LANES), jnp.float32)])
    return pl.pallas_call(
        _pages_chunks_kernel,
        grid_spec=grid_spec,
        out_shape=jax.ShapeDtypeStruct((nseq * N_CMP_CHUNKS, CMP_STRIDE * width), jnp.float32),
        compiler_params=pltpu.CompilerParams(dimension_semantics=("arbitrary", "arbitrary")),
        name="pages_chunks",
    )(page_table, *([cache_t] * per))


def _one_query_softmax(s, keep, s_own):
    s = jnp.where(keep, s, NEG)
    m = jnp.maximum(jnp.max(s, axis=1, keepdims=True), s_own)
    e = jnp.where(keep, jnp.exp(s - m), 0.0)
    e_own = jnp.exp(s_own - m)
    inv = 1.0 / (jnp.sum(e, axis=1, keepdims=True) + e_own)
    return e * inv, e_own * inv


def _own_score(qb, k_row):
    return jnp.sum(qb.astype(jnp.float32) * k_row.astype(jnp.bfloat16).astype(jnp.float32), axis=1, keepdims=True)


def _nsa_sample_kernel(pt_ref, qb_ref, *refs):
    pps = PAGES_PER_STEP
    page_refs, abk_refs, abv_refs = refs[:pps], refs[pps:2 * pps], refs[2 * pps:3 * pps]
    (win_ref, new_ref, gates_ref, slopes_ref, pebk_ref, pebv_ref, w2k_ref, w2v_ref, gk_ref, bd_ref, ov_ref,
     grp_ref, eslc_ref, hmask_ref, o_ref, s_all, v_all, abk_all, abv_all) = refs[3 * pps:]
    p = pl.program_id(1)
    kvw = NSA_KV_WIDTH
    qb = qb_ref[0].astype(jnp.bfloat16)
    for j in range(pps):
        lane0 = pl.multiple_of((p * pps + j) * PAGE_SIZE, PAGE_SIZE)
        page = page_refs[j][0]
        s_all[:, pl.ds(lane0, PAGE_SIZE)] = jnp.dot(qb, page[:kvw].astype(jnp.bfloat16),
                                                    preferred_element_type=jnp.float32)
        v_all[:, pl.ds(lane0, PAGE_SIZE)] = page[kvw:].astype(jnp.bfloat16)
        c0 = pl.multiple_of((p * pps + j) * CHUNKS_PER_PAGE, CHUNKS_PER_PAGE)
        abk_all[pl.ds(c0, CHUNKS_PER_PAGE), :] = abk_refs[j][0]
        abv_all[pl.ds(c0, CHUNKS_PER_PAGE), :] = abv_refs[j][0]

    @pl.when(p == N_PAGES // pps - 1)
    def _():
        slopes = slopes_ref[...]
        new = new_ref[0]
        lane = lax.broadcasted_iota(jnp.int32, (NSA_HEADS, LANES), 1)

        def fold(full):
            m = full * hmask_ref[...]
            return m[0:4] + m[4:8] + m[8:12] + m[12:16]

        kc = _cmp_keys(abk_all[...], pebk_ref, w2k_ref, gk_ref, bd_ref)
        vc = _cmp_second_layer(abv_all[...], pebv_ref, w2v_ref)
        dist_c = (PAST_LEN - (CMP_LEN - 1) - CMP_STRIDE * lane).astype(jnp.float32)
        keep_c = lane < N_CMP_CHUNKS - 1
        s = jnp.where(keep_c, _nt(qb, kc.astype(jnp.bfloat16)) - slopes * dist_c, NEG)
        e = jnp.where(keep_c, jnp.exp(s - jnp.max(s, axis=1, keepdims=True)), 0.0)
        p_cmp = e / jnp.sum(e, axis=1, keepdims=True)
        o_cmp = jnp.dot(p_cmp.astype(jnp.bfloat16), vc.astype(jnp.bfloat16), preferred_element_type=jnp.float32)

        imp = jnp.dot(p_cmp, ov_ref[...], precision=lax.Precision.HIGHEST, preferred_element_type=jnp.float32)
        imp = jnp.dot(grp_ref[...], imp, precision=lax.Precision.HIGHEST, preferred_element_type=jnp.float32)
        sel = _top_lanes(jnp.where(lane < PAST_LEN // SEL_BLOCK, imp, NEG), SEL_TOPK - 1)
        picked = jnp.dot(sel.astype(jnp.bfloat16), eslc_ref[...], preferred_element_type=jnp.float32) > 0.5
        kpos = lax.broadcasted_iota(jnp.int32, (NSA_HEADS, PAST_LEN), 1)
        s = s_all[...] - slopes * (PAST_LEN - kpos).astype(jnp.float32)
        prob, p_own = _one_query_softmax(s, picked, _own_score(qb, new[:, 0:kvw]))
        o_slc = _nt(prob.astype(jnp.bfloat16), v_all[...]) + p_own * new[:, kvw:2 * kvw]

        win = win_ref[0]
        wpos = lax.broadcasted_iota(jnp.int32, (NSA_HEADS, WINDOW), 1)
        s = (jnp.dot(qb, win[:kvw].astype(jnp.bfloat16), preferred_element_type=jnp.float32)
             - slopes * (WINDOW - wpos).astype(jnp.float32))
        prob, p_own = _one_query_softmax(s, wpos >= 0, _own_score(qb, new[:, 2 * kvw:3 * kvw]))
        o_win = _nt(prob.astype(jnp.bfloat16), win[kvw:].astype(jnp.bfloat16)) + p_own * new[:, 3 * kvw:4 * kvw]

        g = gates_ref[0]
        o_ref[0] = g[0] * fold(o_cmp) + g[1] * fold(o_slc) + g[2] * fold(o_win)


def _moba_sample_kernel(pt_ref, qb_ref, *refs):
    pps = PAGES_PER_STEP
    page_refs = refs[:pps]
    new_ref, slopes_ref, emoba_ref, hmask_ref, o_ref, s_all, v_all, ksum_ref = refs[pps:]
    p = pl.program_id(1)
    w = MOBA_WIDTH
    qf = qb_ref[0]
    qb = qf.astype(jnp.bfloat16)

    per_blk = MOBA_BLOCK // PAGE_SIZE
    k_sum = None
    for j in range(pps):
        page_no = p * pps + j
        lane0 = pl.multiple_of(page_no * PAGE_SIZE, PAGE_SIZE)
        page = page_refs[j][0]
        k_t = page[:w]
        s_all[:, pl.ds(lane0, PAGE_SIZE)] = jnp.dot(qb, k_t.astype(jnp.bfloat16), preferred_element_type=jnp.float32)
        v_all[:, pl.ds(lane0, PAGE_SIZE)] = page[w:].astype(jnp.bfloat16)
        k_sum = k_t if j % per_blk == 0 else k_sum + k_t
        if j % per_blk == per_blk - 1:
            ksum_ref[p * (pps // per_blk) + j // per_blk] = k_sum

    @pl.when(p == N_PAGES // pps - 1)
    def _():
        slopes = slopes_ref[...]
        new = new_ref[0]
        lane = lax.broadcasted_iota(jnp.int32, (MOBA_HEADS, LANES), 1)
        sq = lax.broadcasted_iota(jnp.int32, (w, LANES), 1)
        ksum = jnp.zeros((w, LANES), jnp.float32)
        for n in range(PAST_LEN // MOBA_BLOCK):
            ksum = jnp.where(sq == n, jnp.sum(ksum_ref[n], axis=1, keepdims=True), ksum)
        gate = jnp.dot(qf, ksum, precision=lax.Precision.HIGHEST, preferred_element_type=jnp.float32)
        sel = _top_lanes(jnp.where(lane < PAST_LEN // MOBA_BLOCK, gate, NEG), MOBA_TOPK)
        picked = jnp.dot(sel.astype(jnp.bfloat16), emoba_ref[...], preferred_element_type=jnp.float32) > 0.5
        kpos = lax.broadcasted_iota(jnp.int32, (MOBA_HEADS, PAST_LEN), 1)
        s = s_all[...] - slopes * (PAST_LEN - kpos).astype(jnp.float32)
        prob, p_own = _one_query_softmax(s, picked, _own_score(qb, new[:, :w]))
        full = _nt(prob.astype(jnp.bfloat16), v_all[...]) + p_own * new[:, w:]
        o_ref[0] = jnp.sum(full * hmask_ref[...], axis=0, keepdims=True)


def _head_rows(q, lane_head):
    n = int(max(lane_head)) + 1
    onehot = jnp.asarray(np.eye(n, dtype=np.float32)[np.asarray(lane_head)])
    return jnp.einsum('bhd,hn->bhnd', q, onehot).reshape(q.shape[0], q.shape[1], n * HEAD_DIM)


def _block_expand_t(block):
    e = (np.arange(LANES)[:, None] == (np.arange(PAST_LEN)[None, :] // block)).astype(np.float32)
    return jnp.asarray(e, jnp.bfloat16)


def _const_spec(shape):
    nd = len(shape)
    return pl.BlockSpec(shape, lambda b, p, pt: (0,) * nd)


def _page_index(j, row_block):
    return lambda b, p, pt: (pt[b, p * PAGES_PER_STEP + j], row_block, 0)


def _context_page_index(j):
    return lambda b, p, pt: (b * N_PAGES + p * PAGES_PER_STEP + j, 0, 0)


def _nsa_sample(page_table, q, cache_t, abk, abv, win_t, new, gates, slopes, cw):
    nseq = q.shape[0]
    kvw = NSA_KV_WIDTH
    pps = PAGES_PER_STEP
    group_of_head = [h // NSA_GROUP for h in range(NSA_HEADS)]
    qb = _head_rows(q * SCALE, group_of_head)
    hmask = _head_rows(jnp.ones((1, NSA_HEADS, HEAD_DIM), jnp.float32), group_of_head)[0]
    hh = np.arange(NSA_HEADS)
    grp = ((hh[:, None] // NSA_GROUP) == (hh[None, :] // NSA_GROUP)).astype(np.float32)
    consts = [slopes.reshape(NSA_HEADS, 1)] + _cmp_consts(cw) + [jnp.asarray(grp), _block_expand_t(SEL_BLOCK), hmask]
    grid_spec = pltpu.PrefetchScalarGridSpec(
        num_scalar_prefetch=1,
        grid=(nseq, N_PAGES // pps),
        in_specs=[pl.BlockSpec((1, NSA_HEADS, kvw), lambda b, p, pt: (b, 0, 0))]
                 + [pl.BlockSpec((1, 2 * kvw, PAGE_SIZE), _page_index(j, 1)) for j in range(pps)]
                 + [pl.BlockSpec((1, CHUNKS_PER_PAGE, 2 * kvw), _context_page_index(j)) for j in range(pps)] * 2
                 + [pl.BlockSpec((1, 2 * kvw, WINDOW), lambda b, p, pt: (b, 0, 0)),
                    pl.BlockSpec((1, 1, 4 * kvw), lambda b, p, pt: (b, 0, 0)),
                    pl.BlockSpec((1, 3, NSA_GROUP, kvw), lambda b, p, pt: (b, 0, 0, 0))]
                 + [_const_spec(c.shape) for c in consts],
        out_specs=pl.BlockSpec((1, NSA_GROUP, kvw), lambda b, p, pt: (b, 0, 0)),
        scratch_shapes=[pltpu.VMEM((NSA_HEADS, PAST_LEN), jnp.float32),
                        pltpu.VMEM((kvw, PAST_LEN), jnp.bfloat16),
                        pltpu.VMEM((N_CMP_CHUNKS, 2 * kvw), jnp.float32),
                        pltpu.VMEM((N_CMP_CHUNKS, 2 * kvw), jnp.float32)])
    return pl.pallas_call(
        _nsa_sample_kernel,
        grid_spec=grid_spec,
        out_shape=jax.ShapeDtypeStruct((nseq, NSA_GROUP, kvw), jnp.float32),
        compiler_params=pltpu.CompilerParams(dimension_semantics=("arbitrary", "arbitrary")),
        name="nsa_sample",
    )(page_table, qb, *([cache_t] * pps + [abk] * pps + [abv] * pps), win_t, new, gates, *consts)


def _moba_sample(page_table, q, cache_t, new, slopes):
    nseq = q.shape[0]
    w = MOBA_WIDTH
    pps = PAGES_PER_STEP
    own_head = list(range(MOBA_HEADS))
    qb = _head_rows(q * SCALE, own_head)
    hmask = _head_rows(jnp.ones((1, MOBA_HEADS, HEAD_DIM), jnp.float32), own_head)[0]
    consts = [slopes.reshape(MOBA_HEADS, 1), _block_expand_t(MOBA_BLOCK), hmask]
    grid_spec = pltpu.PrefetchScalarGridSpec(
        num_scalar_prefetch=1,
        grid=(nseq, N_PAGES // pps),
        in_specs=[pl.BlockSpec((1, MOBA_HEADS, w), lambda b, p, pt: (b, 0, 0))]
                 + [pl.BlockSpec((1, 2 * w, PAGE_SIZE), _page_index(j, 0)) for j in range(pps)]
                 + [pl.BlockSpec((1, 1, 2 * w), lambda b, p, pt: (b, 0, 0))]
                 + [_const_spec(c.shape) for c in consts],
        out_specs=pl.BlockSpec((1, 1, w), lambda b, p, pt: (b, 0, 0)),
        scratch_shapes=[pltpu.VMEM((MOBA_HEADS, PAST_LEN), jnp.float32),
                        pltpu.VMEM((w, PAST_LEN), jnp.bfloat16),
                        pltpu.VMEM((PAST_LEN // MOBA_BLOCK, w, LANES), jnp.float32)])
    return pl.pallas_call(
        _moba_sample_kernel,
        grid_spec=grid_spec,
        out_shape=jax.ShapeDtypeStruct((nseq, 1, w), jnp.float32),
        compiler_params=pltpu.CompilerParams(dimension_semantics=("arbitrary", "arbitrary"),
                                             vmem_limit_bytes=VMEM_LIMIT),
        name="moba_sample",
    )(page_table, qb, *([cache_t] * pps), new, *consts)


def _pick_top(s_ref, vals_ref, rows_ref, k):
    r_total, width = s_ref.shape
    cw = min(2 * LANES, width)
    kid = lax.broadcasted_iota(jnp.int32, (k, cw), 0)
    rowid = lax.broadcasted_iota(jnp.int32, (r_total, cw), 0)

    def one_pass(ci, _):
        col = pl.multiple_of(ci * cw, cw)

        def body(r, carry):
            w, vals, rows = carry
            mx = jnp.max(w, axis=0, keepdims=True)
            first = jnp.min(jnp.where(w == mx, rowid, r_total), axis=0, keepdims=True)
            w = jnp.where(rowid == first, F32_MIN, w)
            return w, jnp.where(kid == r, mx, vals), jnp.where(kid == r, first, rows)

        init = (s_ref[:, pl.ds(col, cw)], jnp.zeros((k, cw), jnp.float32), jnp.zeros((k, cw), jnp.int32))
        _, vals, rows = lax.fori_loop(0, k, body, init)
        vals_ref[:, pl.ds(col, cw)] = vals
        rows_ref[:, pl.ds(col, cw)] = rows
        return 0

    lax.fori_loop(0, width // cw, one_pass, 0)
    return vals_ref[...], rows_ref[...]


def _peer_route_kernel(x_ref, g_ref, wq_ref, subk_ref, xt_ref, c0_ref, e0_ref, r1_ref, e1_ref, sc_ref, cand_ref,
                       vals_ref, rows_ref, best_ref, crows_ref):
    x = x_ref[...]
    xn = x * lax.rsqrt(jnp.mean(x * x, axis=-1, keepdims=True) + RMS_EPS) * g_ref[...]
    xb = xn.astype(jnp.bfloat16)
    xt_ref[...] = xn.T.astype(jnp.bfloat16)
    t = x.shape[0]
    q = jnp.dot(xb, wq_ref[...], preferred_element_type=jnp.float32)
    half = PEER_QDIM // 2
    k = PEER_TOPK
    for j in range(2 * PEER_HEADS):
        sc_ref[:, j * t:(j + 1) * t] = _nt(q[:, j * half:(j + 1) * half], subk_ref[j],
                                           precision=lax.Precision.HIGHEST).T
    vals, rows = _pick_top(sc_ref, vals_ref, rows_ref, k)
    for p in range(PEER_HEADS):
        av = vals[:, (2 * p) * t:(2 * p + 1) * t]
        bv = vals[:, (2 * p + 1) * t:(2 * p + 2) * t]
        for a in range(k):
            lo, n_b = PEER_CAND_START[a], PEER_CAND_START[a + 1] - PEER_CAND_START[a]
            cand_ref[lo:lo + n_b, p * t:(p + 1) * t] = av[a:a + 1, :] + bv[0:n_b]
        n_real = PEER_CAND_START[k]
        cand_ref[n_real:, p * t:(p + 1) * t] = jnp.full((cand_ref.shape[0] - n_real, t), F32_MIN, jnp.float32)
    best, crows = _pick_top(cand_ref, best_ref, crows_ref, k)
    zsum = jnp.sum(jnp.exp(best - best[0:1, :]), axis=0, keepdims=True)
    win_a = jnp.zeros(crows.shape, jnp.int32)
    for a in range(1, k):
        win_a = win_a + (crows >= PEER_CAND_START[a]).astype(jnp.int32)
    keyid = lax.broadcasted_iota(jnp.int32, (PEER_KEYS, t), 0)
    for p in range(PEER_HEADS):
        lo0, lo1 = (2 * p) * t, (2 * p + 1) * t
        s0, s1 = sc_ref[:, lo0:lo0 + t], sc_ref[:, lo1:lo1 + t]
        c0 = jnp.zeros((PEER_KEYS, t), jnp.float32)
        r1 = jnp.full((PEER_KEYS, t), float(k), jnp.float32)
        for a in range(k):
            cnt_a = jnp.sum((win_a[:, p * t:(p + 1) * t] == a).astype(jnp.float32), axis=0, keepdims=True)
            c0 = jnp.where(keyid == rows[a:a + 1, lo0:lo0 + t], cnt_a, c0)
            r1 = jnp.where(keyid == rows[a:a + 1, lo1:lo1 + t], float(a), r1)
        c0_ref[p] = c0
        e0_ref[p] = jnp.where(c0 > 0.0, jnp.exp(s0 - vals[0:1, lo0:lo0 + t]), 0.0) / zsum[:, p * t:(p + 1) * t]
        r1_ref[p] = r1
        e1_ref[p] = jnp.where(r1 < k, jnp.exp(s1 - vals[0:1, lo1:lo1 + t]), 0.0)


def _peer_expert_kernel(x_ref, xt_ref, c0_ref, e0_ref, r1_ref, e1_ref, u_ref, vt_ref, o_ref, acc_ref, h_ref):
    c = pl.program_id(1)
    per = PEER_CHUNK // PEER_KEYS

    @pl.when(c == 0)
    def _():
        acc_ref[...] = jnp.zeros_like(acc_ref)

    ga = jax.nn.gelu(jnp.dot(u_ref[...], xt_ref[...], preferred_element_type=jnp.float32))
    for k in range(per):
        i1 = c * per + k
        wt = None
        for p in range(PEER_HEADS):
            row_c = c0_ref[p, pl.ds(i1, 1), :]
            row_e = e0_ref[p, pl.ds(i1, 1), :]
            term = jnp.where(r1_ref[p] < row_c, row_e * e1_ref[p], 0.0)
            wt = term if wt is None else wt + term
        rows = slice(k * PEER_KEYS, (k + 1) * PEER_KEYS)
        h_ref[rows, :] = (wt * ga[rows]).astype(jnp.bfloat16)
    acc_ref[...] += jnp.dot(vt_ref[0], h_ref[...], preferred_element_type=jnp.float32)

    @pl.when(c == pl.num_programs(1) - 1)
    def _():
        o_ref[...] = x_ref[...] + acc_ref[...].T


def _peer_residual(x, g_ffn, wq_b, subk, u_b, vt_b):
    n, d = x.shape
    t1 = 256 if n % 256 == 0 else n
    hp = PEER_HEADS
    tab = jax.ShapeDtypeStruct((hp, PEER_KEYS, n), jnp.float32)
    tab_spec = pl.BlockSpec((hp, PEER_KEYS, t1), lambda i: (0, 0, i))
    xt, c0, e0, r1, e1 = pl.pallas_call(
        _peer_route_kernel,
        grid=(n // t1,),
        in_specs=[pl.BlockSpec((t1, d), lambda i: (i, 0)),
                  pl.BlockSpec((1, d), lambda i: (0, 0)),
                  pl.BlockSpec((d, hp * PEER_QDIM), lambda i: (0, 0)),
                  pl.BlockSpec((2 * hp, PEER_KEYS, PEER_QDIM // 2), lambda i: (0, 0, 0))],
        out_specs=[pl.BlockSpec((d, t1), lambda i: (0, i)), tab_spec, tab_spec, tab_spec, tab_spec],
        out_shape=[jax.ShapeDtypeStruct((d, n), jnp.bfloat16), tab, tab, tab, tab],
        scratch_shapes=[pltpu.VMEM((PEER_KEYS, 2 * hp * t1), jnp.float32),
                        pltpu.VMEM((-(-PEER_CAND_START[PEER_TOPK] // 8) * 8, hp * t1), jnp.float32),
                        pltpu.VMEM((PEER_TOPK, 2 * hp * t1), jnp.float32),
                        pltpu.VMEM((PEER_TOPK, 2 * hp * t1), jnp.int32),
                        pltpu.VMEM((PEER_TOPK, hp * t1), jnp.float32),
                        pltpu.VMEM((PEER_TOPK, hp * t1), jnp.int32)],
        compiler_params=pltpu.CompilerParams(dimension_semantics=("arbitrary",), vmem_limit_bytes=VMEM_LIMIT),
        name="peer_route",
    )(x, g_ffn.reshape(1, d), wq_b, subk.reshape(2 * hp, PEER_KEYS, PEER_QDIM // 2))
    t2 = min(ROW_TILE, n)
    once = pl.Buffered(1)
    tab_spec2 = pl.BlockSpec((hp, PEER_KEYS, t2), lambda i, c: (0, 0, i), pipeline_mode=once)
    return pl.pallas_call(
        _peer_expert_kernel,
        grid=(n // t2, PEER_EXPERTS // PEER_CHUNK),
        in_specs=[pl.BlockSpec((t2, d), lambda i, c: (i, 0), pipeline_mode=once),
                  pl.BlockSpec((d, t2), lambda i, c: (0, i), pipeline_mode=once),
                  tab_spec2, tab_spec2, tab_spec2, tab_spec2,
                  pl.BlockSpec((PEER_CHUNK, d), lambda i, c: (c, 0)),
                  pl.BlockSpec((1, d, PEER_CHUNK), lambda i, c: (c, 0, 0))],
        out_specs=pl.BlockSpec((t2, d), lambda i, c: (i, 0)),
        out_shape=jax.ShapeDtypeStruct((n, d), jnp.float32),
        scratch_shapes=[pltpu.VMEM((d, t2), jnp.float32), pltpu.VMEM((PEER_CHUNK, t2), jnp.bfloat16)],
        compiler_params=pltpu.CompilerParams(dimension_semantics=("arbitrary", "arbitrary"),
                                             vmem_limit_bytes=VMEM_LIMIT),
        name="peer_experts",
    )(x, xt, c0, e0, r1, e1, u_b, vt_b)


def _alibi_slopes():
    n = NSA_HEADS + MOBA_HEADS
    s = jnp.exp2(-8.0 * jnp.arange(1, n + 1, dtype=jnp.float32) / n)
    return s[0::2], s[1::2]


def _in_proj_weights(w_in, gains):
    w_t = jnp.transpose(w_in).astype(jnp.bfloat16)
    kvw = NSA_KV_WIDTH
    edges = np.cumsum([0, NSA_WIDTH] + [kvw] * 6 + [N_GATE] + [MOBA_WIDTH] * 3)
    q_n, kc, vc, ks, vs, kw, vw, gt, q_m, k_m, v_m = [w_t[a:b] for a, b in zip(edges[:-1], edges[1:])]

    def gain(key, n_heads):
        return jnp.tile(gains[key].reshape(HEAD_DIM), n_heads)

    def raw(n):
        return jnp.ones((n,), jnp.float32)

    def pack(parts, gain_parts, flags):
        flag = jnp.concatenate([jnp.full((p.shape[0],), f, jnp.float32) for p, f in zip(parts, flags)])
        return jnp.concatenate(parts, axis=0), jnp.concatenate(gain_parts).reshape(-1, 1), flag.reshape(-1, 1)

    per_step = N_GATE // 2
    pad = jnp.zeros((LANES - per_step, D_MODEL), jnp.bfloat16)
    gt_rows = [gt[:per_step], pad, gt[per_step:], pad]
    return {
        'q_n': pack([q_n], [gain('g_q_nsa', NSA_HEADS)], [1.0]),
        'nsa': pack([kc, vc, ks, vs], [raw(kvw), raw(kvw), gain('g_k_slc', NSA_KV_HEADS), raw(kvw)], [0., 0., 1., 0.]),
        'win': pack([kw, vw], [gain('g_k_win', NSA_KV_HEADS), raw(kvw)], [1., 0.]),
        'cmp': pack([kc, vc], [raw(kvw), raw(kvw)], [0., 0.]),
        'gates': pack(gt_rows, [raw(2 * LANES)], [0., 0., 0., 0.]),
        'q_m': pack([q_m], [gain('g_q_moba', MOBA_HEADS)], [1.0]),
        'moba': pack([k_m, v_m], [gain('g_k_moba', MOBA_HEADS), raw(MOBA_WIDTH)], [1., 0.]),
    }


def _in_proj(x, g_mix, pw, batch):
    h_t = _norm_transpose(x, g_mix)
    out = {}
    for key in ('nsa', 'win', 'moba'):
        out[key] = _proj_t(h_t, *pw[key], batch=batch, rows=512)
    for key in ('q_n', 'q_m', 'cmp'):
        out[key] = _proj_t(h_t, *pw[key], batch=batch, rows=512, natural=True)
    out['gates'] = _proj_t(h_t, *pw['gates'], batch=batch, rows=2 * LANES, natural=True, act='sigmoid')
    return out


def _rows_leaf(rows_t, kinds, heads):
    b, _, t = rows_t.shape
    return jnp.transpose(rows_t.reshape(b, kinds, heads, HEAD_DIM, t), (0, 4, 1, 2, 3))[None]


def kernel(x_prompt, x_sample, cache_nsa, cache_moba, state_win, page_table, p_prompt, p_sample,
           g_mix, w_in, g_q_nsa, g_k_cmp, g_k_slc, g_k_win, g_q_moba, g_k_moba,
           cmp_pe_k, cmp_w1_k, cmp_w2_k, cmp_pe_v, cmp_w1_v, cmp_w2_v,
           g_out_nsa, g_out_moba, w_out, g_ffn, w_peer_q, peer_sub_keys, peer_u, peer_v,
           g_ple, w_ple_gate, w_ple_proj, g_ple_post):
    assert w_in.shape[0] == 1, "single-layer trunk"
    slopes_nsa, slopes_moba = _alibi_slopes()
    kvw = NSA_KV_WIDTH
    bf = jnp.bfloat16
    pw = _in_proj_weights(w_in[0], {'g_q_nsa': g_q_nsa[0], 'g_k_slc': g_k_slc[0], 'g_k_win': g_k_win[0],
                                    'g_q_moba': g_q_moba[0], 'g_k_moba': g_k_moba[0]})
    cw = {}
    w1k_bd, cw['w2k'], cw['pebk'] = _cmp_weights(cmp_w1_k[0], cmp_w2_k[0], cmp_pe_k[0])
    w1v_bd, cw['w2v'], cw['pebv'] = _cmp_weights(cmp_w1_v[0], cmp_w2_v[0], cmp_pe_v[0])
    cw['gk'] = jnp.tile(g_k_cmp[0].reshape(1, HEAD_DIM), (1, NSA_KV_HEADS))
    w_out_b, wq_b, u_b = w_out[0].astype(bf), w_peer_q[0].astype(bf), peer_u[0].astype(bf)
    vt_b = jnp.transpose(peer_v[0].reshape(PEER_EXPERTS // PEER_CHUNK, PEER_CHUNK, D_MODEL), (0, 2, 1)).astype(bf)
    wg_b, wp_b = w_ple_gate[0].astype(bf), w_ple_proj[0].astype(bf)

    def tail(x, o_nsa, o_moba, ple):
        x = _out_proj(o_nsa, o_moba, x, g_out_nsa[0], g_out_moba[0], w_out_b)
        x = _peer_residual(x, g_ffn[0], wq_b, peer_sub_keys[0], u_b, vt_b)
        return _ple(x, ple, g_ple[0], wg_b, wp_b, g_ple_post[0])

    b, t, d = x_prompt.shape
    xp = x_prompt.reshape(b * t, d)
    pr = _in_proj(xp, g_mix[0], pw, b)
    q_n = pr['q_n'].reshape(b, t, NSA_WIDTH)
    abk, abv = _cmp_chunks(pr['cmp'].reshape(b * t // CMP_STRIDE, CMP_STRIDE * 2 * kvw), w1k_bd, w1v_bd)
    o_cmp, mask = _cmp_select(q_n, abk.reshape(b, t // CMP_STRIDE, 2 * kvw), abv.reshape(b, t // CMP_STRIDE, 2 * kvw),
                              slopes_nsa, cw)
    o_slc = _attention(q_n, pr['nsa'], slopes_nsa, mode='mask', heads=NSA_STEP_HEADS, k_row0=4, v_row0=6,
                       mask=mask, block=SEL_BLOCK, n_masks=2)
    o_nsa = _attention(q_n, pr['win'], slopes_nsa, mode='window', heads=NSA_STEP_HEADS, k_row0=0, v_row0=2,
                       combine=(o_cmp, o_slc, pr['gates'].reshape(b, t, 2 * LANES)))
    o_moba = _attention(pr['q_m'].reshape(b, t, MOBA_WIDTH), pr['moba'], slopes_moba, mode='gate',
                        heads=MOBA_STEP_HEADS, k_row0=0, v_row0=MOBA_WIDTH // LANES, block=MOBA_BLOCK)
    y_prompt = tail(xp, o_nsa.reshape(b * t, NSA_WIDTH), o_moba.reshape(b * t, MOBA_WIDTH),
                    p_prompt[0].reshape(b * t, PLE_DIM)).reshape(b, t, d)

    ns = x_sample.shape[0]
    xs = x_sample.reshape(ns, d)
    sr = _in_proj(xs, g_mix[0], pw, 1)
    n_phys = cache_nsa.shape[1]
    cache_n_t = jnp.transpose(cache_nsa[0], (0, 2, 3, 4, 1)).reshape(n_phys, 4 * kvw, PAGE_SIZE)
    cache_m_t = jnp.transpose(cache_moba[0], (0, 2, 3, 4, 1)).reshape(n_phys, 2 * MOBA_WIDTH, PAGE_SIZE)
    win_t = jnp.transpose(state_win[0], (0, 2, 3, 4, 1)).reshape(ns, 2 * kvw, WINDOW)
    abk, abv = _cmp_chunks(_pages_chunks(page_table, cache_n_t), w1k_bd, w1v_bd)
    nsa_new = jnp.transpose(sr['nsa'][0])
    win_new = jnp.transpose(sr['win'][0])
    moba_new = jnp.transpose(sr['moba'][0])
    new_n = jnp.concatenate([nsa_new[:, 2 * kvw:], win_new], axis=1)[:, None, :]
    gs = sr['gates']
    gs = jnp.concatenate([gs[:, :N_GATE // 2], gs[:, LANES:LANES + N_GATE // 2]], axis=1)
    g3 = jnp.transpose(gs.reshape(ns, NSA_KV_HEADS, NSA_GROUP, 3), (0, 3, 2, 1))[..., None]
    g3 = jnp.broadcast_to(g3, (ns, 3, NSA_GROUP, NSA_KV_HEADS, HEAD_DIM)).reshape(ns, 3, NSA_GROUP, kvw)
    o_nsa_s = _nsa_sample(page_table, sr['q_n'].reshape(ns, NSA_HEADS, HEAD_DIM), cache_n_t,
                          abk.reshape(ns * N_PAGES, CHUNKS_PER_PAGE, 2 * kvw),
                          abv.reshape(ns * N_PAGES, CHUNKS_PER_PAGE, 2 * kvw),
                          win_t, new_n, g3, slopes_nsa, cw)
    o_nsa_s = jnp.transpose(o_nsa_s.reshape(ns, NSA_GROUP, NSA_KV_HEADS, HEAD_DIM), (0, 2, 1, 3)).reshape(ns, NSA_WIDTH)
    o_moba_s = _moba_sample(page_table, sr['q_m'].reshape(ns, MOBA_HEADS, HEAD_DIM), cache_m_t,
                            moba_new[:, None, :], slopes_moba).reshape(ns, MOBA_WIDTH)
    y_sample = tail(xs, o_nsa_s, o_moba_s, p_sample[0].reshape(ns, PLE_DIM)).reshape(ns, 1, d)

    win_prompt = _rows_leaf(pr['win'][:, :, t - WINDOW:], 2, NSA_KV_HEADS)
    nsa_rows_s = nsa_new.reshape(1, ns, 1, 4, NSA_KV_HEADS, HEAD_DIM)
    moba_rows_s = moba_new.reshape(1, ns, 1, 2, MOBA_HEADS, HEAD_DIM)
    win_s = jnp.concatenate([state_win[0][:, 1:], win_new.reshape(ns, 1, 2, NSA_KV_HEADS, HEAD_DIM)], axis=1)[None]
    return (y_prompt, y_sample, _rows_leaf(pr['nsa'], 4, NSA_KV_HEADS), _rows_leaf(pr['moba'], 2, MOBA_HEADS),
            win_prompt, nsa_rows_s, moba_rows_s, win_s)
```

```python
import functools

import jax
import jax.numpy as jnp
from jax import lax
import numpy as np
from jax.experimental import pallas as pl
from jax.experimental.pallas import tpu as pltpu

D_MODEL = 2048
PAST_LEN = 2048
PAGE_SIZE = 128
HEAD_DIM = 64
NSA_HEADS = D_MODEL // (2 * HEAD_DIM)
NSA_KV_HEADS = max(1, NSA_HEADS // 4)
NSA_GROUP = NSA_HEADS // NSA_KV_HEADS
MOBA_HEADS = D_MODEL // (2 * HEAD_DIM)
NSA_WIDTH = NSA_HEADS * HEAD_DIM
NSA_KV_WIDTH = NSA_KV_HEADS * HEAD_DIM
MOBA_WIDTH = MOBA_HEADS * HEAD_DIM
N_GATE = 3 * NSA_HEADS
CMP_LEN = 32
CMP_STRIDE = 16
CMP_HIDDEN = 64
SEL_BLOCK = 64
SEL_TOPK = 8
WINDOW = 512
MOBA_BLOCK = 256
MOBA_TOPK = 3
PEER_KEYS = 128
PEER_EXPERTS = PEER_KEYS * PEER_KEYS
PEER_HEADS = 8
PEER_TOPK = 16
PEER_QDIM = 256
PLE_DIM = 256
RMS_EPS = 1e-6
NEG = -1e30
OWN_SCORE = 1e9
SCALE = HEAD_DIM ** -0.5

LANES = 128
ATTN_TILE = 256
ROW_TILE = 512
PEER_CHUNK = 1024
VMEM_LIMIT = 56 * 1024 * 1024
N_PAGES = PAST_LEN // PAGE_SIZE
PAGES_PER_STEP = 8
N_CMP_CHUNKS = PAST_LEN // CMP_STRIDE
CHUNKS_PER_PAGE = PAGE_SIZE // CMP_STRIDE
F32_MIN = float(np.finfo(np.float32).min)
PEER_CAND_START = tuple(int(v) for v in np.cumsum([0] + [PEER_TOPK // (a + 1) for a in range(PEER_TOPK)]))

MOBA_STEP_HEADS = ((0, 0, 0), (HEAD_DIM, 1, 1))
NSA_STEP_HEADS = tuple((g * NSA_GROUP * HEAD_DIM + r * HEAD_DIM, g, g) for g in range(2) for r in range(NSA_GROUP))


def _nt(a, b, **kw):
    return lax.dot_general(a, b, (((1,), (1,)), ((), ())), preferred_element_type=jnp.float32, **kw)


def _top_lanes_by_rows(score, k, n_valid):
    rows = score.shape[0]
    if rows < LANES:
        score = jnp.concatenate([score, jnp.full((LANES - rows, LANES), NEG, jnp.float32)], axis=0)
    st = score.T[0:n_valid]
    _, w = _extract_top(st, k)
    sel = ((w == F32_MIN) & (st > 0.5 * NEG)).astype(jnp.float32)
    pad = jnp.zeros((LANES - n_valid, st.shape[1]), jnp.float32)
    return jnp.concatenate([sel, pad], axis=0).T[0:rows]


def _extract_top(work, k):
    r_total, t = work.shape
    rowid = lax.broadcasted_iota(jnp.int32, (r_total, t), 0)
    kid = lax.broadcasted_iota(jnp.int32, (k, t), 0)

    def body(r, carry):
        w, vals = carry
        mx = jnp.max(w, axis=0, keepdims=True)
        first = jnp.min(jnp.where(w == mx, rowid, r_total), axis=0, keepdims=True)
        w = jnp.where(rowid == first, F32_MIN, w)
        vals = jnp.where(kid == r, mx, vals)
        return w, vals

    w, vals = lax.fori_loop(0, k, body, (work, jnp.zeros((k, t), jnp.float32)))
    return vals, w


def _norm_t_kernel(x_ref, g_ref, o_ref):
    x = x_ref[...]
    xn = x * lax.rsqrt(jnp.mean(x * x, axis=-1, keepdims=True) + RMS_EPS) * g_ref[...]
    o_ref[...] = xn.T.astype(jnp.bfloat16)


def _norm_transpose(x, g):
    n, d = x.shape
    tm = min(ROW_TILE, n)
    return pl.pallas_call(
        _norm_t_kernel,
        grid=(n // tm,),
        in_specs=[pl.BlockSpec((tm, d), lambda i: (i, 0)), pl.BlockSpec((1, d), lambda i: (0, 0))],
        out_specs=pl.BlockSpec((d, tm), lambda i: (0, i)),
        out_shape=jax.ShapeDtypeStruct((d, n), jnp.bfloat16),
        compiler_params=pltpu.CompilerParams(dimension_semantics=("arbitrary",)),
        name="norm_transpose",
    )(x, g.reshape(1, d))


def _proj_t_kernel(act, natural, w_ref, ht_ref, gain_ref, flag_ref, o_ref):
    y = jnp.dot(w_ref[...], ht_ref[...], preferred_element_type=jnp.float32)
    r, tm = y.shape
    y3 = y.reshape(r // HEAD_DIM, HEAD_DIM, tm)
    ms = jnp.mean(y3 * y3, axis=1, keepdims=True)
    yn = (y3 * lax.rsqrt(ms + RMS_EPS)).reshape(r, tm) * gain_ref[...]
    y = jnp.where(flag_ref[...] > 0.5, yn, y)
    if act == 'sigmoid':
        y = jax.nn.sigmoid(y)
    if natural:
        o_ref[...] = y.T
    else:
        o_ref[0] = y


def _proj_t(ht, w_t, gain, flag, *, batch, rows, act=None, natural=False):
    c, d = w_t.shape
    n = ht.shape[1]
    t = n // batch
    tm = min(ROW_TILE, t)
    nt = t // tm
    if natural:
        out_shape = jax.ShapeDtypeStruct((n, c), jnp.float32)
        out_spec = pl.BlockSpec((tm, rows), lambda b, i, j: (b * nt + i, j))
    else:
        out_shape = jax.ShapeDtypeStruct((batch, c, t), jnp.float32)
        out_spec = pl.BlockSpec((1, rows, tm), lambda b, i, j: (b, j, i))
    return pl.pallas_call(
        functools.partial(_proj_t_kernel, act, natural),
        grid=(batch, nt, c // rows),
        in_specs=[pl.BlockSpec((rows, d), lambda b, i, j: (j, 0)),
                  pl.BlockSpec((d, tm), lambda b, i, j: (0, b * nt + i)),
                  pl.BlockSpec((rows, 1), lambda b, i, j: (j, 0)),
                  pl.BlockSpec((rows, 1), lambda b, i, j: (j, 0))],
        out_specs=out_spec,
        out_shape=out_shape,
        compiler_params=pltpu.CompilerParams(dimension_semantics=("arbitrary",) * 3),
        name="proj_t",
    )(w_t, ht, gain, flag)


def _out_proj_kernel(on_ref, om_ref, x_ref, gn_ref, gm_ref, w_ref, o_ref, mix_ref):
    @pl.when(pl.program_id(1) == 0)
    def _():
        for src, g_ref, lo in ((on_ref, gn_ref, 0), (om_ref, gm_ref, NSA_WIDTH)):
            a = src[...]
            an = a * lax.rsqrt(jnp.mean(a * a, axis=-1, keepdims=True) + RMS_EPS) * g_ref[...]
            mix_ref[:, lo:lo + a.shape[1]] = an.astype(jnp.bfloat16)

    o_ref[...] = x_ref[...] + jnp.dot(mix_ref[...], w_ref[...], preferred_element_type=jnp.float32)


def _out_proj(o_nsa, o_moba, x, g_nsa, g_moba, w_b):
    n, d = x.shape
    tm = min(ROW_TILE, n)
    tn = 512
    return pl.pallas_call(
        _out_proj_kernel,
        grid=(n // tm, d // tn),
        in_specs=[pl.BlockSpec((tm, NSA_WIDTH), lambda i, j: (i, 0)),
                  pl.BlockSpec((tm, MOBA_WIDTH), lambda i, j: (i, 0)),
                  pl.BlockSpec((tm, tn), lambda i, j: (i, j)),
                  pl.BlockSpec((1, NSA_WIDTH), lambda i, j: (0, 0)),
                  pl.BlockSpec((1, MOBA_WIDTH), lambda i, j: (0, 0)),
                  pl.BlockSpec((NSA_WIDTH + MOBA_WIDTH, tn), lambda i, j: (0, j))],
        out_specs=pl.BlockSpec((tm, tn), lambda i, j: (i, j)),
        out_shape=jax.ShapeDtypeStruct((n, d), jnp.float32),
        scratch_shapes=[pltpu.VMEM((tm, NSA_WIDTH + MOBA_WIDTH), jnp.bfloat16)],
        compiler_params=pltpu.CompilerParams(dimension_semantics=("arbitrary", "arbitrary")),
        name="out_proj",
    )(o_nsa, o_moba, x, g_nsa.reshape(1, -1), g_moba.reshape(1, -1), w_b)


def _ple_kernel(tn, x_ref, p_ref, g_ref, wg_ref, wp_ref, gp_ref, o_ref, xn_ref, pn_ref):
    j = pl.program_id(1)

    @pl.when(j == 0)
    def _():
        x = x_ref[...]
        xn_ref[...] = (x * lax.rsqrt(jnp.mean(x * x, axis=-1, keepdims=True) + RMS_EPS) * g_ref[...]
                       ).astype(jnp.bfloat16)
        pr = jnp.dot(p_ref[...].astype(jnp.bfloat16), wp_ref[...], preferred_element_type=jnp.float32)
        pn_ref[...] = pr * lax.rsqrt(jnp.mean(pr * pr, axis=-1, keepdims=True) + RMS_EPS) * gp_ref[...]

    col = pl.multiple_of(j * tn, tn)
    gate = jax.nn.sigmoid(jnp.dot(xn_ref[...], wg_ref[...], preferred_element_type=jnp.float32))
    o_ref[...] = x_ref[:, pl.ds(col, tn)] + gate * pn_ref[:, pl.ds(col, tn)]


def _ple(x, ple, g_ple, wg_b, wp_b, g_post):
    n, d = x.shape
    tm = min(ROW_TILE, n)
    tn = 512
    return pl.pallas_call(
        functools.partial(_ple_kernel, tn),
        grid=(n // tm, d // tn),
        in_specs=[pl.BlockSpec((tm, d), lambda i, j: (i, 0)),
                  pl.BlockSpec((tm, PLE_DIM), lambda i, j: (i, 0)),
                  pl.BlockSpec((1, d), lambda i, j: (0, 0)),
                  pl.BlockSpec((d, tn), lambda i, j: (0, j)),
                  pl.BlockSpec((PLE_DIM, d), lambda i, j: (0, 0)),
                  pl.BlockSpec((1, d), lambda i, j: (0, 0))],
        out_specs=pl.BlockSpec((tm, tn), lambda i, j: (i, j)),
        out_shape=jax.ShapeDtypeStruct((n, d), jnp.float32),
        scratch_shapes=[pltpu.VMEM((tm, d), jnp.bfloat16), pltpu.VMEM((tm, d), jnp.float32)],
        compiler_params=pltpu.CompilerParams(dimension_semantics=("arbitrary", "arbitrary")),
        name="ple_gate",
    )(x, ple, g_ple.reshape(1, d), wg_b, wp_b, g_post.reshape(1, d))


def _attn_kernel(heads, mode, *refs):
    if mode == 'mask':
        slopes_ref, q_ref, k_ref, v_ref, mask_ref, expand_ref, o_ref, kb_ref, vb_ref = refs
    elif mode == 'gate':
        slopes_ref, q_ref, k_ref, v_ref, expand_ref, o_ref, kb_ref, vb_ref, means_ref = refs
    else:
        slopes_ref, q_ref, k_ref, v_ref, ocmp_ref, oslc_ref, gates_ref, o_ref, kb_ref, vb_ref = refs
    s_idx = pl.program_id(1)
    i = pl.program_id(2)
    tq = ATTN_TILE
    n_heads = len(heads)
    t_total = k_ref.shape[2]
    lane = lax.broadcasted_iota(jnp.int32, (tq, LANES), 1)

    @pl.when(i == 0)
    def _():
        kb_ref[...] = k_ref[0].astype(jnp.bfloat16)
        vb_ref[...] = v_ref[0].astype(jnp.bfloat16)
        if mode == 'gate':
            sq = lax.broadcasted_iota(jnp.int32, (LANES, LANES), 1)
            means = jnp.zeros((LANES, LANES), jnp.float32)
            for n in range(t_total // MOBA_BLOCK):
                col = jnp.mean(k_ref[0, :, n * MOBA_BLOCK:(n + 1) * MOBA_BLOCK], axis=1, keepdims=True)
                means = jnp.where(sq == n, col, means)
            means_ref[...] = means.T

    sel_b = {}
    if mode == 'mask':
        for m in range(mask_ref.shape[2] // LANES):
            sel_b[m] = mask_ref[0, :, m * LANES:(m + 1) * LANES].astype(jnp.bfloat16)

    row = lax.broadcasted_iota(jnp.int32, (tq, tq), 0)
    col = lax.broadcasted_iota(jnp.int32, (tq, tq), 1)
    rc = (row - col).astype(jnp.float32)
    diag0 = pl.multiple_of(i * tq, tq)

    qbs, slopes = [], []
    for h, (q_off, kv_half, m_idx) in enumerate(heads):
        slopes.append(slopes_ref[s_idx * n_heads + h])
        chunk = q_off // LANES
        qc = q_ref[0, :, chunk * LANES:(chunk + 1) * LANES]
        if (q_off % LANES) // HEAD_DIM != kv_half:
            qc = pltpu.roll(qc, HEAD_DIM, 1)
        in_half = (lane >= HEAD_DIM) if kv_half else (lane < HEAD_DIM)
        qh = jnp.where(in_half, qc, 0.0)
        qbs.append((qh * SCALE).astype(jnp.bfloat16))
        if mode == 'gate':
            n_blk = t_total // MOBA_BLOCK
            gate = _nt(means_ref[...], qh, precision=lax.Precision.HIGHEST)[0:n_blk]
            blk = lax.broadcasted_iota(jnp.int32, (n_blk, tq), 0)
            gate = jnp.where(blk < i, gate, NEG)
            _, taken = _extract_top(gate, MOBA_TOPK)
            sel = ((taken == F32_MIN) & (gate > 0.5 * NEG)) | (blk == i)
            sel = jnp.concatenate([sel.astype(jnp.float32), jnp.zeros((LANES - n_blk, tq), jnp.float32)], axis=0)
            sel_b[m_idx] = sel.T.astype(jnp.bfloat16)

    def tile_step(start, dist, tile_keep, states):
        k_tile = kb_ref[:, pl.ds(start, tq)]
        v_tile = vb_ref[:, pl.ds(start, tq)]
        picked = {m: jnp.dot(sb, expand_ref[:, pl.ds(start, tq)], preferred_element_type=jnp.float32) > 0.5
                  for m, sb in sel_b.items()}
        new_states = []
        for h, (_, _, m_idx) in enumerate(heads):
            keep = picked[m_idx] if mode != 'window' else tile_keep
            if mode != 'window' and tile_keep is not None:
                keep = keep & tile_keep
            s = jnp.dot(qbs[h], k_tile, preferred_element_type=jnp.float32) - slopes[h] * dist
            s = jnp.where(keep, s, NEG)
            m_tile = jnp.max(s, axis=1, keepdims=True)
            if states is None:
                p = jnp.exp(s - m_tile)
                new_states += [m_tile, jnp.sum(p, axis=1, keepdims=True), _nt(p.astype(jnp.bfloat16), v_tile)]
            else:
                m_prev, l_prev, acc = states[3 * h:3 * h + 3]
                m_new = jnp.maximum(m_prev, m_tile)
                alpha = jnp.exp(m_prev - m_new)
                p = jnp.exp(s - m_new)
                new_states += [m_new, alpha * l_prev + jnp.sum(p, axis=1, keepdims=True),
                               alpha * acc + _nt(p.astype(jnp.bfloat16), v_tile)]
        return tuple(new_states)

    states = tile_step(diag0, rc, rc >= 0.0, None)

    def body(n, states):
        start = pl.multiple_of(n * tq, tq)
        dist = rc + ((i - n) * tq).astype(jnp.float32)
        tile_keep = None
        if mode == 'window':
            tile_keep = rc <= jnp.where(n == i - WINDOW // tq, 0.0, float(tq))
        return tile_step(start, dist, tile_keep, states)

    first_tile = jnp.maximum(i - WINDOW // tq, 0) if mode == 'window' else 0
    states = lax.fori_loop(first_tile, i, body, states)
    outs = []
    for h, (q_off, kv_half, _) in enumerate(heads):
        o = states[3 * h + 2] / states[3 * h + 1]
        if (q_off % LANES) // HEAD_DIM != kv_half:
            o = pltpu.roll(o, HEAD_DIM, 1)
        outs.append(o)

    for c in range(n_heads // 2):
        o = jnp.where(lane < HEAD_DIM, outs[2 * c], outs[2 * c + 1])
        if mode == 'window':
            g = gates_ref[0]

            def gate_of(branch):
                lo = g[:, 3 * (2 * c) + branch:3 * (2 * c) + branch + 1]
                hi = g[:, 3 * (2 * c + 1) + branch:3 * (2 * c + 1) + branch + 1]
                return jnp.where(lane < HEAD_DIM, lo, hi)

            sl = slice(c * LANES, (c + 1) * LANES)
            o = gate_of(0) * ocmp_ref[0, :, sl] + gate_of(1) * oslc_ref[0, :, sl] + gate_of(2) * o
        o_ref[0, :, c * LANES:(c + 1) * LANES] = o


def _attention(q, kv_t, slopes, *, mode, heads, k_row0, v_row0, mask=None, block=None, n_masks=0, combine=None):
    b, t, qcols = q.shape
    qw = (len(heads) // 2) * LANES
    n_steps = qcols // qw
    tq = ATTN_TILE
    assert mode != 'gate' or MOBA_BLOCK == tq
    q_spec = pl.BlockSpec((1, tq, qw), lambda bi, s, i: (bi, i, s))
    in_specs = [pl.BlockSpec(memory_space=pltpu.SMEM), q_spec,
                pl.BlockSpec((1, LANES, t), lambda bi, s, i: (bi, k_row0 + s, 0)),
                pl.BlockSpec((1, LANES, t), lambda bi, s, i: (bi, v_row0 + s, 0))]
    args = [slopes, q, kv_t, kv_t]
    scratch = [pltpu.VMEM((LANES, t), jnp.bfloat16), pltpu.VMEM((LANES, t), jnp.bfloat16)]
    if mode == 'mask':
        in_specs.append(pl.BlockSpec((1, tq, n_masks * LANES), lambda bi, s, i: (bi, i, s)))
        args.append(mask)
    if mode == 'window':
        in_specs += [q_spec, q_spec, pl.BlockSpec((1, tq, LANES), lambda bi, s, i: (bi, i, s))]
        args += list(combine)
    else:
        expand = (np.arange(LANES)[:, None] == (np.arange(t)[None, :] // block)).astype(np.float32)
        in_specs.append(pl.BlockSpec((LANES, t), lambda bi, s, i: (0, 0)))
        args.append(jnp.asarray(expand, jnp.bfloat16))
    if mode == 'gate':
        scratch.append(pltpu.VMEM((LANES, LANES), jnp.float32))
    return pl.pallas_call(
        functools.partial(_attn_kernel, heads, mode),
        grid=(b, n_steps, t // tq),
        in_specs=in_specs,
        out_specs=q_spec,
        out_shape=jax.ShapeDtypeStruct((b, t, qcols), jnp.float32),
        scratch_shapes=scratch,
        compiler_params=pltpu.CompilerParams(dimension_semantics=("arbitrary",) * 3),
        name="attention_" + mode,
    )(*args)


def _cmp_chunk_kernel(x_ref, wk_ref, wv_ref, abk_ref, abv_ref):
    @pl.when(pl.program_id(1) == 0)
    def _():
        abk_ref[...] = jnp.zeros_like(abk_ref)
        abv_ref[...] = jnp.zeros_like(abv_ref)

    x = x_ref[...]
    abk_ref[...] += jnp.dot(x[:, :NSA_KV_WIDTH].astype(jnp.bfloat16), wk_ref[0], preferred_element_type=jnp.float32)
    abv_ref[...] += jnp.dot(x[:, NSA_KV_WIDTH:].astype(jnp.bfloat16), wv_ref[0], preferred_element_type=jnp.float32)


def _cmp_chunks(rows, wk_bd, wv_bd):
    n = rows.shape[0]
    tm = next(c for c in (1024, 512, 256, 128) if n % c == 0)
    width = 2 * NSA_KV_WIDTH
    out = jax.ShapeDtypeStruct((n, width), jnp.float32)
    return pl.pallas_call(
        _cmp_chunk_kernel,
        grid=(n // tm, CMP_STRIDE),
        in_specs=[pl.BlockSpec((tm, width), lambda i, r: (i, r)),
                  pl.BlockSpec((1, NSA_KV_WIDTH, width), lambda i, r: (r, 0, 0)),
                  pl.BlockSpec((1, NSA_KV_WIDTH, width), lambda i, r: (r, 0, 0))],
        out_specs=[pl.BlockSpec((tm, width), lambda i, r: (i, 0)), pl.BlockSpec((tm, width), lambda i, r: (i, 0))],
        out_shape=[out, out],
        compiler_params=pltpu.CompilerParams(dimension_semantics=("arbitrary", "arbitrary")),
        name="cmp_chunks",
    )(rows, wk_bd, wv_bd)


def _cmp_weights(w1, w2, pe):
    w1r = w1.reshape(2, CMP_STRIDE, HEAD_DIM, CMP_HIDDEN)
    eye = jnp.eye(NSA_KV_HEADS, dtype=w1.dtype)
    w1_bd = jnp.einsum('gG,ardh->rgdaGh', eye, w1r).reshape(CMP_STRIDE, NSA_KV_WIDTH, 2 * NSA_KV_WIDTH)
    w2_bd = jnp.kron(eye, w2)
    bias = jnp.dot(pe.reshape(1, CMP_LEN * HEAD_DIM), w1, precision=lax.Precision.HIGHEST)
    return w1_bd.astype(jnp.bfloat16), w2_bd.astype(jnp.bfloat16), jnp.tile(bias, (1, NSA_KV_HEADS))


def _cmp_second_layer(ab, peb_ref, w2_ref):
    kvw = NSA_KV_WIDTH
    h = jax.nn.gelu(ab[:, :kvw] + pltpu.roll(ab[:, kvw:], N_CMP_CHUNKS - 1, 0) + peb_ref[...])
    return jnp.dot(h.astype(jnp.bfloat16), w2_ref[...], preferred_element_type=jnp.float32)


def _cmp_keys(ab, peb_ref, w2_ref, gk_ref, bd_ref):
    kc = _cmp_second_layer(ab, peb_ref, w2_ref)
    ms = jnp.dot(kc * kc, bd_ref[...], precision=lax.Precision.HIGHEST, preferred_element_type=jnp.float32)
    return kc * lax.rsqrt(ms + RMS_EPS) * gk_ref[...]


def _cmp_select_kernel(slopes_ref, q_ref, abk_ref, abv_ref, pebk_ref, pebv_ref, w2k_ref, w2v_ref, gk_ref, bd_ref,
                       ov_ref, o_ref, mask_ref, kcb_ref, vcb_ref):
    i = pl.program_id(1)
    tq = ATTN_TILE

    @pl.when(i == 0)
    def _():
        kcb_ref[...] = _cmp_keys(abk_ref[0], pebk_ref, w2k_ref, gk_ref, bd_ref).astype(jnp.bfloat16)
        vcb_ref[...] = _cmp_second_layer(abv_ref[0], pebv_ref, w2v_ref).astype(jnp.bfloat16)

    lane = lax.broadcasted_iota(jnp.int32, (tq, LANES), 1)
    t_pos = i * tq + lax.broadcasted_iota(jnp.int32, (tq, LANES), 0)
    dist_i = t_pos - (CMP_STRIDE * lane + CMP_LEN - 1)
    keep = (dist_i >= 0) & (lane < N_CMP_CHUNKS - 1)
    dist = dist_i.astype(jnp.float32)
    own = jnp.right_shift(t_pos, SEL_BLOCK.bit_length() - 1)
    outs = []
    for g in range(NSA_KV_HEADS):
        kchunk, khalf = divmod(g, 2)
        kc_g = kcb_ref[:, kchunk * LANES:(kchunk + 1) * LANES]
        vc_g = vcb_ref[:, kchunk * LANES:(kchunk + 1) * LANES]
        in_half = (lane >= HEAD_DIM) if khalf else (lane < HEAD_DIM)
        psum = None
        for r in range(NSA_GROUP):
            h = g * NSA_GROUP + r
            qc = q_ref[0, :, (h // 2) * LANES:(h // 2 + 1) * LANES]
            if h % 2 != khalf:
                qc = pltpu.roll(qc, HEAD_DIM, 1)
            qb = (jnp.where(in_half, qc, 0.0) * SCALE).astype(jnp.bfloat16)
            s = jnp.where(keep, _nt(qb, kc_g) - slopes_ref[h] * dist, NEG)
            m = jnp.max(s, axis=1, keepdims=True)
            e = jnp.where(keep, jnp.exp(s - m), 0.0)
            l = jnp.sum(e, axis=1, keepdims=True)
            p = e / jnp.where(l > 0.0, l, 1.0)
            o = jnp.dot(p.astype(jnp.bfloat16), vc_g, preferred_element_type=jnp.float32)
            if h % 2 != khalf:
                o = pltpu.roll(o, HEAD_DIM, 1)
            outs.append(o)
            psum = p if psum is None else psum + p
        imp = jnp.dot(psum, ov_ref[...], precision=lax.Precision.HIGHEST, preferred_element_type=jnp.float32)
        score = jnp.where(lane == own, OWN_SCORE, jnp.where(lane < own, imp, NEG))
        mask_ref[0, :, g * LANES:(g + 1) * LANES] = _top_lanes_by_rows(score, SEL_TOPK, N_CMP_CHUNKS * CMP_STRIDE // SEL_BLOCK)
    for c in range(NSA_HEADS // 2):
        o_ref[0, :, c * LANES:(c + 1) * LANES] = jnp.where(lane < HEAD_DIM, outs[2 * c], outs[2 * c + 1])


def _cmp_consts(cw):
    c_start = np.arange(LANES) * CMP_STRIDE
    j_start = np.arange(LANES) * SEL_BLOCK
    ov = ((c_start[:, None] < j_start[None, :] + SEL_BLOCK) & (c_start[:, None] + CMP_LEN > j_start[None, :])
          & (np.arange(LANES)[:, None] < N_CMP_CHUNKS - 1)).astype(np.float32)
    bd = np.kron(np.eye(NSA_KV_HEADS, dtype=np.float32), np.full((HEAD_DIM, HEAD_DIM), 1.0 / HEAD_DIM, np.float32))
    return [cw['pebk'], cw['pebv'], cw['w2k'], cw['w2v'], cw['gk'], jnp.asarray(bd), jnp.asarray(ov)]


def _cmp_select(q, abk, abv, slopes, cw):
    b, t, _ = q.shape
    tq = ATTN_TILE
    kvw = NSA_KV_WIDTH
    consts = _cmp_consts(cw)
    return pl.pallas_call(
        _cmp_select_kernel,
        grid=(b, t // tq),
        in_specs=[pl.BlockSpec(memory_space=pltpu.SMEM),
                  pl.BlockSpec((1, tq, NSA_WIDTH), lambda bi, i: (bi, i, 0)),
                  pl.BlockSpec((1, N_CMP_CHUNKS, 2 * kvw), lambda bi, i: (bi, 0, 0)),
                  pl.BlockSpec((1, N_CMP_CHUNKS, 2 * kvw), lambda bi, i: (bi, 0, 0))]
                 + [pl.BlockSpec(c.shape, lambda bi, i: (0, 0)) for c in consts],
        out_specs=[pl.BlockSpec((1, tq, NSA_WIDTH), lambda bi, i: (bi, i, 0)),
                   pl.BlockSpec((1, tq, NSA_KV_HEADS * LANES), lambda bi, i: (bi, i, 0))],
        out_shape=[jax.ShapeDtypeStruct((b, t, NSA_WIDTH), jnp.float32),
                   jax.ShapeDtypeStruct((b, t, NSA_KV_HEADS * LANES), jnp.float32)],
        scratch_shapes=[pltpu.VMEM((N_CMP_CHUNKS, kvw), jnp.bfloat16), pltpu.VMEM((N_CMP_CHUNKS, kvw), jnp.bfloat16)],
        compiler_params=pltpu.CompilerParams(dimension_semantics=("arbitrary", "arbitrary")),
        name="cmp_select",
    )(slopes, q, abk, abv, *consts)


def _pages_chunks_kernel(pt_ref, *refs):
    x_refs, (o_ref, nat_ref) = refs[:-2], refs[-2:]
    n_col = nat_ref.shape[0] // PAGE_SIZE
    width = n_col * LANES
    for j, x_ref in enumerate(x_refs):
        for c in range(n_col):
            nat_ref[c * PAGE_SIZE:(c + 1) * PAGE_SIZE, :] = x_ref[0, c * LANES:(c + 1) * LANES, :].T
        for r in range(CMP_STRIDE):
            for c in range(n_col):
                o_ref[j * CHUNKS_PER_PAGE:(j + 1) * CHUNKS_PER_PAGE, r * width + c * LANES:r * width + (c + 1) * LANES] = (
                    nat_ref[pl.ds(c * PAGE_SIZE + r, CHUNKS_PER_PAGE, stride=CMP_STRIDE), :])


def _pages_chunks(page_table, cache_t):
    nseq = page_table.shape[0]
    per = 8
    width = 2 * NSA_KV_WIDTH
    groups = N_PAGES // per

    def page_index(j):
        return lambda b, s, pt: (pt[b, s * per + j], 0, 0)

    grid_spec = pltpu.PrefetchScalarGridSpec(
        num_scalar_prefetch=1,
        grid=(nseq, groups),
        in_specs=[pl.BlockSpec((1, width, PAGE_SIZE), page_index(j)) for j in range(per)],
        out_specs=pl.BlockSpec((per * CHUNKS_PER_PAGE, CMP_STRIDE * width), lambda b, s, pt: (b * groups + s, 0)),
        scratch_shapes=[pltpu.VMEM((width // LANES * PAGE_SIZE, LANES), jnp.float32)])
    return pl.pallas_call(
        _pages_chunks_kernel,
        grid_spec=grid_spec,
        out_shape=jax.ShapeDtypeStruct((nseq * N_CMP_CHUNKS, CMP_STRIDE * width), jnp.float32),
        compiler_params=pltpu.CompilerParams(dimension_semantics=("arbitrary", "arbitrary")),
        name="pages_chunks",
    )(page_table, *([cache_t] * per))


def _one_query_softmax(s, keep, s_own):
    s = jnp.where(keep, s, NEG)
    m = jnp.maximum(jnp.max(s, axis=1, keepdims=True), s_own)
    e = jnp.where(keep, jnp.exp(s - m), 0.0)
    e_own = jnp.exp(s_own - m)
    inv = 1.0 / (jnp.sum(e, axis=1, keepdims=True) + e_own)
    return e * inv, e_own * inv


def _own_score(qb, k_row):
    return jnp.sum(qb.astype(jnp.float32) * k_row.astype(jnp.bfloat16).astype(jnp.float32), axis=1, keepdims=True)


def _nsa_sample_kernel(pt_ref, qb_ref, *refs):
    pps = PAGES_PER_STEP
    page_refs, abk_refs, abv_refs = refs[:pps], refs[pps:2 * pps], refs[2 * pps:3 * pps]
    (win_ref, new_ref, gates_ref, slopes_ref, pebk_ref, pebv_ref, w2k_ref, w2v_ref, gk_ref, bd_ref, ov_ref,
     grp_ref, eslc_ref, hmask_ref, o_ref, s_all, v_all, abk_all, abv_all) = refs[3 * pps:]
    p = pl.program_id(1)
    kvw = NSA_KV_WIDTH
    qb = qb_ref[0].astype(jnp.bfloat16)
    for j in range(pps):
        lane0 = pl.multiple_of((p * pps + j) * PAGE_SIZE, PAGE_SIZE)
        page = page_refs[j][0]
        s_all[:, pl.ds(lane0, PAGE_SIZE)] = jnp.dot(qb, page[:kvw].astype(jnp.bfloat16),
                                                    preferred_element_type=jnp.float32)
        v_all[:, pl.ds(lane0, PAGE_SIZE)] = page[kvw:].astype(jnp.bfloat16)
        c0 = pl.multiple_of((p * pps + j) * CHUNKS_PER_PAGE, CHUNKS_PER_PAGE)
        abk_all[pl.ds(c0, CHUNKS_PER_PAGE), :] = abk_refs[j][0]
        abv_all[pl.ds(c0, CHUNKS_PER_PAGE), :] = abv_refs[j][0]

    @pl.when(p == N_PAGES // pps - 1)
    def _():
        slopes = slopes_ref[...]
        new = new_ref[0]
        lane = lax.broadcasted_iota(jnp.int32, (NSA_HEADS, LANES), 1)

        def fold(full):
            m = full * hmask_ref[...]
            return m[0:4] + m[4:8] + m[8:12] + m[12:16]

        kc = _cmp_keys(abk_all[...], pebk_ref, w2k_ref, gk_ref, bd_ref)
        vc = _cmp_second_layer(abv_all[...], pebv_ref, w2v_ref)
        dist_c = (PAST_LEN - (CMP_LEN - 1) - CMP_STRIDE * lane).astype(jnp.float32)
        keep_c = lane < N_CMP_CHUNKS - 1
        s = jnp.where(keep_c, _nt(qb, kc.astype(jnp.bfloat16)) - slopes * dist_c, NEG)
        e = jnp.where(keep_c, jnp.exp(s - jnp.max(s, axis=1, keepdims=True)), 0.0)
        p_cmp = e / jnp.sum(e, axis=1, keepdims=True)
        o_cmp = jnp.dot(p_cmp.astype(jnp.bfloat16), vc.astype(jnp.bfloat16), preferred_element_type=jnp.float32)

        imp = jnp.dot(p_cmp, ov_ref[...], precision=lax.Precision.HIGHEST, preferred_element_type=jnp.float32)
        imp = jnp.dot(grp_ref[...], imp, precision=lax.Precision.HIGHEST, preferred_element_type=jnp.float32)
        sel = _top_lanes_by_rows(imp, SEL_TOPK - 1, PAST_LEN // SEL_BLOCK)
        picked = jnp.dot(sel.astype(jnp.bfloat16), eslc_ref[...], preferred_element_type=jnp.float32) > 0.5
        kpos = lax.broadcasted_iota(jnp.int32, (NSA_HEADS, PAST_LEN), 1)
        s = s_all[...] - slopes * (PAST_LEN - kpos).astype(jnp.float32)
        prob, p_own = _one_query_softmax(s, picked, _own_score(qb, new[:, 0:kvw]))
        o_slc = _nt(prob.astype(jnp.bfloat16), v_all[...]) + p_own * new[:, kvw:2 * kvw]

        win = win_ref[0]
        wpos = lax.broadcasted_iota(jnp.int32, (NSA_HEADS, WINDOW), 1)
        s = (jnp.dot(qb, win[:kvw].astype(jnp.bfloat16), preferred_element_type=jnp.float32)
             - slopes * (WINDOW - wpos).astype(jnp.float32))
        prob, p_own = _one_query_softmax(s, wpos >= 0, _own_score(qb, new[:, 2 * kvw:3 * kvw]))
        o_win = _nt(prob.astype(jnp.bfloat16), win[kvw:].astype(jnp.bfloat16)) + p_own * new[:, 3 * kvw:4 * kvw]

        g = gates_ref[0]
        o_ref[0] = g[0] * fold(o_cmp) + g[1] * fold(o_slc) + g[2] * fold(o_win)


def _moba_sample_kernel(pt_ref, qb_ref, *refs):
    pps = PAGES_PER_STEP
    page_refs = refs[:pps]
    new_ref, slopes_ref, emoba_ref, hmask_ref, o_ref, s_all, v_all, ksum_ref = refs[pps:]
    p = pl.program_id(1)
    w = MOBA_WIDTH
    qf = qb_ref[0]
    qb = qf.astype(jnp.bfloat16)

    per_blk = MOBA_BLOCK // PAGE_SIZE
    k_sum = None
    for j in range(pps):
        page_no = p * pps + j
        lane0 = pl.multiple_of(page_no * PAGE_SIZE, PAGE_SIZE)
        page = page_refs[j][0]
        k_t = page[:w]
        s_all[:, pl.ds(lane0, PAGE_SIZE)] = jnp.dot(qb, k_t.astype(jnp.bfloat16), preferred_element_type=jnp.float32)
        v_all[:, pl.ds(lane0, PAGE_SIZE)] = page[w:].astype(jnp.bfloat16)
        k_sum = k_t if j % per_blk == 0 else k_sum + k_t
        if j % per_blk == per_blk - 1:
            ksum_ref[p * (pps // per_blk) + j // per_blk] = k_sum

    @pl.when(p == N_PAGES // pps - 1)
    def _():
        slopes = slopes_ref[...]
        new = new_ref[0]
        sq = lax.broadcasted_iota(jnp.int32, (w, LANES), 1)
        ksum = jnp.zeros((w, LANES), jnp.float32)
        for n in range(PAST_LEN // MOBA_BLOCK):
            ksum = jnp.where(sq == n, jnp.sum(ksum_ref[n], axis=1, keepdims=True), ksum)
        gate = jnp.dot(qf, ksum, precision=lax.Precision.HIGHEST, preferred_element_type=jnp.float32)
        sel = _top_lanes_by_rows(gate, MOBA_TOPK, PAST_LEN // MOBA_BLOCK)
        picked = jnp.dot(sel.astype(jnp.bfloat16), emoba_ref[...], preferred_element_type=jnp.float32) > 0.5
        kpos = lax.broadcasted_iota(jnp.int32, (MOBA_HEADS, PAST_LEN), 1)
        s = s_all[...] - slopes * (PAST_LEN - kpos).astype(jnp.float32)
        prob, p_own = _one_query_softmax(s, picked, _own_score(qb, new[:, :w]))
        full = _nt(prob.astype(jnp.bfloat16), v_all[...]) + p_own * new[:, w:]
        o_ref[0] = jnp.sum(full * hmask_ref[...], axis=0, keepdims=True)


def _head_rows(q, lane_head):
    n = int(max(lane_head)) + 1
    onehot = jnp.asarray(np.eye(n, dtype=np.float32)[np.asarray(lane_head)])
    return jnp.einsum('bhd,hn->bhnd', q, onehot).reshape(q.shape[0], q.shape[1], n * HEAD_DIM)


def _block_expand_t(block):
    e = (np.arange(LANES)[:, None] == (np.arange(PAST_LEN)[None, :] // block)).astype(np.float32)
    return jnp.asarray(e, jnp.bfloat16)


def _const_spec(shape):
    nd = len(shape)
    return pl.BlockSpec(shape, lambda b, p, pt: (0,) * nd)


def _page_index(j, row_block):
    return lambda b, p, pt: (pt[b, p * PAGES_PER_STEP + j], row_block, 0)


def _context_page_index(j):
    return lambda b, p, pt: (b * N_PAGES + p * PAGES_PER_STEP + j, 0, 0)


def _nsa_sample(page_table, q, cache_t, abk, abv, win_t, new, gates, slopes, cw):
    nseq = q.shape[0]
    kvw = NSA_KV_WIDTH
    pps = PAGES_PER_STEP
    group_of_head = [h // NSA_GROUP for h in range(NSA_HEADS)]
    qb = _head_rows(q * SCALE, group_of_head)
    hmask = _head_rows(jnp.ones((1, NSA_HEADS, HEAD_DIM), jnp.float32), group_of_head)[0]
    hh = np.arange(NSA_HEADS)
    grp = ((hh[:, None] // NSA_GROUP) == (hh[None, :] // NSA_GROUP)).astype(np.float32)
    consts = [slopes.reshape(NSA_HEADS, 1)] + _cmp_consts(cw) + [jnp.asarray(grp), _block_expand_t(SEL_BLOCK), hmask]
    grid_spec = pltpu.PrefetchScalarGridSpec(
        num_scalar_prefetch=1,
        grid=(nseq, N_PAGES // pps),
        in_specs=[pl.BlockSpec((1, NSA_HEADS, kvw), lambda b, p, pt: (b, 0, 0))]
                 + [pl.BlockSpec((1, 2 * kvw, PAGE_SIZE), _page_index(j, 1)) for j in range(pps)]
                 + [pl.BlockSpec((1, CHUNKS_PER_PAGE, 2 * kvw), _context_page_index(j)) for j in range(pps)] * 2
                 + [pl.BlockSpec((1, 2 * kvw, WINDOW), lambda b, p, pt: (b, 0, 0)),
                    pl.BlockSpec((1, 1, 4 * kvw), lambda b, p, pt: (b, 0, 0)),
                    pl.BlockSpec((1, 3, NSA_GROUP, kvw), lambda b, p, pt: (b, 0, 0, 0))]
                 + [_const_spec(c.shape) for c in consts],
        out_specs=pl.BlockSpec((1, NSA_GROUP, kvw), lambda b, p, pt: (b, 0, 0)),
        scratch_shapes=[pltpu.VMEM((NSA_HEADS, PAST_LEN), jnp.float32),
                        pltpu.VMEM((kvw, PAST_LEN), jnp.bfloat16),
                        pltpu.VMEM((N_CMP_CHUNKS, 2 * kvw), jnp.float32),
                        pltpu.VMEM((N_CMP_CHUNKS, 2 * kvw), jnp.float32)])
    return pl.pallas_call(
        _nsa_sample_kernel,
        grid_spec=grid_spec,
        out_shape=jax.ShapeDtypeStruct((nseq, NSA_GROUP, kvw), jnp.float32),
        compiler_params=pltpu.CompilerParams(dimension_semantics=("arbitrary", "arbitrary")),
        name="nsa_sample",
    )(page_table, qb, *([cache_t] * pps + [abk] * pps + [abv] * pps), win_t, new, gates, *consts)


def _moba_sample(page_table, q, cache_t, new, slopes):
    nseq = q.shape[0]
    w = MOBA_WIDTH
    pps = PAGES_PER_STEP
    own_head = list(range(MOBA_HEADS))
    qb = _head_rows(q * SCALE, own_head)
    hmask = _head_rows(jnp.ones((1, MOBA_HEADS, HEAD_DIM), jnp.float32), own_head)[0]
    consts = [slopes.reshape(MOBA_HEADS, 1), _block_expand_t(MOBA_BLOCK), hmask]
    grid_spec = pltpu.PrefetchScalarGridSpec(
        num_scalar_prefetch=1,
        grid=(nseq, N_PAGES // pps),
        in_specs=[pl.BlockSpec((1, MOBA_HEADS, w), lambda b, p, pt: (b, 0, 0))]
                 + [pl.BlockSpec((1, 2 * w, PAGE_SIZE), _page_index(j, 0)) for j in range(pps)]
                 + [pl.BlockSpec((1, 1, 2 * w), lambda b, p, pt: (b, 0, 0))]
                 + [_const_spec(c.shape) for c in consts],
        out_specs=pl.BlockSpec((1, 1, w), lambda b, p, pt: (b, 0, 0)),
        scratch_shapes=[pltpu.VMEM((MOBA_HEADS, PAST_LEN), jnp.float32),
                        pltpu.VMEM((w, PAST_LEN), jnp.bfloat16),
                        pltpu.VMEM((PAST_LEN // MOBA_BLOCK, w, LANES), jnp.float32)])
    return pl.pallas_call(
        _moba_sample_kernel,
        grid_spec=grid_spec,
        out_shape=jax.ShapeDtypeStruct((nseq, 1, w), jnp.float32),
        compiler_params=pltpu.CompilerParams(dimension_semantics=("arbitrary", "arbitrary"),
                                             vmem_limit_bytes=VMEM_LIMIT),
        name="moba_sample",
    )(page_table, qb, *([cache_t] * pps), new, *consts)


def _pick_top(s_ref, vals_ref, rows_ref, k):
    r_total, width = s_ref.shape
    cw = min(2 * LANES, width)
    kid = lax.broadcasted_iota(jnp.int32, (k, cw), 0)
    rowid = lax.broadcasted_iota(jnp.int32, (r_total, cw), 0)

    def one_pass(ci, _):
        col = pl.multiple_of(ci * cw, cw)

        def body(r, carry):
            w, vals, rows = carry
            mx = jnp.max(w, axis=0, keepdims=True)
            first = jnp.min(jnp.where(w == mx, rowid, r_total), axis=0, keepdims=True)
            w = jnp.where(rowid == first, F32_MIN, w)
            return w, jnp.where(kid == r, mx, vals), jnp.where(kid == r, first, rows)

        init = (s_ref[:, pl.ds(col, cw)], jnp.zeros((k, cw), jnp.float32), jnp.zeros((k, cw), jnp.int32))
        _, vals, rows = lax.fori_loop(0, k, body, init)
        vals_ref[:, pl.ds(col, cw)] = vals
        rows_ref[:, pl.ds(col, cw)] = rows
        return 0

    lax.fori_loop(0, width // cw, one_pass, 0)
    return vals_ref[...], rows_ref[...]


def _peer_route_kernel(x_ref, g_ref, wq_ref, subk_ref, xt_ref, c0_ref, e0_ref, r1_ref, e1_ref, sc_ref, cand_ref,
                       vals_ref, rows_ref, best_ref, crows_ref):
    x = x_ref[...]
    xn = x * lax.rsqrt(jnp.mean(x * x, axis=-1, keepdims=True) + RMS_EPS) * g_ref[...]
    xb = xn.astype(jnp.bfloat16)
    xt_ref[...] = xn.T.astype(jnp.bfloat16)
    t = x.shape[0]
    q = jnp.dot(xb, wq_ref[...], preferred_element_type=jnp.float32)
    half = PEER_QDIM // 2
    k = PEER_TOPK
    for j in range(2 * PEER_HEADS):
        sc_ref[:, j * t:(j + 1) * t] = _nt(q[:, j * half:(j + 1) * half], subk_ref[j],
                                           precision=lax.Precision.HIGHEST).T
    vals, rows = _pick_top(sc_ref, vals_ref, rows_ref, k)
    for p in range(PEER_HEADS):
        av = vals[:, (2 * p) * t:(2 * p + 1) * t]
        bv = vals[:, (2 * p + 1) * t:(2 * p + 2) * t]
        for a in range(k):
            lo, n_b = PEER_CAND_START[a], PEER_CAND_START[a + 1] - PEER_CAND_START[a]
            cand_ref[lo:lo + n_b, p * t:(p + 1) * t] = av[a:a + 1, :] + bv[0:n_b]
        n_real = PEER_CAND_START[k]
        cand_ref[n_real:, p * t:(p + 1) * t] = jnp.full((cand_ref.shape[0] - n_real, t), F32_MIN, jnp.float32)
    best, crows = _pick_top(cand_ref, best_ref, crows_ref, k)
    zsum = jnp.sum(jnp.exp(best - best[0:1, :]), axis=0, keepdims=True)
    win_a = jnp.zeros(crows.shape, jnp.int32)
    for a in range(1, k):
        win_a = win_a + (crows >= PEER_CAND_START[a]).astype(jnp.int32)
    keyid = lax.broadcasted_iota(jnp.int32, (PEER_KEYS, t), 0)
    for p in range(PEER_HEADS):
        lo0, lo1 = (2 * p) * t, (2 * p + 1) * t
        s0, s1 = sc_ref[:, lo0:lo0 + t], sc_ref[:, lo1:lo1 + t]
        c0 = jnp.zeros((PEER_KEYS, t), jnp.float32)
        r1 = jnp.full((PEER_KEYS, t), float(k), jnp.float32)
        for a in range(k):
            cnt_a = jnp.sum((win_a[:, p * t:(p + 1) * t] == a).astype(jnp.float32), axis=0, keepdims=True)
            c0 = jnp.where(keyid == rows[a:a + 1, lo0:lo0 + t], cnt_a, c0)
            r1 = jnp.where(keyid == rows[a:a + 1, lo1:lo1 + t], float(a), r1)
        c0_ref[p] = c0
        e0_ref[p] = jnp.where(c0 > 0.0, jnp.exp(s0 - vals[0:1, lo0:lo0 + t]), 0.0) / zsum[:, p * t:(p + 1) * t]
        r1_ref[p] = r1
        e1_ref[p] = jnp.where(r1 < k, jnp.exp(s1 - vals[0:1, lo1:lo1 + t]), 0.0)


def _peer_expert_kernel(x_ref, xt_ref, c0_ref, e0_ref, r1_ref, e1_ref, u_ref, vt_ref, o_ref, acc_ref, h_ref):
    c = pl.program_id(1)
    per = PEER_CHUNK // PEER_KEYS

    @pl.when(c == 0)
    def _():
        acc_ref[...] = jnp.zeros_like(acc_ref)

    ga = jax.nn.gelu(jnp.dot(u_ref[...], xt_ref[...], preferred_element_type=jnp.float32))
    for k in range(per):
        i1 = c * per + k
        wt = None
        for p in range(PEER_HEADS):
            row_c = c0_ref[p, pl.ds(i1, 1), :]
            row_e = e0_ref[p, pl.ds(i1, 1), :]
            term = jnp.where(r1_ref[p] < row_c, row_e * e1_ref[p], 0.0)
            wt = term if wt is None else wt + term
        rows = slice(k * PEER_KEYS, (k + 1) * PEER_KEYS)
        h_ref[rows, :] = (wt * ga[rows]).astype(jnp.bfloat16)
    acc_ref[...] += jnp.dot(vt_ref[...], h_ref[...], preferred_element_type=jnp.float32)

    @pl.when(c == pl.num_programs(1) - 1)
    def _():
        o_ref[...] = x_ref[...] + acc_ref[...].T


def _peer_residual(x, g_ffn, wq_b, subk, u_b, vt_b):
    n, d = x.shape
    t1 = 256 if n % 256 == 0 else n
    hp = PEER_HEADS
    tab = jax.ShapeDtypeStruct((hp, PEER_KEYS, n), jnp.float32)
    tab_spec = pl.BlockSpec((hp, PEER_KEYS, t1), lambda i: (0, 0, i))
    xt, c0, e0, r1, e1 = pl.pallas_call(
        _peer_route_kernel,
        grid=(n // t1,),
        in_specs=[pl.BlockSpec((t1, d), lambda i: (i, 0)),
                  pl.BlockSpec((1, d), lambda i: (0, 0)),
                  pl.BlockSpec((d, hp * PEER_QDIM), lambda i: (0, 0)),
                  pl.BlockSpec((2 * hp, PEER_KEYS, PEER_QDIM // 2), lambda i: (0, 0, 0))],
        out_specs=[pl.BlockSpec((d, t1), lambda i: (0, i)), tab_spec, tab_spec, tab_spec, tab_spec],
        out_shape=[jax.ShapeDtypeStruct((d, n), jnp.bfloat16), tab, tab, tab, tab],
        scratch_shapes=[pltpu.VMEM((PEER_KEYS, 2 * hp * t1), jnp.float32),
                        pltpu.VMEM((-(-PEER_CAND_START[PEER_TOPK] // 8) * 8, hp * t1), jnp.float32),
                        pltpu.VMEM((PEER_TOPK, 2 * hp * t1), jnp.float32),
                        pltpu.VMEM((PEER_TOPK, 2 * hp * t1), jnp.int32),
                        pltpu.VMEM((PEER_TOPK, hp * t1), jnp.float32),
                        pltpu.VMEM((PEER_TOPK, hp * t1), jnp.int32)],
        compiler_params=pltpu.CompilerParams(dimension_semantics=("arbitrary",), vmem_limit_bytes=VMEM_LIMIT),
        name="peer_route",
    )(x, g_ffn.reshape(1, d), wq_b, subk.reshape(2 * hp, PEER_KEYS, PEER_QDIM // 2))
    t2 = min(ROW_TILE, n)
    once = pl.Buffered(1)
    tab_spec2 = pl.BlockSpec((hp, PEER_KEYS, t2), lambda i, c: (0, 0, i), pipeline_mode=once)
    return pl.pallas_call(
        _peer_expert_kernel,
        grid=(n // t2, PEER_EXPERTS // PEER_CHUNK),
        in_specs=[pl.BlockSpec((t2, d), lambda i, c: (i, 0), pipeline_mode=once),
                  pl.BlockSpec((d, t2), lambda i, c: (0, i), pipeline_mode=once),
                  tab_spec2, tab_spec2, tab_spec2, tab_spec2,
                  pl.BlockSpec((PEER_CHUNK, d), lambda i, c: (c, 0)),
                  pl.BlockSpec((d, PEER_CHUNK), lambda i, c: (0, c))],
        out_specs=pl.BlockSpec((t2, d), lambda i, c: (i, 0)),
        out_shape=jax.ShapeDtypeStruct((n, d), jnp.float32),
        scratch_shapes=[pltpu.VMEM((d, t2), jnp.float32), pltpu.VMEM((PEER_CHUNK, t2), jnp.bfloat16)],
        compiler_params=pltpu.CompilerParams(dimension_semantics=("arbitrary", "arbitrary"),
                                             vmem_limit_bytes=VMEM_LIMIT),
        name="peer_experts",
    )(x, xt, c0, e0, r1, e1, u_b, vt_b)


def _alibi_slopes():
    n = NSA_HEADS + MOBA_HEADS
    s = jnp.exp2(-8.0 * jnp.arange(1, n + 1, dtype=jnp.float32) / n)
    return s[0::2], s[1::2]


def _in_proj_weights(w_in, gains):
    w_t = jnp.transpose(w_in).astype(jnp.bfloat16)
    kvw = NSA_KV_WIDTH
    edges = np.cumsum([0, NSA_WIDTH] + [kvw] * 6 + [N_GATE] + [MOBA_WIDTH] * 3)
    q_n, kc, vc, ks, vs, kw, vw, gt, q_m, k_m, v_m = [w_t[a:b] for a, b in zip(edges[:-1], edges[1:])]

    def gain(key, n_heads):
        return jnp.tile(gains[key].reshape(HEAD_DIM), n_heads)

    def raw(n):
        return jnp.ones((n,), jnp.float32)

    def pack(parts, gain_parts, flags):
        flag = jnp.concatenate([jnp.full((p.shape[0],), f, jnp.float32) for p, f in zip(parts, flags)])
        return jnp.concatenate(parts, axis=0), jnp.concatenate(gain_parts).reshape(-1, 1), flag.reshape(-1, 1)

    per_step = N_GATE // 2
    pad = jnp.zeros((LANES - per_step, D_MODEL), jnp.bfloat16)
    gt_rows = [gt[:per_step], pad, gt[per_step:], pad]
    return {
        'q_n': pack([q_n], [gain('g_q_nsa', NSA_HEADS)], [1.0]),
        'nsa': pack([kc, vc, ks, vs], [raw(kvw), raw(kvw), gain('g_k_slc', NSA_KV_HEADS), raw(kvw)], [0., 0., 1., 0.]),
        'win': pack([kw, vw], [gain('g_k_win', NSA_KV_HEADS), raw(kvw)], [1., 0.]),
        'cmp': pack([kc, vc], [raw(kvw), raw(kvw)], [0., 0.]),
        'gates': pack(gt_rows, [raw(2 * LANES)], [0., 0., 0., 0.]),
        'q_m': pack([q_m], [gain('g_q_moba', MOBA_HEADS)], [1.0]),
        'moba': pack([k_m, v_m], [gain('g_k_moba', MOBA_HEADS), raw(MOBA_WIDTH)], [1., 0.]),
    }


def _in_proj(x, g_mix, pw, batch):
    h_t = _norm_transpose(x, g_mix)
    out = {}
    for key in ('nsa', 'win', 'moba'):
        out[key] = _proj_t(h_t, *pw[key], batch=batch, rows=512)
    for key in ('q_n', 'q_m', 'cmp'):
        out[key] = _proj_t(h_t, *pw[key], batch=batch, rows=512, natural=True)
    out['gates'] = _proj_t(h_t, *pw['gates'], batch=batch, rows=2 * LANES, natural=True, act='sigmoid')
    return out


def _rows_leaf(rows_t, kinds, heads):
    b, _, t = rows_t.shape
    return jnp.transpose(rows_t.reshape(b, kinds, heads, HEAD_DIM, t), (0, 4, 1, 2, 3))[None]


def kernel(x_prompt, x_sample, cache_nsa, cache_moba, state_win, page_table, p_prompt, p_sample,
           g_mix, w_in, g_q_nsa, g_k_cmp, g_k_slc, g_k_win, g_q_moba, g_k_moba,
           cmp_pe_k, cmp_w1_k, cmp_w2_k, cmp_pe_v, cmp_w1_v, cmp_w2_v,
           g_out_nsa, g_out_moba, w_out, g_ffn, w_peer_q, peer_sub_keys, peer_u, peer_v,
           g_ple, w_ple_gate, w_ple_proj, g_ple_post):
    assert w_in.shape[0] == 1, "single-layer trunk"
    slopes_nsa, slopes_moba = _alibi_slopes()
    kvw = NSA_KV_WIDTH
    bf = jnp.bfloat16
    pw = _in_proj_weights(w_in[0], {'g_q_nsa': g_q_nsa[0], 'g_k_slc': g_k_slc[0], 'g_k_win': g_k_win[0],
                                    'g_q_moba': g_q_moba[0], 'g_k_moba': g_k_moba[0]})
    cw = {}
    w1k_bd, cw['w2k'], cw['pebk'] = _cmp_weights(cmp_w1_k[0], cmp_w2_k[0], cmp_pe_k[0])
    w1v_bd, cw['w2v'], cw['pebv'] = _cmp_weights(cmp_w1_v[0], cmp_w2_v[0], cmp_pe_v[0])
    cw['gk'] = jnp.tile(g_k_cmp[0].reshape(1, HEAD_DIM), (1, NSA_KV_HEADS))
    w_out_b, wq_b, u_b = w_out[0].astype(bf), w_peer_q[0].astype(bf), peer_u[0].astype(bf)
    vt_b = jnp.transpose(peer_v[0]).astype(bf)
    wg_b, wp_b = w_ple_gate[0].astype(bf), w_ple_proj[0].astype(bf)

    def tail(x, o_nsa, o_moba, ple):
        x = _out_proj(o_nsa, o_moba, x, g_out_nsa[0], g_out_moba[0], w_out_b)
        x = _peer_residual(x, g_ffn[0], wq_b, peer_sub_keys[0], u_b, vt_b)
        return _ple(x, ple, g_ple[0], wg_b, wp_b, g_ple_post[0])

    b, t, d = x_prompt.shape
    xp = x_prompt.reshape(b * t, d)
    pr = _in_proj(xp, g_mix[0], pw, b)
    q_n = pr['q_n'].reshape(b, t, NSA_WIDTH)
    abk, abv = _cmp_chunks(pr['cmp'].reshape(b * t // CMP_STRIDE, CMP_STRIDE * 2 * kvw), w1k_bd, w1v_bd)
    o_cmp, mask = _cmp_select(q_n, abk.reshape(b, t // CMP_STRIDE, 2 * kvw), abv.reshape(b, t // CMP_STRIDE, 2 * kvw),
                              slopes_nsa, cw)
    o_slc = _attention(q_n, pr['nsa'], slopes_nsa, mode='mask', heads=NSA_STEP_HEADS, k_row0=4, v_row0=6,
                       mask=mask, block=SEL_BLOCK, n_masks=2)
    o_nsa = _attention(q_n, pr['win'], slopes_nsa, mode='window', heads=NSA_STEP_HEADS, k_row0=0, v_row0=2,
                       combine=(o_cmp, o_slc, pr['gates'].reshape(b, t, 2 * LANES)))
    o_moba = _attention(pr['q_m'].reshape(b, t, MOBA_WIDTH), pr['moba'], slopes_moba, mode='gate',
                        heads=MOBA_STEP_HEADS, k_row0=0, v_row0=MOBA_WIDTH // LANES, block=MOBA_BLOCK)
    y_prompt = tail(xp, o_nsa.reshape(b * t, NSA_WIDTH), o_moba.reshape(b * t, MOBA_WIDTH),
                    p_prompt[0].reshape(b * t, PLE_DIM)).reshape(b, t, d)

    ns = x_sample.shape[0]
    xs = x_sample.reshape(ns, d)
    sr = _in_proj(xs, g_mix[0], pw, 1)
    n_phys = cache_nsa.shape[1]
    cache_n_t = jnp.transpose(cache_nsa[0], (0, 2, 3, 4, 1)).reshape(n_phys, 4 * kvw, PAGE_SIZE)
    cache_m_t = jnp.transpose(cache_moba[0], (0, 2, 3, 4, 1)).reshape(n_phys, 2 * MOBA_WIDTH, PAGE_SIZE)
    win_t = jnp.transpose(state_win[0], (0, 2, 3, 4, 1)).reshape(ns, 2 * kvw, WINDOW)
    abk, abv = _cmp_chunks(_pages_chunks(page_table, cache_n_t), w1k_bd, w1v_bd)
    nsa_new = jnp.transpose(sr['nsa'][0])
    win_new = jnp.transpose(sr['win'][0])
    moba_new = jnp.transpose(sr['moba'][0])
    new_n = jnp.concatenate([nsa_new[:, 2 * kvw:], win_new], axis=1)[:, None, :]
    gs = sr['gates']
    gs = jnp.concatenate([gs[:, :N_GATE // 2], gs[:, LANES:LANES + N_GATE // 2]], axis=1)
    g3 = jnp.transpose(gs.reshape(ns, NSA_KV_HEADS, NSA_GROUP, 3), (0, 3, 2, 1))[..., None]
    g3 = jnp.broadcast_to(g3, (ns, 3, NSA_GROUP, NSA_KV_HEADS, HEAD_DIM)).reshape(ns, 3, NSA_GROUP, kvw)
    o_nsa_s = _nsa_sample(page_table, sr['q_n'].reshape(ns, NSA_HEADS, HEAD_DIM), cache_n_t,
                          abk.reshape(ns * N_PAGES, CHUNKS_PER_PAGE, 2 * kvw),
                          abv.reshape(ns * N_PAGES, CHUNKS_PER_PAGE, 2 * kvw),
                          win_t, new_n, g3, slopes_nsa, cw)
    o_nsa_s = jnp.transpose(o_nsa_s.reshape(ns, NSA_GROUP, NSA_KV_HEADS, HEAD_DIM), (0, 2, 1, 3)).reshape(ns, NSA_WIDTH)
    o_moba_s = _moba_sample(page_table, sr['q_m'].reshape(ns, MOBA_HEADS, HEAD_DIM), cache_m_t,
                            moba_new[:, None, :], slopes_moba).reshape(ns, MOBA_WIDTH)
    y_sample = tail(xs, o_nsa_s, o_moba_s, p_sample[0].reshape(ns, PLE_DIM)).reshape(ns, 1, d)

    win_prompt = _rows_leaf(pr['win'][:, :, t - WINDOW:], 2, NSA_KV_HEADS)
    nsa_rows_s = nsa_new.reshape(1, ns, 1, 4, NSA_KV_HEADS, HEAD_DIM)
    moba_rows_s = moba_new.reshape(1, ns, 1, 2, MOBA_HEADS, HEAD_DIM)
    win_s = jnp.concatenate([state_win[0][:, 1:], win_new.reshape(ns, 1, 2, NSA_KV_HEADS, HEAD_DIM)], axis=1)[None]
    return (y_prompt, y_sample, _rows_leaf(pr['nsa'], 4, NSA_KV_HEADS), _rows_leaf(pr['moba'], 2, MOBA_HEADS),
            win_prompt, nsa_rows_s, moba_rows_s, win_s)
```
